```python
import math
import jax, jax.numpy as jnp
from jax import lax
import numpy as np

D_MODEL = 4096
BATCH = 2
SEQ = 8192
DEPTH = 2

N_EVEN = (DEPTH + 1) // 2
N_ODD = DEPTH // 2

POOL_WINDOWS = (2, 4, 8, 16)
POOL_GROUP = D_MODEL // 8
POOL_WIDTH = POOL_GROUP * len(POOL_WINDOWS)
SWA_HEAD_DIM = 64
SWA_Q_HEADS = (D_MODEL // 2) // SWA_HEAD_DIM
SWA_KV_HEADS = SWA_Q_HEADS // 8
SWA_GQ = SWA_Q_HEADS // SWA_KV_HEADS
SWA_WINDOW = 128
SWA_BLOCK = 128
REL_BUCKETS = 32
REL_MAX_DIST = 128
HG_DIM = 128
HG_WIDTH = 3 * D_MODEL // 4
HG_HEADS = HG_WIDTH // HG_DIM
HG_CHUNK = 64
RMS_EPS = 1e-6
S5_WIDTH = D_MODEL // 4
S5_GROUP = 16
S5_GROUPS = S5_WIDTH // S5_GROUP
S5_STATE = 64
SWA_QW = SWA_Q_HEADS * SWA_HEAD_DIM
SWA_KVW = SWA_KV_HEADS * SWA_HEAD_DIM
EVEN_IN = POOL_WIDTH + SWA_QW + 2 * SWA_KVW
ODD_IN = 4 * HG_WIDTH + S5_WIDTH
MEM_LEN = 256
XA_HEADS = 4
XA_HEAD_DIM = 128
XA_WIDTH = XA_HEADS * XA_HEAD_DIM
N_EXPERTS = 64
TOP_K = 8
N_EXPERT_GROUPS = 8
TOPK_GROUPS = 4
EXPERT_HIDDEN = D_MODEL // 16
SHARED_HIDDEN = D_MODEL // 4
ROUTED_SCALE = 2.5
DN_ALPHA = (2 * DEPTH) ** 0.25
DN_BETA = (8 * DEPTH) ** -0.25
LN_EPS = 1e-5

kernel_name = 'hybrid_pool_swa_hgrn2_s5_moe_trunk'

F32 = jnp.float32


def layer_norm(x, g, b):
    xf = x.astype(F32)
    mu = jnp.mean(xf, -1, keepdims=True)
    var = jnp.mean(jnp.square(xf - mu), -1, keepdims=True)
    return ((xf - mu) * lax.rsqrt(var + LN_EPS) * g + b).astype(x.dtype)


def causal_window_mean(u, w):
    s = u.shape[1]
    cs = jnp.cumsum(u, axis=1)
    lagged = jnp.pad(cs, ((0, 0), (w, 0), (0, 0)))[:, :s]
    count = jnp.minimum(jnp.arange(1, s + 1), w).astype(u.dtype)
    return (cs - lagged) / count[None, :, None]


def pool_mixer(u, pool_w, pool_scale):
    bsz, s, _ = u.shape
    uf = u.astype(F32)
    groups = jnp.split(uf, len(POOL_WINDOWS), axis=-1)
    pooled = jnp.stack([causal_window_mean(g, w) - g for g, w in zip(groups, POOL_WINDOWS)], axis=2)
    y = jnp.einsum('bsgc,gcd->bsgd', pooled.astype(u.dtype), pool_w)
    return y.reshape(bsz, s, POOL_WIDTH) * pool_scale


def t5_causal_bucket(dist):
    exact = REL_BUCKETS // 2
    far = exact + (jnp.log(jnp.maximum(dist, exact).astype(F32) / exact)
                   / math.log(REL_MAX_DIST / exact) * (REL_BUCKETS - exact)).astype(jnp.int32)
    return jnp.where(dist < exact, dist, jnp.minimum(far, REL_BUCKETS - 1))


def swa_sink_attention(q, k, v, sinks, rel_bias):
    bsz, s, _ = q.shape
    nb = s // SWA_BLOCK
    qb = q.reshape(bsz, nb, SWA_BLOCK, SWA_KV_HEADS, SWA_GQ, SWA_HEAD_DIM)
    kb = k.reshape(bsz, nb, SWA_BLOCK, SWA_KV_HEADS, SWA_HEAD_DIM)
    vb = v.reshape(bsz, nb, SWA_BLOCK, SWA_KV_HEADS, SWA_HEAD_DIM)

    def with_prev(t):
        prev = jnp.pad(t, ((0, 0), (1, 0), (0, 0), (0, 0), (0, 0)))[:, :-1]
        return jnp.concatenate([prev, t], axis=2)

    kk, vv = with_prev(kb), with_prev(vb)
    logits = jnp.einsum('bnqhgd,bnkhd->bnhgqk', qb, kk).astype(F32) * (SWA_HEAD_DIM ** -0.5)
    qi = jnp.arange(SWA_BLOCK)[:, None]
    kj = jnp.arange(2 * SWA_BLOCK)[None, :]
    dist = qi + SWA_BLOCK - kj
    band = (dist >= 0) & (dist < SWA_WINDOW)
    key_pos = jnp.arange(nb)[:, None, None] * SWA_BLOCK + kj[None] - SWA_BLOCK
    valid = band[None] & (key_pos >= 0)
    bias = rel_bias.astype(F32)[t5_causal_bucket(jnp.maximum(dist, 0))]
    bias = bias.transpose(2, 0, 1).reshape(SWA_KV_HEADS, SWA_GQ, SWA_BLOCK, 2 * SWA_BLOCK)
    logits = jnp.where(valid[None, :, None, None], logits + bias, -jnp.inf)
    sink = sinks.astype(F32).reshape(SWA_KV_HEADS, SWA_GQ)[:, :, None, None]
    m = jnp.maximum(jnp.max(logits, -1, keepdims=True), sink)
    p = jnp.exp(logits - m)
    p = p / (jnp.sum(p, -1, keepdims=True) + jnp.exp(sink - m))
    out = jnp.einsum('bnhgqk,bnkhd->bnqhgd', p.astype(v.dtype), vv)
    return out.reshape(bsz, s, SWA_QW)


def even_mixer(h, w_in, pool_w, pool_scale, sinks, rel_bias, w_out):
    z = h @ w_in
    u, q, k, v = jnp.split(z, [POOL_WIDTH, POOL_WIDTH + SWA_QW, POOL_WIDTH + SWA_QW + SWA_KVW], axis=-1)
    y_a = pool_mixer(u, pool_w, pool_scale)
    y_b = swa_sink_attention(q, k, v, sinks, rel_bias)
    return jnp.concatenate([y_a.astype(h.dtype), y_b.astype(h.dtype)], axis=-1) @ w_out


def hgrn2_mixer(q, f, i, g, lb, norm_w):
    bsz, s, _ = q.shape
    nc = s // HG_CHUNK
    hd = (bsz, s, HG_HEADS, HG_DIM)
    lbr = lb.astype(F32).reshape(HG_HEADS, HG_DIM)
    qf = jax.nn.silu(q.astype(F32)).reshape(hd)
    forget = lbr + (1.0 - lbr) * jax.nn.sigmoid(f.astype(F32).reshape(hd))
    log_f = jnp.log(forget)
    k_in = 1.0 - forget
    iv = i.astype(F32).reshape(hd)
    chunk = lambda t: t.reshape(bsz, nc, HG_CHUNK, HG_HEADS, HG_DIM)
    qc, kc, vc, lc = chunk(qf), chunk(k_in), chunk(iv), chunk(log_f)
    cum = jnp.cumsum(lc, axis=2)
    ref = cum[:, :, HG_CHUNK // 2:HG_CHUNK // 2 + 1]
    total = cum[:, :, -1]
    scores = jnp.einsum('bclhk,bcmhk->bchlm', qc * jnp.exp(cum - ref), kc * jnp.exp(ref - cum))
    causal = jnp.tril(jnp.ones((HG_CHUNK, HG_CHUNK), bool))
    scores = jnp.where(causal, scores, 0.0)
    o_intra = jnp.einsum('bchlm,bcmhv->bclhv', scores, vc)
    chunk_upd = jnp.einsum('bclhk,bclhv->bchkv', kc * jnp.exp(total[:, :, None] - cum), vc)

    def step(state, inp):
        decay, upd = inp
        return decay[..., None] * state + upd, state

    init = jnp.zeros((bsz, HG_HEADS, HG_DIM, HG_DIM), F32)
    _, s_in = lax.scan(step, init, (jnp.exp(total).swapaxes(0, 1), chunk_upd.swapaxes(0, 1)))
    o_inter = jnp.einsum('bclhk,cbhkv->bclhv', qc * jnp.exp(cum), s_in)
    o = (o_intra + o_inter).reshape(hd)
    o = o * lax.rsqrt(jnp.mean(jnp.square(o), -1, keepdims=True) + RMS_EPS) * norm_w
    o = o * jax.nn.silu(g.astype(F32).reshape(hd))
    return o.reshape(bsz, s, HG_WIDTH)


def _cmul(ar, ai, br, bi):
    return ar * br - ai * bi, ar * bi + ai * br


def _ssm_combine(e1, e2):
    a1r, a1i, b1r, b1i = e1
    a2r, a2i, b2r, b2i = e2
    ar, ai = _cmul(a2r, a2i, a1r, a1i)
    br, bi = _cmul(a2r, a2i, b1r, b1i)
    return ar, ai, br + b2r, bi + b2i


def s5_mixer(u, a_re, a_im, log_step, b_re, b_im, c_re, c_im, d_skip, w_glu, b_glu):
    bsz, s, _ = u.shape
    uf = u.astype(F32).reshape(bsz, s, S5_GROUPS, S5_GROUP)
    a_re, a_im = a_re.astype(F32), a_im.astype(F32)
    dt = jnp.exp(log_step.astype(F32))[:, None]
    mag = jnp.exp(dt * a_re)
    ang = dt * a_im
    ab_re, ab_im = mag * jnp.cos(ang), mag * jnp.sin(ang)
    den = a_re * a_re + a_im * a_im
    co_re = ((ab_re - 1.0) * a_re + ab_im * a_im) / den
    co_im = (ab_im * a_re - (ab_re - 1.0) * a_im) / den
    b_re, b_im = b_re.astype(F32), b_im.astype(F32)
    bb_re = co_re[..., None] * b_re - co_im[..., None] * b_im
    bb_im = co_re[..., None] * b_im + co_im[..., None] * b_re
    bu_re = jnp.einsum('gnc,bsgc->bsgn', bb_re, uf)
    bu_im = jnp.einsum('gnc,bsgc->bsgn', bb_im, uf)
    a_seq_re = jnp.broadcast_to(ab_re, bu_re.shape)
    a_seq_im = jnp.broadcast_to(ab_im, bu_re.shape)
    _, _, x_re, x_im = lax.associative_scan(_ssm_combine, (a_seq_re, a_seq_im, bu_re, bu_im), axis=1)
    y = (jnp.einsum('gcn,bsgn->bsgc', c_re.astype(F32), x_re)
         - jnp.einsum('gcn,bsgn->bsgc', c_im.astype(F32), x_im))
    y = y.reshape(bsz, s, S5_WIDTH) + d_skip.astype(F32) * u.astype(F32)
    z = jax.nn.gelu(y)
    return z * jax.nn.sigmoid(z @ w_glu.astype(F32) + b_glu.astype(F32))


def odd_mixer(h, w_in, lb, hg_norm, a_re, a_im, log_step, b_re, b_im, c_re, c_im, d_skip, w_glu, b_glu, w_out):
    z = h @ w_in
    q, f, i, g, u = jnp.split(z, [HG_WIDTH, 2 * HG_WIDTH, 3 * HG_WIDTH, 4 * HG_WIDTH], axis=-1)
    y_c = hgrn2_mixer(q, f, i, g, lb, hg_norm)
    y_d = s5_mixer(u, a_re, a_im, log_step, b_re, b_im, c_re, c_im, d_skip, w_glu, b_glu)
    return jnp.concatenate([y_c.astype(h.dtype), y_d.astype(h.dtype)], axis=-1) @ w_out


def cross_attention(h, mem, wq, wk, wv, wo):
    bsz, s, _ = h.shape
    m = mem.shape[1]
    q = (h @ wq).reshape(bsz, s, XA_HEADS, XA_HEAD_DIM)
    k = (mem @ wk).reshape(bsz, m, XA_HEADS, XA_HEAD_DIM)
    v = (mem @ wv).reshape(bsz, m, XA_HEADS, XA_HEAD_DIM)
    logits = jnp.einsum('bshd,bmhd->bhsm', q, k).astype(F32) * (XA_HEAD_DIM ** -0.5)
    p = jax.nn.softmax(logits, axis=-1)
    o = jnp.einsum('bhsm,bmhd->bshd', p.astype(v.dtype), v).reshape(bsz, s, XA_WIDTH)
    return o @ wo


def moe_ffn(h, router, router_bias, w_gate, w_up, w_down, s_gate, s_up, s_down):
    bsz, s, d = h.shape
    xt = h.reshape(-1, d)
    n = xt.shape[0]
    scores = jax.nn.sigmoid(jnp.einsum('nd,ed->ne', xt, router).astype(F32))
    biased = scores + router_bias.astype(F32)
    grp = biased.reshape(n, N_EXPERT_GROUPS, N_EXPERTS // N_EXPERT_GROUPS)
    grp_score = jnp.sum(lax.top_k(grp, 2)[0], axis=-1)
    _, top_grp = lax.top_k(grp_score, TOPK_GROUPS)
    grp_mask = jnp.sum(jax.nn.one_hot(top_grp, N_EXPERT_GROUPS, dtype=F32), axis=1) > 0
    expert_mask = jnp.repeat(grp_mask, N_EXPERTS // N_EXPERT_GROUPS, axis=1)
    _, top_idx = lax.top_k(jnp.where(expert_mask, biased, -jnp.inf), TOP_K)
    sel = jnp.take_along_axis(scores, top_idx, axis=1)
    gates = sel / jnp.sum(sel, -1, keepdims=True) * ROUTED_SCALE
    combine = jnp.zeros((n, N_EXPERTS), F32).at[jnp.arange(n)[:, None], top_idx].set(gates)
    hid = jax.nn.silu(jnp.einsum('nd,edh->neh', xt, w_gate)) * jnp.einsum('nd,edh->neh', xt, w_up)
    routed = jnp.einsum('neh,ehd->nd', hid * combine[..., None].astype(hid.dtype), w_down)
    shared = (jax.nn.silu(xt @ s_gate) * (xt @ s_up)) @ s_down
    return (routed + shared).reshape(bsz, s, d)


def setup_inputs(seed: int = 0) -> dict:
    key = jax.random.key(seed)
    ks = jax.random.split(key, 40)
    nrm = lambda k, shape, scale: jax.random.normal(k, shape, F32) * scale
    d = D_MODEL
    log_lo, log_hi = math.log(1e-3), math.log(1e-1)
    return {
        'x': nrm(ks[0], (BATCH, SEQ, d), 1.0),
        'mem': nrm(ks[1], (BATCH, MEM_LEN, d), 1.0),
        'ev_w_in': nrm(ks[2], (N_EVEN, d, EVEN_IN), d ** -0.5),
        'ev_pool_w': nrm(ks[3], (N_EVEN, len(POOL_WINDOWS), POOL_GROUP, POOL_GROUP), POOL_GROUP ** -0.5),
        'ev_pool_scale': 1.0 + nrm(ks[4], (N_EVEN, POOL_WIDTH), 0.02),
        'ev_sinks': nrm(ks[5], (N_EVEN, SWA_Q_HEADS), 0.5),
        'ev_w_out': nrm(ks[6], (N_EVEN, d, d), d ** -0.5 * DN_BETA),
        'rel_bias': nrm(ks[7], (REL_BUCKETS, SWA_Q_HEADS), 0.2),
        'od_w_in': nrm(ks[8], (N_ODD, d, ODD_IN), d ** -0.5),
        'hg_lb_logits': nrm(ks[9], (DEPTH, HG_WIDTH), 0.1),
        'od_hg_norm': 1.0 + nrm(ks[10], (N_ODD, HG_DIM), 0.02),
        'od_a_re': -0.5 + nrm(ks[11], (N_ODD, S5_GROUPS, S5_STATE), 0.01),
        'od_a_im': math.pi * jnp.arange(S5_STATE, dtype=F32) + nrm(ks[12], (N_ODD, S5_GROUPS, S5_STATE), 0.01),
        'od_log_step': jax.random.uniform(ks[13], (N_ODD, S5_GROUPS), F32, log_lo, log_hi),
        'od_b_re': nrm(ks[14], (N_ODD, S5_GROUPS, S5_STATE, S5_GROUP), (2 * S5_GROUP) ** -0.5),
        'od_b_im': nrm(ks[15], (N_ODD, S5_GROUPS, S5_STATE, S5_GROUP), (2 * S5_GROUP) ** -0.5),
        'od_c_re': nrm(ks[16], (N_ODD, S5_GROUPS, S5_GROUP, S5_STATE), S5_STATE ** -0.5),
        'od_c_im': nrm(ks[17], (N_ODD, S5_GROUPS, S5_GROUP, S5_STATE), S5_STATE ** -0.5),
        'od_d_skip': nrm(ks[18], (N_ODD, S5_WIDTH), 1.0),
        'od_w_glu': nrm(ks[19], (N_ODD, S5_WIDTH, S5_WIDTH), S5_WIDTH ** -0.5),
        'od_b_glu': nrm(ks[20], (N_ODD, S5_WIDTH), 0.01),
        'od_w_out': nrm(ks[21], (N_ODD, d, d), d ** -0.5 * DN_BETA),
        'xa_wq': nrm(ks[22], (DEPTH, d, XA_WIDTH), d ** -0.5),
        'xa_wk': nrm(ks[23], (DEPTH, d, XA_WIDTH), d ** -0.5),
        'xa_wv': nrm(ks[24], (DEPTH, d, XA_WIDTH), d ** -0.5),
        'xa_wo': nrm(ks[25], (DEPTH, XA_WIDTH, d), XA_WIDTH ** -0.5 * DN_BETA),
        'moe_router': nrm(ks[26], (DEPTH, N_EXPERTS, d), d ** -0.5),
        'moe_bias': nrm(ks[27], (DEPTH, N_EXPERTS), 0.01),
        'moe_w_gate': nrm(ks[28], (DEPTH, N_EXPERTS, d, EXPERT_HIDDEN), d ** -0.5),
        'moe_w_up': nrm(ks[29], (DEPTH, N_EXPERTS, d, EXPERT_HIDDEN), d ** -0.5),
        'moe_w_down': nrm(ks[30], (DEPTH, N_EXPERTS, EXPERT_HIDDEN, d), EXPERT_HIDDEN ** -0.5 * DN_BETA),
        'sh_w_gate': nrm(ks[31], (DEPTH, d, SHARED_HIDDEN), d ** -0.5),
        'sh_w_up': nrm(ks[32], (DEPTH, d, SHARED_HIDDEN), d ** -0.5),
        'sh_w_down': nrm(ks[33], (DEPTH, SHARED_HIDDEN, d), SHARED_HIDDEN ** -0.5 * DN_BETA),
        'ln_g': 1.0 + nrm(ks[34], (DEPTH, 3, d), 0.02),
        'ln_b': nrm(ks[35], (DEPTH, 3, d), 0.02),
    }


def reference(x, mem, ev_w_in, ev_pool_w, ev_pool_scale, ev_sinks, ev_w_out, rel_bias,
              od_w_in, hg_lb_logits, od_hg_norm, od_a_re, od_a_im, od_log_step, od_b_re, od_b_im,
              od_c_re, od_c_im, od_d_skip, od_w_glu, od_b_glu, od_w_out,
              xa_wq, xa_wk, xa_wv, xa_wo,
              moe_router, moe_bias, moe_w_gate, moe_w_up, moe_w_down, sh_w_gate, sh_w_up, sh_w_down,
              ln_g, ln_b):
    p = jax.nn.softmax(hg_lb_logits.astype(F32), axis=0)
    lower_bounds = jnp.cumsum(p, axis=0) - p[0]
    h = x
    for l in range(DEPTH):
        j = l // 2
        if l % 2 == 0:
            mix = even_mixer(h, ev_w_in[j], ev_pool_w[j], ev_pool_scale[j], ev_sinks[j], rel_bias, ev_w_out[j])
        else:
            mix = odd_mixer(h, od_w_in[j], lower_bounds[l], od_hg_norm[j], od_a_re[j], od_a_im[j],
                            od_log_step[j], od_b_re[j], od_b_im[j], od_c_re[j], od_c_im[j],
                            od_d_skip[j], od_w_glu[j], od_b_glu[j], od_w_out[j])
        h = layer_norm(DN_ALPHA * h + mix, ln_g[l, 0], ln_b[l, 0])
        h = layer_norm(DN_ALPHA * h + cross_attention(h, mem, xa_wq[l], xa_wk[l], xa_wv[l], xa_wo[l]),
                       ln_g[l, 1], ln_b[l, 1])
        h = layer_norm(DN_ALPHA * h + moe_ffn(h, moe_router[l], moe_bias[l], moe_w_gate[l], moe_w_up[l],
                                              moe_w_down[l], sh_w_gate[l], sh_w_up[l], sh_w_down[l]),
                       ln_g[l, 2], ln_b[l, 2])
    return h
```

```python
import functools
import math

import jax
import jax.numpy as jnp
from jax import lax
from jax.experimental import pallas as pl
from jax.experimental.pallas import tpu as pltpu

F32 = jnp.float32
BF16 = jnp.bfloat16

V7X_VMEM_BYTES = 64 * 1024 * 1024
LANES = 128
SUBLANES = 8

POOL_WINDOWS = (2, 4, 8, 16)
SWA_HEAD_DIM = 64
SWA_GQ = 8
SWA_BLOCK = 128
REL_BUCKETS = 32
REL_MAX_DIST = 128
HG_DIM = 128
HG_CHUNK = 64
RMS_EPS = 1e-6
S5_GROUP = 16
S5_STATE = 64
S5_CHUNK = 16
XA_HEADS = 4
XA_HEAD_DIM = 128
N_EXPERTS = 64
TOP_K = 8
N_EXPERT_GROUPS = 8
TOPK_GROUPS = 4
EXPERT_HIDDEN = 256
ROUTED_SCALE = 2.5
LN_EPS = 1e-5


def _params(sem, vmem_mb):
    return pltpu.CompilerParams(dimension_semantics=sem, vmem_limit_bytes=vmem_mb * 1024 * 1024)


def _nt(a, b):
    return lax.dot_general(a, b, (((1,), (1,)), ((), ())), preferred_element_type=F32)


def _tn(a, b):
    return lax.dot_general(a, b, (((0,), (0,)), ((), ())), preferred_element_type=F32)


def _dot(a, b):
    return jnp.dot(a, b, preferred_element_type=F32)


def _sigmoid(x):
    return 1.0 / (1.0 + jnp.exp(-x))


def _mm_body(x_ref, w_ref, o_ref):
    o_ref[...] = _dot(x_ref[...], w_ref[...]).astype(o_ref.dtype)


def _mm(x, w, out_dtype, tm=1024, tn=512):
    m, k = x.shape
    n = w.shape[1]
    tm = min(tm, m)
    tn = min(tn, n)
    return pl.pallas_call(
        _mm_body,
        out_shape=jax.ShapeDtypeStruct((m, n), out_dtype),
        grid=(m // tm, n // tn),
        in_specs=[pl.BlockSpec((tm, k), lambda i, j: (i, 0)),
                  pl.BlockSpec((k, tn), lambda i, j: (0, j))],
        out_specs=pl.BlockSpec((tm, tn), lambda i, j: (i, j)),
        compiler_params=_params(("parallel", "parallel"), 48),
        name="mm",
    )(x, w)


def _add_ln_body(h_ref, y_ref, g_ref, b_ref, o_ref, ob_ref, *, alpha):
    v = alpha * h_ref[...] + y_ref[...].astype(F32)
    mu = jnp.mean(v, axis=-1, keepdims=True)
    c = v - mu
    var = jnp.mean(c * c, axis=-1, keepdims=True)
    out = c * lax.rsqrt(var + LN_EPS) * g_ref[...] + b_ref[...]
    o_ref[...] = out
    ob_ref[...] = out.astype(BF16)


def _add_ln(h, y, g, b, alpha, tm=256):
    n, d = h.shape
    return pl.pallas_call(
        functools.partial(_add_ln_body, alpha=alpha),
        out_shape=(jax.ShapeDtypeStruct((n, d), F32), jax.ShapeDtypeStruct((n, d), BF16)),
        grid=(n // tm,),
        in_specs=[pl.BlockSpec((tm, d), lambda i: (i, 0)),
                  pl.BlockSpec((tm, d), lambda i: (i, 0)),
                  pl.BlockSpec((1, d), lambda i: (0, 0)),
                  pl.BlockSpec((1, d), lambda i: (0, 0))],
        out_specs=(pl.BlockSpec((tm, d), lambda i: (i, 0)),
                   pl.BlockSpec((tm, d), lambda i: (i, 0))),
        compiler_params=_params(("parallel",), 48),
        name="add_ln",
    )(h, y, g.reshape(1, d), b.reshape(1, d))


def _pool_body(u_ref, prev_ref, w_ref, sc_ref, o_ref, *, ts, gw, halo):
    i = pl.program_id(1)
    row = lax.broadcasted_iota(jnp.int32, (ts, gw), 0)
    pos1 = (row + i * ts + 1).astype(F32)
    for g, win in enumerate(POOL_WINDOWS):
        u = u_ref[0, :, g * gw:(g + 1) * gw]
        prev = prev_ref[0, :, g * gw:(g + 1) * gw]
        prev = jnp.where(i == 0, 0.0, prev)
        s = jnp.concatenate([prev, u], axis=0)
        step = 1
        while step < win:
            s = s + pltpu.roll(s, step, axis=0)
            step *= 2
        wsum = s[halo:, :]
        pooled = wsum / jnp.minimum(pos1, float(win)) - u
        y = _dot(pooled.astype(BF16), w_ref[g])
        o_ref[0, :, g * gw:(g + 1) * gw] = (y * sc_ref[:, g * gw:(g + 1) * gw]).astype(o_ref.dtype)


def _pool(z3, pool_w, pool_scale, ts=512):
    bsz, s, _ = z3.shape
    ng, gw, _ = pool_w.shape
    width = ng * gw
    halo = max(POOL_WINDOWS)
    return pl.pallas_call(
        functools.partial(_pool_body, ts=ts, gw=gw, halo=halo),
        out_shape=jax.ShapeDtypeStruct((bsz, s, width), BF16),
        grid=(bsz, s // ts),
        in_specs=[pl.BlockSpec((1, ts, width), lambda b, i: (b, i, 0)),
                  pl.BlockSpec((1, halo, width), lambda b, i: (b, jnp.maximum(i * (ts // halo) - 1, 0), 0)),
                  pl.BlockSpec((ng, gw, gw), lambda b, i: (0, 0, 0)),
                  pl.BlockSpec((1, width), lambda b, i: (0, 0))],
        out_specs=pl.BlockSpec((1, ts, width), lambda b, i: (b, i, 0)),
        compiler_params=_params(("parallel", "parallel"), 48),
        name="pool",
    )(z3, z3, pool_w, pool_scale.reshape(1, width))


def _t5_bucket_table():
    qi = jnp.arange(SWA_BLOCK)[:, None]
    kj = jnp.arange(2 * SWA_BLOCK)[None, :]
    dist = qi + SWA_BLOCK - kj
    band = (dist >= 0) & (dist < SWA_BLOCK)
    d = jnp.maximum(dist, 0)
    exact = REL_BUCKETS // 2
    far = exact + (jnp.log(jnp.maximum(d, exact).astype(F32) / exact)
                   / math.log(REL_MAX_DIST / exact) * (REL_BUCKETS - exact)).astype(jnp.int32)
    bucket = jnp.where(d < exact, d, jnp.minimum(far, REL_BUCKETS - 1))
    return jnp.where(band, bucket, -1).astype(jnp.int32)


def _bias_body(rb_ref, bucket_ref, o_ref):
    h = pl.program_id(0)
    bucket = bucket_ref[...]
    acc = jnp.full(bucket.shape, -jnp.inf, F32)
    for b in range(REL_BUCKETS):
        acc = jnp.where(bucket == b, rb_ref[b, h], acc)
    o_ref[0] = acc


def _bias_table(rel_bias):
    nb, nh = rel_bias.shape
    bucket = _t5_bucket_table()
    return pl.pallas_call(
        _bias_body,
        out_shape=jax.ShapeDtypeStruct((nh, SWA_BLOCK, 2 * SWA_BLOCK), F32),
        grid=(nh,),
        in_specs=[pl.BlockSpec(memory_space=pltpu.SMEM),
                  pl.BlockSpec((SWA_BLOCK, 2 * SWA_BLOCK), lambda h: (0, 0))],
        out_specs=pl.BlockSpec((1, SWA_BLOCK, 2 * SWA_BLOCK), lambda h: (h, 0, 0)),
        compiler_params=_params(("parallel",), 32),
        name="swa_bias",
    )(rel_bias.astype(F32), bucket)


def _swa_body(sink_ref, q_ref, kp_ref, kc_ref, vp_ref, vc_ref, bias_ref, o_ref, *, n_kv):
    n = pl.program_id(1)
    dh = SWA_HEAD_DIM
    col = lax.broadcasted_iota(jnp.int32, (SWA_BLOCK, 2 * SWA_BLOCK), 1)
    no_prev = jnp.logical_and(n == 0, col < SWA_BLOCK)
    scale = dh ** -0.5
    for kv in range(n_kv):
        sl = slice(kv * dh, (kv + 1) * dh)
        k2 = jnp.concatenate([kp_ref[0, :, sl], kc_ref[0, :, sl]], axis=0).astype(BF16)
        v2 = jnp.concatenate([vp_ref[0, :, sl], vc_ref[0, :, sl]], axis=0).astype(BF16)
        for g in range(SWA_GQ):
            h = kv * SWA_GQ + g
            qh = q_ref[0, :, h * dh:(h + 1) * dh].astype(BF16)
            logits = _nt(qh, k2) * scale + bias_ref[h]
            logits = jnp.where(no_prev, -jnp.inf, logits)
            sink = sink_ref[h]
            m = jnp.maximum(jnp.max(logits, axis=-1, keepdims=True), sink)
            p = jnp.exp(logits - m)
            p = p / (jnp.sum(p, axis=-1, keepdims=True) + jnp.exp(sink - m))
            o_ref[0, :, h * dh:(h + 1) * dh] = _dot(p.astype(BF16), v2).astype(o_ref.dtype)


def _swa(z3, sinks, bias_tab, q_off, n_q, n_kv):
    bsz, s, _ = z3.shape
    dh = SWA_HEAD_DIM
    qw, kvw = n_q * dh, n_kv * dh
    qb = q_off // qw
    kb = (q_off + qw) // kvw
    vb = kb + 1
    blk = SWA_BLOCK
    prev = lambda n: jnp.maximum(n - 1, 0)
    return pl.pallas_call(
        functools.partial(_swa_body, n_kv=n_kv),
        out_shape=jax.ShapeDtypeStruct((bsz, s, qw), BF16),
        grid=(bsz, s // blk),
        in_specs=[pl.BlockSpec(memory_space=pltpu.SMEM),
                  pl.BlockSpec((1, blk, qw), lambda b, n: (b, n, qb)),
                  pl.BlockSpec((1, blk, kvw), lambda b, n: (b, prev(n), kb)),
                  pl.BlockSpec((1, blk, kvw), lambda b, n: (b, n, kb)),
                  pl.BlockSpec((1, blk, kvw), lambda b, n: (b, prev(n), vb)),
                  pl.BlockSpec((1, blk, kvw), lambda b, n: (b, n, vb)),
                  pl.BlockSpec((n_q, blk, 2 * blk), lambda b, n: (0, 0, 0))],
        out_specs=pl.BlockSpec((1, blk, qw), lambda b, n: (b, n, 0)),
        compiler_params=_params(("parallel", "parallel"), 48),
        name="swa",
    )(sinks.astype(F32), z3, z3, z3, z3, z3, bias_tab)


def _hgrn_body(q_ref, f_ref, i_ref, g_ref, lbl_ref, nw_ref, o_ref, st_ref, *, n_chunk, layer):
    @pl.when(pl.program_id(2) == 0)
    def _():
        st_ref[...] = jnp.zeros_like(st_ref)

    lg = lbl_ref[:, 0, 0, :]
    e = jnp.exp(lg - jnp.max(lg, axis=0, keepdims=True))
    p = e / jnp.sum(e, axis=0, keepdims=True)
    lb = jnp.sum(p[:layer + 1], axis=0, keepdims=True) - p[0:1]
    nw = nw_ref[...]
    c = HG_CHUNK
    r = lax.broadcasted_iota(jnp.int32, (c, c), 0)
    cc = lax.broadcasted_iota(jnp.int32, (c, c), 1)
    causal = r >= cc
    tri = jnp.where(causal, 1.0, 0.0).astype(BF16)

    def chunk(ci, carry):
        r0 = pl.multiple_of(ci * c, c)
        q = q_ref[0, pl.ds(r0, c), :]
        f = f_ref[0, pl.ds(r0, c), :]
        iv = i_ref[0, pl.ds(r0, c), :].astype(BF16)
        g = g_ref[0, pl.ds(r0, c), :]
        qf = q * _sigmoid(q)
        forget = lb + (1.0 - lb) * _sigmoid(f)
        log_f = jnp.log(forget)
        k_in = 1.0 - forget
        hi = log_f.astype(BF16)
        r1 = log_f - hi.astype(F32)
        mid = r1.astype(BF16)
        lo = (r1 - mid.astype(F32)).astype(BF16)
        cum = _dot(tri, hi) + _dot(tri, mid) + _dot(tri, lo)
        ref = cum[c // 2:c // 2 + 1, :]
        total = cum[c - 1:c, :]
        scores = _nt((qf * jnp.exp(cum - ref)).astype(BF16), (k_in * jnp.exp(ref - cum)).astype(BF16))
        scores = jnp.where(causal, scores, 0.0)
        o_intra = _dot(scores.astype(BF16), iv)
        st = st_ref[...]
        o_inter = _nt((qf * jnp.exp(cum)).astype(BF16), st.astype(BF16))
        upd = _tn(iv, (k_in * jnp.exp(total - cum)).astype(BF16))
        st_ref[...] = st * jnp.exp(total) + upd
        o = o_intra + o_inter
        o = o * lax.rsqrt(jnp.mean(o * o, axis=-1, keepdims=True) + RMS_EPS) * nw
        o = o * (g * _sigmoid(g))
        o_ref[0, pl.ds(r0, c), :] = o.astype(o_ref.dtype)
        return carry

    lax.fori_loop(0, n_chunk, chunk, 0)


def _hgrn(z3, lb_logits, norm_w, layer, n_heads, ts=512):
    bsz, s, _ = z3.shape
    depth = lb_logits.shape[0]
    d = HG_DIM
    spec = lambda off: pl.BlockSpec((1, ts, d), lambda b, h, t: (b, t, off + h))
    return pl.pallas_call(
        functools.partial(_hgrn_body, n_chunk=ts // HG_CHUNK, layer=layer),
        out_shape=jax.ShapeDtypeStruct((bsz, s, n_heads * d), BF16),
        grid=(bsz, n_heads, s // ts),
        in_specs=[spec(0), spec(n_heads), spec(2 * n_heads), spec(3 * n_heads),
                  pl.BlockSpec((depth, 1, 1, d), lambda b, h, t: (0, h, 0, 0)),
                  pl.BlockSpec((1, d), lambda b, h, t: (0, 0))],
        out_specs=pl.BlockSpec((1, ts, d), lambda b, h, t: (b, t, h)),
        scratch_shapes=[pltpu.VMEM((d, d), F32)],
        compiler_params=_params(("parallel", "parallel", "arbitrary"), 32),
        name="hgrn2",
    )(z3, z3, z3, z3, lb_logits.astype(F32).reshape(depth, n_heads, 1, d), norm_w.astype(F32).reshape(1, d))


def _cpow(pw, lam_re, lam_im):
    mag = jnp.exp(pw * lam_re)
    ang = pw * lam_im
    return mag * jnp.cos(ang), mag * jnp.sin(ang)


def _s5_prep_body(ls_ref, ar_ref, ai_ref, arc_ref, aic_ref, br_ref, bi_ref, cr_ref, ci_ref,
                  m_ref, p_ref, q_ref, tr_ref, ti_ref):
    nst = S5_STATE
    lc = S5_CHUNK
    gc = S5_GROUP
    w = lc * gc
    dt = jnp.exp(ls_ref[0])
    a_re, a_im = ar_ref[0], ai_ref[0]
    lam_re, lam_im = dt * a_re, dt * a_im
    ab_re, ab_im = _cpow(1.0, lam_re, lam_im)
    den = a_re * a_re + a_im * a_im
    co_re = ((ab_re - 1.0) * a_re + ab_im * a_im) / den
    co_im = (ab_im * a_re - (ab_re - 1.0) * a_im) / den
    b_re, b_im = br_ref[0], bi_ref[0]
    bb_re = co_re * b_re - co_im * b_im
    bb_im = co_re * b_im + co_im * b_re
    lane = lax.broadcasted_iota(jnp.int32, (w, 2 * nst), 1)
    rowi = lax.broadcasted_iota(jnp.int32, (w, 2 * nst), 0)

    pw = (lc - 1 - rowi // gc).astype(F32)
    ap_re, ap_im = _cpow(pw, lam_re, lam_im)
    tb_re = jnp.concatenate([bb_re] * lc, axis=0)
    tb_im = jnp.concatenate([bb_im] * lc, axis=0)
    p_ref[0] = jnp.where(lane < nst, ap_re * tb_re - ap_im * tb_im,
                         ap_re * tb_im + ap_im * tb_re).astype(p_ref.dtype)

    lam_re_c, lam_im_c = dt * arc_ref[0], dt * aic_ref[0]
    c_re, c_im = cr_ref[0], ci_ref[0]
    lane_q = lax.broadcasted_iota(jnp.int32, (2 * nst, w), 1)
    row_q = lax.broadcasted_iota(jnp.int32, (2 * nst, w), 0)
    tq = (lane_q // gc).astype(F32)

    def c_times_apow(pwq):
        pr, pi = _cpow(pwq, lam_re_c, lam_im_c)
        return jnp.where(row_q < nst, c_re * pr - c_im * pi, -(c_re * pi + c_im * pr))

    q_ref[0] = c_times_apow(tq + 1.0).astype(q_ref.dtype)
    wmat = c_times_apow(tq)

    bbt = jnp.where(lax.broadcasted_iota(jnp.int32, (gc, 2 * nst), 1) < nst, bb_re, bb_im)
    r = jnp.dot(bbt, wmat, preferred_element_type=F32, precision=lax.Precision.HIGHEST)
    lane_r = lax.broadcasted_iota(jnp.int32, (gc, w), 1)
    for t in range(lc):
        blk = r if t == 0 else jnp.where(lane_r >= t * gc, pltpu.roll(r, t * gc, axis=1), 0.0)
        m_ref[0, t * gc:(t + 1) * gc, :] = blk.astype(m_ref.dtype)

    kk = lax.broadcasted_iota(jnp.int32, (16, 2 * nst), 0)
    pws = (lc * jnp.left_shift(1, jnp.minimum(kk, 12))).astype(F32)
    sr, si = _cpow(pws, lam_re, lam_im)
    tr_ref[0] = sr
    ti_ref[0] = jnp.where(lax.broadcasted_iota(jnp.int32, (16, 2 * nst), 1) < nst, -si, si)


def _s5_prep(a_re, a_im, log_step, b_re, b_im, c_re, c_im):
    ng, nst = a_re.shape
    gc, lc = S5_GROUP, S5_CHUNK
    w = lc * gc
    dup = lambda t: jnp.concatenate([t, t], axis=-1)
    ar = dup(a_re.astype(F32))
    ai = dup(a_im.astype(F32))
    brt = dup(jnp.swapaxes(b_re.astype(F32), 1, 2))
    bit = dup(jnp.swapaxes(b_im.astype(F32), 1, 2))
    tile_c = lambda t: jnp.tile(jnp.concatenate([jnp.swapaxes(t.astype(F32), 1, 2)] * 2, axis=1), (1, 1, lc))
    crt, cit = tile_c(c_re), tile_c(c_im)
    g3 = lambda blk: pl.BlockSpec((1,) + blk, lambda g: (g, 0, 0))
    return pl.pallas_call(
        _s5_prep_body,
        out_shape=(jax.ShapeDtypeStruct((ng, w, w), BF16),
                   jax.ShapeDtypeStruct((ng, w, 2 * nst), BF16),
                   jax.ShapeDtypeStruct((ng, 2 * nst, w), BF16),
                   jax.ShapeDtypeStruct((ng, 16, 2 * nst), F32),
                   jax.ShapeDtypeStruct((ng, 16, 2 * nst), F32)),
        grid=(ng,),
        in_specs=[g3((1, 1)), g3((1, 2 * nst)), g3((1, 2 * nst)), g3((2 * nst, 1)), g3((2 * nst, 1)),
                  g3((gc, 2 * nst)), g3((gc, 2 * nst)), g3((2 * nst, w)), g3((2 * nst, w))],
        out_specs=(g3((w, w)), g3((w, 2 * nst)), g3((2 * nst, w)), g3((16, 2 * nst)), g3((16, 2 * nst))),
        compiler_params=_params(("parallel",), 32),
        name="s5_prep",
    )(log_step.astype(F32).reshape(ng, 1, 1), ar.reshape(ng, 1, 2 * nst), ai.reshape(ng, 1, 2 * nst),
      ar.reshape(ng, 2 * nst, 1), ai.reshape(ng, 2 * nst, 1), brt, bit, crt, cit)


def _s5_scan_body(u_ref, m_ref, p_ref, q_ref, tr_ref, ti_ref, y_ref, *, n_steps):
    nst = S5_STATE
    u = u_ref[0, 0]
    x = _dot(u, p_ref[0])
    row = lax.broadcasted_iota(jnp.int32, x.shape, 0)
    tr = tr_ref[0]
    ti = ti_ref[0]
    for k in range(n_steps):
        s = 1 << k
        sh = jnp.where(row >= s, pltpu.roll(x, s, axis=0), 0.0)
        x = x + tr[k:k + 1, :] * sh + ti[k:k + 1, :] * pltpu.roll(sh, nst, axis=1)
    x_in = jnp.where(row >= 1, pltpu.roll(x, 1, axis=0), 0.0)
    y_ref[0, 0] = _dot(u, m_ref[0]) + _dot(x_in.astype(BF16), q_ref[0])


def _s5_scan(ug, m, p, q, tr, ti):
    ng, bsz, nck, w = ug.shape
    nst2 = p.shape[-1]
    n_steps = int(math.log2(nck))
    assert (1 << n_steps) == nck
    g3 = lambda blk: pl.BlockSpec((1,) + blk, lambda g, b: (g, 0, 0))
    return pl.pallas_call(
        functools.partial(_s5_scan_body, n_steps=n_steps),
        out_shape=jax.ShapeDtypeStruct((ng, bsz, nck, w), F32),
        grid=(ng, bsz),
        in_specs=[pl.BlockSpec((1, 1, nck, w), lambda g, b: (g, b, 0, 0)),
                  g3((w, w)), g3((w, nst2)), g3((nst2, w)), g3((16, nst2)), g3((16, nst2))],
        out_specs=pl.BlockSpec((1, 1, nck, w), lambda g, b: (g, b, 0, 0)),
        compiler_params=_params(("parallel", "parallel"), 32),
        name="s5_scan",
    )(ug, m, p, q, tr, ti)


def _glu_body(y_ref, u_ref, d_ref, w_ref, b_ref, o_ref):
    yy = y_ref[...] + d_ref[...] * u_ref[...]
    z = 0.5 * yy * (1.0 + jnp.tanh(math.sqrt(2.0 / math.pi) * (yy + 0.044715 * (yy * yy * yy))))
    gate = _sigmoid(_dot(z.astype(BF16), w_ref[...]) + b_ref[...])
    o_ref[...] = (z * gate).astype(o_ref.dtype)


def _glu(y, z2, u_blk, d_skip, w_glu, b_glu, tm=512):
    n, wd = y.shape
    return pl.pallas_call(
        _glu_body,
        out_shape=jax.ShapeDtypeStruct((n, wd), BF16),
        grid=(n // tm,),
        in_specs=[pl.BlockSpec((tm, wd), lambda i: (i, 0)),
                  pl.BlockSpec((tm, wd), lambda i: (i, u_blk)),
                  pl.BlockSpec((1, wd), lambda i: (0, 0)),
                  pl.BlockSpec((wd, wd), lambda i: (0, 0)),
                  pl.BlockSpec((1, wd), lambda i: (0, 0))],
        out_specs=pl.BlockSpec((tm, wd), lambda i: (i, 0)),
        compiler_params=_params(("parallel",), 32),
        name="s5_glu",
    )(y, z2, d_skip.astype(F32).reshape(1, wd), w_glu.astype(BF16), b_glu.astype(F32).reshape(1, wd))


def _xattn_body(hb_ref, h_ref, wq_ref, k_ref, v_ref, wo_ref, g_ref, b_ref, o_ref, ob_ref, *, alpha):
    d = XA_HEAD_DIM
    q = _dot(hb_ref[...], wq_ref[...]).astype(BF16)
    outs = []
    for hd in range(XA_HEADS):
        sl = slice(hd * d, (hd + 1) * d)
        logits = _nt(q[:, sl], k_ref[0, :, sl]) * (d ** -0.5)
        m = jnp.max(logits, axis=-1, keepdims=True)
        p = jnp.exp(logits - m)
        p = p / jnp.sum(p, axis=-1, keepdims=True)
        outs.append(_dot(p.astype(BF16), v_ref[0, :, sl]))
    o = jnp.concatenate(outs, axis=-1).astype(BF16)
    v = alpha * h_ref[...] + _dot(o, wo_ref[...])
    mu = jnp.mean(v, axis=-1, keepdims=True)
    c = v - mu
    var = jnp.mean(c * c, axis=-1, keepdims=True)
    out = c * lax.rsqrt(var + LN_EPS) * g_ref[...] + b_ref[...]
    o_ref[...] = out
    ob_ref[...] = out.astype(BF16)


def _xattn(h, hb, kmem, vmem, wq, wo, g, b, alpha, seq, tm=256):
    n, d = h.shape
    xw = wq.shape[1]
    ml = kmem.shape[1]
    per_b = seq // tm
    return pl.pallas_call(
        functools.partial(_xattn_body, alpha=alpha),
        out_shape=(jax.ShapeDtypeStruct((n, d), F32), jax.ShapeDtypeStruct((n, d), BF16)),
        grid=(n // tm,),
        in_specs=[pl.BlockSpec((tm, d), lambda i: (i, 0)),
                  pl.BlockSpec((tm, d), lambda i: (i, 0)),
                  pl.BlockSpec((d, xw), lambda i: (0, 0)),
                  pl.BlockSpec((1, ml, xw), lambda i: (i // per_b, 0, 0)),
                  pl.BlockSpec((1, ml, xw), lambda i: (i // per_b, 0, 0)),
                  pl.BlockSpec((xw, d), lambda i: (0, 0)),
                  pl.BlockSpec((1, d), lambda i: (0, 0)),
                  pl.BlockSpec((1, d), lambda i: (0, 0))],
        out_specs=(pl.BlockSpec((tm, d), lambda i: (i, 0)),
                   pl.BlockSpec((tm, d), lambda i: (i, 0))),
        compiler_params=_params(("parallel",), 56),
        name="xattn",
    )(hb, h, wq, kmem, vmem, wo, g.reshape(1, d), b.reshape(1, d))


def _first_max(v, idx, n):
    m = jnp.max(v, axis=0, keepdims=True)
    first = jnp.min(jnp.where(v == m, idx, n), axis=0, keepdims=True)
    return m, first


def _router_body(x_ref, r_ref, rb_ref, o_ref, *, tm, n_shared):
    ne, ng = N_EXPERTS, N_EXPERT_GROUPS
    per = ne // ng
    logits = lax.dot_general(r_ref[...], x_ref[...], (((1,), (1,)), ((), ())),
                             preferred_element_type=F32, precision=lax.Precision.HIGHEST)
    scores = _sigmoid(logits)
    biased = scores + rb_ref[...]
    gsc = []
    eidx = lax.broadcasted_iota(jnp.int32, (per, tm), 0)
    for g in range(ng):
        v = biased[g * per:(g + 1) * per, :]
        m1, i1 = _first_max(v, eidx, per)
        m2 = jnp.max(jnp.where(eidx == i1, -jnp.inf, v), axis=0, keepdims=True)
        gsc.append(m1 + m2)
    gs = jnp.concatenate(gsc, axis=0)
    gidx = lax.broadcasted_iota(jnp.int32, (ng, tm), 0)
    gsel = jnp.zeros((ng, tm), jnp.bool_)
    for _ in range(TOPK_GROUPS):
        _, first = _first_max(gs, gidx, ng)
        hit = gidx == first
        gsel = jnp.logical_or(gsel, hit)
        gs = jnp.where(hit, -jnp.inf, gs)
    emask = jnp.concatenate([jnp.broadcast_to(gsel[g:g + 1, :], (per, tm)) for g in range(ng)], axis=0)
    cand = jnp.where(emask, biased, -jnp.inf)
    aidx = lax.broadcasted_iota(jnp.int32, (ne, tm), 0)
    sel = jnp.zeros((ne, tm), jnp.bool_)
    for _ in range(TOP_K):
        _, first = _first_max(cand, aidx, ne)
        hit = aidx == first
        sel = jnp.logical_or(sel, hit)
        cand = jnp.where(hit, -jnp.inf, cand)
    picked = jnp.where(sel, scores, 0.0)
    gates = picked / jnp.sum(picked, axis=0, keepdims=True) * ROUTED_SCALE
    rows = lax.broadcasted_iota(jnp.int32, (LANES - ne, tm), 0)
    pad = jnp.where(rows < n_shared, 1.0, 0.0).astype(F32)
    o_ref[...] = jnp.concatenate([gates, pad], axis=0).T


def _router(h, router, router_bias, n_shared, tm=512):
    n, d = h.shape
    ne = router.shape[0]
    return pl.pallas_call(
        functools.partial(_router_body, tm=tm, n_shared=n_shared),
        out_shape=jax.ShapeDtypeStruct((n, LANES), F32),
        grid=(n // tm,),
        in_specs=[pl.BlockSpec((tm, d), lambda i: (i, 0)),
                  pl.BlockSpec((ne, d), lambda i: (0, 0)),
                  pl.BlockSpec((ne, 1), lambda i: (0, 0))],
        out_specs=pl.BlockSpec((tm, LANES), lambda i: (i, 0)),
        compiler_params=_params(("parallel",), 48),
        name="moe_router",
    )(h, router.astype(F32), router_bias.astype(F32).reshape(ne, 1))


def _moe_body(x_ref, wgu_ref, wd_ref, comb_ref, o_ref, *, tm, n_col):
    e = pl.program_id(1)

    @pl.when(e == 0)
    def _():
        o_ref[...] = jnp.zeros_like(o_ref)

    hdim = EXPERT_HIDDEN
    gu = _dot(x_ref[...], wgu_ref[0])
    gte = gu[:, :hdim]
    up = gu[:, hdim:]
    lane = lax.broadcasted_iota(jnp.int32, (tm, LANES), 1)
    c = jnp.sum(jnp.where(lane == e, comb_ref[...], 0.0), axis=1, keepdims=True)
    hid = ((gte * _sigmoid(gte)) * up * c).astype(BF16)
    d = o_ref.shape[1]
    cw = d // n_col
    for j in range(n_col):
        o_ref[:, j * cw:(j + 1) * cw] += _dot(hid, wd_ref[0, :, j * cw:(j + 1) * cw])


def _moe(hb, wgu, wd, comb, tm=512):
    n, d = hb.shape
    nexp = wgu.shape[0]
    return pl.pallas_call(
        functools.partial(_moe_body, tm=tm, n_col=4),
        out_shape=jax.ShapeDtypeStruct((n, d), F32),
        grid=(n // tm, nexp),
        in_specs=[pl.BlockSpec((tm, d), lambda i, e: (i, 0)),
                  pl.BlockSpec((1, d, wgu.shape[2]), lambda i, e: (e, 0, 0)),
                  pl.BlockSpec((1, wd.shape[1], d), lambda i, e: (e, 0, 0)),
                  pl.BlockSpec((tm, LANES), lambda i, e: (i, 0))],
        out_specs=pl.BlockSpec((tm, d), lambda i, e: (i, 0)),
        compiler_params=_params(("parallel", "arbitrary"), 56),
        name="moe_experts",
    )(hb, wgu, wd, comb)


def kernel(x, mem, ev_w_in, ev_pool_w, ev_pool_scale, ev_sinks, ev_w_out, rel_bias, od_w_in, hg_lb_logits, od_hg_norm, od_a_re, od_a_im, od_log_step, od_b_re, od_b_im, od_c_re, od_c_im, od_d_skip, od_w_glu, od_b_glu, od_w_out, xa_wq, xa_wk, xa_wv, xa_wo, moe_router, moe_bias, moe_w_gate, moe_w_up, moe_w_down, sh_w_gate, sh_w_up, sh_w_down, ln_g, ln_b):
    bsz, seq, d = x.shape
    n = bsz * seq
    depth = ln_g.shape[0]
    alpha = (2 * depth) ** 0.25
    memb = mem.astype(BF16).reshape(bsz * mem.shape[1], d)

    h = x.astype(F32).reshape(n, d)
    hb = h.astype(BF16)
    bias_tab = _bias_table(rel_bias)

    for l in range(depth):
        j = l // 2
        if l % 2 == 0:
            pool_w = ev_pool_w[j]
            pool_width = pool_w.shape[0] * pool_w.shape[1]
            n_q = ev_sinks.shape[1]
            n_kv = n_q // SWA_GQ
            z = _mm(hb, ev_w_in[j].astype(BF16), F32)
            z3 = z.reshape(bsz, seq, z.shape[1])
            y_a = _pool(z3, pool_w.astype(BF16), ev_pool_scale[j].astype(F32))
            y_b = _swa(z3, ev_sinks[j], bias_tab, pool_width, n_q, n_kv)
            y = jnp.concatenate([y_a, y_b], axis=-1).reshape(n, d)
            mix = _mm(y, ev_w_out[j].astype(BF16), F32)
        else:
            n_heads = hg_lb_logits.shape[1] // HG_DIM
            hgw = n_heads * HG_DIM
            ng, nst = od_a_re.shape[1], od_a_re.shape[2]
            s5w = ng * S5_GROUP
            z = _mm(hb, od_w_in[j].astype(BF16), F32)
            z3 = z.reshape(bsz, seq, z.shape[1])
            y_c = _hgrn(z3, hg_lb_logits, od_hg_norm[j], l, n_heads)
            m_mat, p_mat, q_mat, t_re, t_im = _s5_prep(od_a_re[j], od_a_im[j], od_log_step[j], od_b_re[j],
                                                        od_b_im[j], od_c_re[j], od_c_im[j])
            nck = seq // S5_CHUNK
            u = z3[:, :, 4 * hgw:].astype(BF16).reshape(bsz, nck, S5_CHUNK, ng, S5_GROUP)
            ug = u.transpose(3, 0, 1, 2, 4).reshape(ng, bsz, nck, S5_CHUNK * S5_GROUP)
            yg = _s5_scan(ug, m_mat, p_mat, q_mat, t_re, t_im)
            y_s = yg.reshape(ng, bsz, nck, S5_CHUNK, S5_GROUP).transpose(1, 2, 3, 0, 4).reshape(n, s5w)
            y_d = _glu(y_s, z, (4 * hgw) // s5w, od_d_skip[j], od_w_glu[j], od_b_glu[j])
            y = jnp.concatenate([y_c.reshape(n, hgw), y_d], axis=-1)
            mix = _mm(y, od_w_out[j].astype(BF16), F32)
        h, hb = _add_ln(h, mix, ln_g[l, 0], ln_b[l, 0], alpha)

        xw = xa_wq.shape[2]
        kmem = _mm(memb, xa_wk[l].astype(BF16), BF16).reshape(bsz, -1, xw)
        vmem = _mm(memb, xa_wv[l].astype(BF16), BF16).reshape(bsz, -1, xw)
        h, hb = _xattn(h, hb, kmem, vmem, xa_wq[l].astype(BF16), xa_wo[l].astype(BF16),
                       ln_g[l, 1], ln_b[l, 1], alpha, seq)

        n_shared = sh_w_gate.shape[2] // EXPERT_HIDDEN
        comb = _router(h, moe_router[l], moe_bias[l], n_shared)
        split = lambda t: t.reshape(d, n_shared, EXPERT_HIDDEN).transpose(1, 0, 2)
        wgu = jnp.concatenate([
            jnp.concatenate([moe_w_gate[l], moe_w_up[l]], axis=-1),
            jnp.concatenate([split(sh_w_gate[l]), split(sh_w_up[l])], axis=-1)], axis=0).astype(BF16)
        wd = jnp.concatenate([moe_w_down[l], sh_w_down[l].reshape(n_shared, EXPERT_HIDDEN, d)], axis=0).astype(BF16)
        ffn = _moe(hb, wgu, wd, comb)
        h, hb = _add_ln(h, ffn, ln_g[l, 2], ln_b[l, 2], alpha)

    return h.reshape(bsz, seq, d)
```

```python
import functools
import math

import jax
import jax.numpy as jnp
from jax import lax
from jax.experimental import pallas as pl
from jax.experimental.pallas import tpu as pltpu

F32 = jnp.float32
BF16 = jnp.bfloat16

V7X_VMEM_BYTES = 64 * 1024 * 1024
LANES = 128
SUBLANES = 8

POOL_WINDOWS = (2, 4, 8, 16)
SWA_HEAD_DIM = 64
SWA_GQ = 8
SWA_BLOCK = 128
REL_BUCKETS = 32
REL_MAX_DIST = 128
HG_DIM = 128
HG_CHUNK = 64
RMS_EPS = 1e-6
S5_GROUP = 16
S5_STATE = 64
S5_CHUNK = 16
XA_HEADS = 4
XA_HEAD_DIM = 128
N_EXPERTS = 64
TOP_K = 8
N_EXPERT_GROUPS = 8
TOPK_GROUPS = 4
EXPERT_HIDDEN = 256
ROUTED_SCALE = 2.5
LN_EPS = 1e-5


def _params(sem, vmem_mb):
    return pltpu.CompilerParams(dimension_semantics=sem, vmem_limit_bytes=vmem_mb * 1024 * 1024)


def _nt(a, b):
    return lax.dot_general(a, b, (((1,), (1,)), ((), ())), preferred_element_type=F32)


def _tn(a, b):
    return lax.dot_general(a, b, (((0,), (0,)), ((), ())), preferred_element_type=F32)


def _dot(a, b):
    return jnp.dot(a, b, preferred_element_type=F32)


def _sigmoid(x):
    return 1.0 / (1.0 + jnp.exp(-x))


def _mm_body(x_ref, w_ref, o_ref):
    o_ref[...] = _dot(x_ref[...], w_ref[...]).astype(o_ref.dtype)


def _mm(x, w, out_dtype, tm=1024, tn=512):
    m, k = x.shape
    n = w.shape[1]
    tm = min(tm, m)
    tn = min(tn, n)
    return pl.pallas_call(
        _mm_body,
        out_shape=jax.ShapeDtypeStruct((m, n), out_dtype),
        grid=(m // tm, n // tn),
        in_specs=[pl.BlockSpec((tm, k), lambda i, j: (i, 0)),
                  pl.BlockSpec((k, tn), lambda i, j: (0, j))],
        out_specs=pl.BlockSpec((tm, tn), lambda i, j: (i, j)),
        compiler_params=_params(("parallel", "parallel"), 48),
        name="mm",
    )(x, w)


def _add_ln_body(h_ref, y_ref, g_ref, b_ref, o_ref, ob_ref, *, alpha):
    v = alpha * h_ref[...] + y_ref[...].astype(F32)
    mu = jnp.mean(v, axis=-1, keepdims=True)
    c = v - mu
    var = jnp.mean(c * c, axis=-1, keepdims=True)
    out = c * lax.rsqrt(var + LN_EPS) * g_ref[...] + b_ref[...]
    o_ref[...] = out
    ob_ref[...] = out.astype(BF16)


def _add_ln(h, y, g, b, alpha, tm=256):
    n, d = h.shape
    return pl.pallas_call(
        functools.partial(_add_ln_body, alpha=alpha),
        out_shape=(jax.ShapeDtypeStruct((n, d), F32), jax.ShapeDtypeStruct((n, d), BF16)),
        grid=(n // tm,),
        in_specs=[pl.BlockSpec((tm, d), lambda i: (i, 0)),
                  pl.BlockSpec((tm, d), lambda i: (i, 0)),
                  pl.BlockSpec((1, d), lambda i: (0, 0)),
                  pl.BlockSpec((1, d), lambda i: (0, 0))],
        out_specs=(pl.BlockSpec((tm, d), lambda i: (i, 0)),
                   pl.BlockSpec((tm, d), lambda i: (i, 0))),
        compiler_params=_params(("parallel",), 48),
        name="add_ln",
    )(h, y, g.reshape(1, d), b.reshape(1, d))


def _pool_body(u_ref, prev_ref, w_ref, sc_ref, o_ref, *, ts, gw, halo):
    i = pl.program_id(1)
    row = lax.broadcasted_iota(jnp.int32, (ts, gw), 0)
    pos1 = (row + i * ts + 1).astype(F32)
    for g, win in enumerate(POOL_WINDOWS):
        u = u_ref[0, :, g * gw:(g + 1) * gw]
        prev = prev_ref[0, :, g * gw:(g + 1) * gw]
        prev = jnp.where(i == 0, 0.0, prev)
        s = jnp.concatenate([prev, u], axis=0)
        step = 1
        while step < win:
            s = s + pltpu.roll(s, step, axis=0)
            step *= 2
        wsum = s[halo:, :]
        pooled = wsum / jnp.minimum(pos1, float(win)) - u
        y = _dot(pooled.astype(BF16), w_ref[g])
        o_ref[0, :, g * gw:(g + 1) * gw] = (y * sc_ref[:, g * gw:(g + 1) * gw]).astype(o_ref.dtype)


def _pool(z3, pool_w, pool_scale, ts=512):
    bsz, s, _ = z3.shape
    ng, gw, _ = pool_w.shape
    width = ng * gw
    halo = max(POOL_WINDOWS)
    return pl.pallas_call(
        functools.partial(_pool_body, ts=ts, gw=gw, halo=halo),
        out_shape=jax.ShapeDtypeStruct((bsz, s, width), BF16),
        grid=(bsz, s // ts),
        in_specs=[pl.BlockSpec((1, ts, width), lambda b, i: (b, i, 0)),
                  pl.BlockSpec((1, halo, width), lambda b, i: (b, jnp.maximum(i * (ts // halo) - 1, 0), 0)),
                  pl.BlockSpec((ng, gw, gw), lambda b, i: (0, 0, 0)),
                  pl.BlockSpec((1, width), lambda b, i: (0, 0))],
        out_specs=pl.BlockSpec((1, ts, width), lambda b, i: (b, i, 0)),
        compiler_params=_params(("parallel", "parallel"), 48),
        name="pool",
    )(z3, z3, pool_w, pool_scale.reshape(1, width))


def _t5_bucket_table():
    qi = jnp.arange(SWA_BLOCK)[:, None]
    kj = jnp.arange(2 * SWA_BLOCK)[None, :]
    dist = qi + SWA_BLOCK - kj
    band = (dist >= 0) & (dist < SWA_BLOCK)
    d = jnp.maximum(dist, 0)
    exact = REL_BUCKETS // 2
    far = exact + (jnp.log(jnp.maximum(d, exact).astype(F32) / exact)
                   / math.log(REL_MAX_DIST / exact) * (REL_BUCKETS - exact)).astype(jnp.int32)
    bucket = jnp.where(d < exact, d, jnp.minimum(far, REL_BUCKETS - 1))
    return jnp.where(band, bucket, -1).astype(jnp.int32)


def _bias_body(rb_ref, bucket_ref, o_ref):
    h = pl.program_id(0)
    bucket = bucket_ref[...]
    acc = jnp.full(bucket.shape, -jnp.inf, F32)
    for b in range(REL_BUCKETS):
        acc = jnp.where(bucket == b, rb_ref[b, h], acc)
    o_ref[0] = acc


def _bias_table(rel_bias):
    nb, nh = rel_bias.shape
    bucket = _t5_bucket_table()
    return pl.pallas_call(
        _bias_body,
        out_shape=jax.ShapeDtypeStruct((nh, SWA_BLOCK, 2 * SWA_BLOCK), F32),
        grid=(nh,),
        in_specs=[pl.BlockSpec(memory_space=pltpu.SMEM),
                  pl.BlockSpec((SWA_BLOCK, 2 * SWA_BLOCK), lambda h: (0, 0))],
        out_specs=pl.BlockSpec((1, SWA_BLOCK, 2 * SWA_BLOCK), lambda h: (h, 0, 0)),
        compiler_params=_params(("parallel",), 32),
        name="swa_bias",
    )(rel_bias.astype(F32), bucket)


def _swa_body(sink_ref, q_ref, kp_ref, kc_ref, vp_ref, vc_ref, bias_ref, o_ref, *, n_kv):
    n = pl.program_id(1)
    dh = SWA_HEAD_DIM
    col = lax.broadcasted_iota(jnp.int32, (SWA_BLOCK, 2 * SWA_BLOCK), 1)
    no_prev = jnp.logical_and(n == 0, col < SWA_BLOCK)
    scale = dh ** -0.5
    for kv in range(n_kv):
        sl = slice(kv * dh, (kv + 1) * dh)
        k2 = jnp.concatenate([kp_ref[0, :, sl], kc_ref[0, :, sl]], axis=0).astype(BF16)
        v2 = jnp.concatenate([vp_ref[0, :, sl], vc_ref[0, :, sl]], axis=0).astype(BF16)
        for g in range(SWA_GQ):
            h = kv * SWA_GQ + g
            qh = q_ref[0, :, h * dh:(h + 1) * dh].astype(BF16)
            logits = _nt(qh, k2) * scale + bias_ref[h]
            logits = jnp.where(no_prev, -jnp.inf, logits)
            sink = sink_ref[h]
            m = jnp.maximum(jnp.max(logits, axis=-1, keepdims=True), sink)
            p = jnp.exp(logits - m)
            p = p / (jnp.sum(p, axis=-1, keepdims=True) + jnp.exp(sink - m))
            o_ref[0, :, h * dh:(h + 1) * dh] = _dot(p.astype(BF16), v2).astype(o_ref.dtype)


def _swa(z3, sinks, bias_tab, q_off, n_q, n_kv):
    bsz, s, _ = z3.shape
    dh = SWA_HEAD_DIM
    qw, kvw = n_q * dh, n_kv * dh
    qb = q_off // qw
    kb = (q_off + qw) // kvw
    vb = kb + 1
    blk = SWA_BLOCK
    prev = lambda n: jnp.maximum(n - 1, 0)
    return pl.pallas_call(
        functools.partial(_swa_body, n_kv=n_kv),
        out_shape=jax.ShapeDtypeStruct((bsz, s, qw), BF16),
        grid=(bsz, s // blk),
        in_specs=[pl.BlockSpec(memory_space=pltpu.SMEM),
                  pl.BlockSpec((1, blk, qw), lambda b, n: (b, n, qb)),
                  pl.BlockSpec((1, blk, kvw), lambda b, n: (b, prev(n), kb)),
                  pl.BlockSpec((1, blk, kvw), lambda b, n: (b, n, kb)),
                  pl.BlockSpec((1, blk, kvw), lambda b, n: (b, prev(n), vb)),
                  pl.BlockSpec((1, blk, kvw), lambda b, n: (b, n, vb)),
                  pl.BlockSpec((n_q, blk, 2 * blk), lambda b, n: (0, 0, 0))],
        out_specs=pl.BlockSpec((1, blk, qw), lambda b, n: (b, n, 0)),
        compiler_params=_params(("parallel", "parallel"), 48),
        name="swa",
    )(sinks.astype(F32), z3, z3, z3, z3, z3, bias_tab)


def _hgrn_body(q_ref, f_ref, i_ref, g_ref, lbl_ref, nw_ref, o_ref, st_ref, *, n_chunk, layer):
    @pl.when(pl.program_id(2) == 0)
    def _():
        st_ref[...] = jnp.zeros_like(st_ref)

    lg = lbl_ref[:, 0, 0, :]
    e = jnp.exp(lg - jnp.max(lg, axis=0, keepdims=True))
    p = e / jnp.sum(e, axis=0, keepdims=True)
    lb = jnp.sum(p[:layer + 1], axis=0, keepdims=True) - p[0:1]
    nw = nw_ref[...]
    c = HG_CHUNK
    r = lax.broadcasted_iota(jnp.int32, (c, c), 0)
    cc = lax.broadcasted_iota(jnp.int32, (c, c), 1)
    causal = r >= cc
    tri = jnp.where(causal, 1.0, 0.0).astype(BF16)

    def chunk(ci, carry):
        r0 = pl.multiple_of(ci * c, c)
        q = q_ref[0, pl.ds(r0, c), :]
        f = f_ref[0, pl.ds(r0, c), :]
        iv = i_ref[0, pl.ds(r0, c), :].astype(BF16)
        g = g_ref[0, pl.ds(r0, c), :]
        qf = q * _sigmoid(q)
        forget = lb + (1.0 - lb) * _sigmoid(f)
        log_f = jnp.log(forget)
        k_in = 1.0 - forget
        hi = log_f.astype(BF16)
        r1 = log_f - hi.astype(F32)
        mid = r1.astype(BF16)
        lo = (r1 - mid.astype(F32)).astype(BF16)
        cum = _dot(tri, hi) + _dot(tri, mid) + _dot(tri, lo)
        ref = cum[c // 2:c // 2 + 1, :]
        total = cum[c - 1:c, :]
        scores = _nt((qf * jnp.exp(cum - ref)).astype(BF16), (k_in * jnp.exp(ref - cum)).astype(BF16))
        scores = jnp.where(causal, scores, 0.0)
        o_intra = _dot(scores.astype(BF16), iv)
        st = st_ref[...]
        o_inter = _nt((qf * jnp.exp(cum)).astype(BF16), st.astype(BF16))
        upd = _tn(iv, (k_in * jnp.exp(total - cum)).astype(BF16))
        st_ref[...] = st * jnp.exp(total) + upd
        o = o_intra + o_inter
        o = o * lax.rsqrt(jnp.mean(o * o, axis=-1, keepdims=True) + RMS_EPS) * nw
        o = o * (g * _sigmoid(g))
        o_ref[0, pl.ds(r0, c), :] = o.astype(o_ref.dtype)
        return carry

    lax.fori_loop(0, n_chunk, chunk, 0)


def _hgrn(z3, lb_logits, norm_w, layer, n_heads, ts=512):
    bsz, s, _ = z3.shape
    depth = lb_logits.shape[0]
    d = HG_DIM
    spec = lambda off: pl.BlockSpec((1, ts, d), lambda b, h, t: (b, t, off + h))
    return pl.pallas_call(
        functools.partial(_hgrn_body, n_chunk=ts // HG_CHUNK, layer=layer),
        out_shape=jax.ShapeDtypeStruct((bsz, s, n_heads * d), BF16),
        grid=(bsz, n_heads, s // ts),
        in_specs=[spec(0), spec(n_heads), spec(2 * n_heads), spec(3 * n_heads),
                  pl.BlockSpec((depth, 1, 1, d), lambda b, h, t: (0, h, 0, 0)),
                  pl.BlockSpec((1, d), lambda b, h, t: (0, 0))],
        out_specs=pl.BlockSpec((1, ts, d), lambda b, h, t: (b, t, h)),
        scratch_shapes=[pltpu.VMEM((d, d), F32)],
        compiler_params=_params(("parallel", "parallel", "arbitrary"), 32),
        name="hgrn2",
    )(z3, z3, z3, z3, lb_logits.astype(F32).reshape(depth, n_heads, 1, d), norm_w.astype(F32).reshape(1, d))


def _cpow(pw, lam_re, lam_im):
    mag = jnp.exp(pw * lam_re)
    ang = pw * lam_im
    return mag * jnp.cos(ang), mag * jnp.sin(ang)


def _s5_prep_body(ls_ref, ar_ref, ai_ref, arc_ref, aic_ref, br_ref, bi_ref, cr_ref, ci_ref,
                  m_ref, p_ref, q_ref, tr_ref, ti_ref):
    nst = S5_STATE
    lc = S5_CHUNK
    gc = S5_GROUP
    w = lc * gc
    dt = jnp.exp(ls_ref[0])
    a_re, a_im = ar_ref[0], ai_ref[0]
    lam_re, lam_im = dt * a_re, dt * a_im
    ab_re, ab_im = _cpow(1.0, lam_re, lam_im)
    den = a_re * a_re + a_im * a_im
    co_re = ((ab_re - 1.0) * a_re + ab_im * a_im) / den
    co_im = (ab_im * a_re - (ab_re - 1.0) * a_im) / den
    b_re, b_im = br_ref[0], bi_ref[0]
    bb_re = co_re * b_re - co_im * b_im
    bb_im = co_re * b_im + co_im * b_re
    lane = lax.broadcasted_iota(jnp.int32, (w, 2 * nst), 1)
    rowi = lax.broadcasted_iota(jnp.int32, (w, 2 * nst), 0)

    pw = (lc - 1 - rowi // gc).astype(F32)
    ap_re, ap_im = _cpow(pw, lam_re, lam_im)
    tb_re = jnp.concatenate([bb_re] * lc, axis=0)
    tb_im = jnp.concatenate([bb_im] * lc, axis=0)
    p_ref[0] = jnp.where(lane < nst, ap_re * tb_re - ap_im * tb_im,
                         ap_re * tb_im + ap_im * tb_re).astype(p_ref.dtype)

    lam_re_c, lam_im_c = dt * arc_ref[0], dt * aic_ref[0]
    c_re, c_im = cr_ref[0], ci_ref[0]
    lane_q = lax.broadcasted_iota(jnp.int32, (2 * nst, w), 1)
    row_q = lax.broadcasted_iota(jnp.int32, (2 * nst, w), 0)
    tq = (lane_q // gc).astype(F32)

    def c_times_apow(pwq):
        pr, pi = _cpow(pwq, lam_re_c, lam_im_c)
        return jnp.where(row_q < nst, c_re * pr - c_im * pi, -(c_re * pi + c_im * pr))

    q_ref[0] = c_times_apow(tq + 1.0).astype(q_ref.dtype)
    wmat = c_times_apow(tq)

    bbt = jnp.where(lax.broadcasted_iota(jnp.int32, (gc, 2 * nst), 1) < nst, bb_re, bb_im)
    r = jnp.dot(bbt, wmat, preferred_element_type=F32, precision=lax.Precision.HIGHEST)
    lane_r = lax.broadcasted_iota(jnp.int32, (gc, w), 1)
    for t in range(lc):
        blk = r if t == 0 else jnp.where(lane_r >= t * gc, pltpu.roll(r, t * gc, axis=1), 0.0)
        m_ref[0, t * gc:(t + 1) * gc, :] = blk.astype(m_ref.dtype)

    kk = lax.broadcasted_iota(jnp.int32, (16, 2 * nst), 0)
    pws = (lc * jnp.left_shift(1, jnp.minimum(kk, 12))).astype(F32)
    sr, si = _cpow(pws, lam_re, lam_im)
    tr_ref[0] = sr
    ti_ref[0] = jnp.where(lax.broadcasted_iota(jnp.int32, (16, 2 * nst), 1) < nst, -si, si)


def _s5_prep(a_re, a_im, log_step, b_re, b_im, c_re, c_im):
    ng, nst = a_re.shape
    gc, lc = S5_GROUP, S5_CHUNK
    w = lc * gc
    dup = lambda t: jnp.concatenate([t, t], axis=-1)
    ar = dup(a_re.astype(F32))
    ai = dup(a_im.astype(F32))
    brt = dup(jnp.swapaxes(b_re.astype(F32), 1, 2))
    bit = dup(jnp.swapaxes(b_im.astype(F32), 1, 2))
    tile_c = lambda t: jnp.tile(jnp.concatenate([jnp.swapaxes(t.astype(F32), 1, 2)] * 2, axis=1), (1, 1, lc))
    crt, cit = tile_c(c_re), tile_c(c_im)
    g3 = lambda blk: pl.BlockSpec((1,) + blk, lambda g: (g, 0, 0))
    return pl.pallas_call(
        _s5_prep_body,
        out_shape=(jax.ShapeDtypeStruct((ng, w, w), BF16),
                   jax.ShapeDtypeStruct((ng, w, 2 * nst), BF16),
                   jax.ShapeDtypeStruct((ng, 2 * nst, w), BF16),
                   jax.ShapeDtypeStruct((ng, 16, 2 * nst), F32),
                   jax.ShapeDtypeStruct((ng, 16, 2 * nst), F32)),
        grid=(ng,),
        in_specs=[g3((1, 1)), g3((1, 2 * nst)), g3((1, 2 * nst)), g3((2 * nst, 1)), g3((2 * nst, 1)),
                  g3((gc, 2 * nst)), g3((gc, 2 * nst)), g3((2 * nst, w)), g3((2 * nst, w))],
        out_specs=(g3((w, w)), g3((w, 2 * nst)), g3((2 * nst, w)), g3((16, 2 * nst)), g3((16, 2 * nst))),
        compiler_params=_params(("parallel",), 32),
        name="s5_prep",
    )(log_step.astype(F32).reshape(ng, 1, 1), ar.reshape(ng, 1, 2 * nst), ai.reshape(ng, 1, 2 * nst),
      ar.reshape(ng, 2 * nst, 1), ai.reshape(ng, 2 * nst, 1), brt, bit, crt, cit)


def _s5_scan_body(u_ref, m_ref, p_ref, q_ref, tr_ref, ti_ref, y_ref, *, n_steps):
    nst = S5_STATE
    u = u_ref[0, 0]
    x = _dot(u, p_ref[0])
    row = lax.broadcasted_iota(jnp.int32, x.shape, 0)
    tr = tr_ref[0]
    ti = ti_ref[0]
    for k in range(n_steps):
        s = 1 << k
        sh = jnp.where(row >= s, pltpu.roll(x, s, axis=0), 0.0)
        x = x + tr[k:k + 1, :] * sh + ti[k:k + 1, :] * pltpu.roll(sh, nst, axis=1)
    x_in = jnp.where(row >= 1, pltpu.roll(x, 1, axis=0), 0.0)
    y_ref[0, 0] = _dot(u, m_ref[0]) + _dot(x_in.astype(BF16), q_ref[0])


def _s5_scan(ug, m, p, q, tr, ti):
    ng, bsz, nck, w = ug.shape
    nst2 = p.shape[-1]
    n_steps = int(math.log2(nck))
    assert (1 << n_steps) == nck
    g3 = lambda blk: pl.BlockSpec((1,) + blk, lambda g, b: (g, 0, 0))
    return pl.pallas_call(
        functools.partial(_s5_scan_body, n_steps=n_steps),
        out_shape=jax.ShapeDtypeStruct((ng, bsz, nck, w), F32),
        grid=(ng, bsz),
        in_specs=[pl.BlockSpec((1, 1, nck, w), lambda g, b: (g, b, 0, 0)),
                  g3((w, w)), g3((w, nst2)), g3((nst2, w)), g3((16, nst2)), g3((16, nst2))],
        out_specs=pl.BlockSpec((1, 1, nck, w), lambda g, b: (g, b, 0, 0)),
        compiler_params=_params(("parallel", "parallel"), 32),
        name="s5_scan",
    )(ug, m, p, q, tr, ti)


def _glu_body(y_ref, u_ref, d_ref, w_ref, b_ref, o_ref):
    yy = y_ref[...] + d_ref[...] * u_ref[...]
    z = 0.5 * yy * (1.0 + jnp.tanh(math.sqrt(2.0 / math.pi) * (yy + 0.044715 * (yy * yy * yy))))
    gate = _sigmoid(_dot(z.astype(BF16), w_ref[...]) + b_ref[...])
    o_ref[...] = (z * gate).astype(o_ref.dtype)


def _glu(y, z2, u_blk, d_skip, w_glu, b_glu, tm=512):
    n, wd = y.shape
    return pl.pallas_call(
        _glu_body,
        out_shape=jax.ShapeDtypeStruct((n, wd), BF16),
        grid=(n // tm,),
        in_specs=[pl.BlockSpec((tm, wd), lambda i: (i, 0)),
                  pl.BlockSpec((tm, wd), lambda i: (i, u_blk)),
                  pl.BlockSpec((1, wd), lambda i: (0, 0)),
                  pl.BlockSpec((wd, wd), lambda i: (0, 0)),
                  pl.BlockSpec((1, wd), lambda i: (0, 0))],
        out_specs=pl.BlockSpec((tm, wd), lambda i: (i, 0)),
        compiler_params=_params(("parallel",), 32),
        name="s5_glu",
    )(y, z2, d_skip.astype(F32).reshape(1, wd), w_glu.astype(BF16), b_glu.astype(F32).reshape(1, wd))


def _xattn_body(hb_ref, h_ref, wq_ref, k_ref, v_ref, wo_ref, g_ref, b_ref, o_ref, ob_ref, *, alpha):
    d = XA_HEAD_DIM
    q = _dot(hb_ref[...], wq_ref[...]).astype(BF16)
    outs = []
    for hd in range(XA_HEADS):
        sl = slice(hd * d, (hd + 1) * d)
        logits = _nt(q[:, sl], k_ref[0, :, sl]) * (d ** -0.5)
        m = jnp.max(logits, axis=-1, keepdims=True)
        p = jnp.exp(logits - m)
        p = p / jnp.sum(p, axis=-1, keepdims=True)
        outs.append(_dot(p.astype(BF16), v_ref[0, :, sl]))
    o = jnp.concatenate(outs, axis=-1).astype(BF16)
    v = alpha * h_ref[...] + _dot(o, wo_ref[...])
    mu = jnp.mean(v, axis=-1, keepdims=True)
    c = v - mu
    var = jnp.mean(c * c, axis=-1, keepdims=True)
    out = c * lax.rsqrt(var + LN_EPS) * g_ref[...] + b_ref[...]
    o_ref[...] = out
    ob_ref[...] = out.astype(BF16)


def _xattn(h, hb, kmem, vmem, wq, wo, g, b, alpha, seq, tm=256):
    n, d = h.shape
    xw = wq.shape[1]
    ml = kmem.shape[1]
    per_b = seq // tm
    return pl.pallas_call(
        functools.partial(_xattn_body, alpha=alpha),
        out_shape=(jax.ShapeDtypeStruct((n, d), F32), jax.ShapeDtypeStruct((n, d), BF16)),
        grid=(n // tm,),
        in_specs=[pl.BlockSpec((tm, d), lambda i: (i, 0)),
                  pl.BlockSpec((tm, d), lambda i: (i, 0)),
                  pl.BlockSpec((d, xw), lambda i: (0, 0)),
                  pl.BlockSpec((1, ml, xw), lambda i: (i // per_b, 0, 0)),
                  pl.BlockSpec((1, ml, xw), lambda i: (i // per_b, 0, 0)),
                  pl.BlockSpec((xw, d), lambda i: (0, 0)),
                  pl.BlockSpec((1, d), lambda i: (0, 0)),
                  pl.BlockSpec((1, d), lambda i: (0, 0))],
        out_specs=(pl.BlockSpec((tm, d), lambda i: (i, 0)),
                   pl.BlockSpec((tm, d), lambda i: (i, 0))),
        compiler_params=_params(("parallel",), 56),
        name="xattn",
    )(hb, h, wq, kmem, vmem, wo, g.reshape(1, d), b.reshape(1, d))


def _first_max(v, idx, n):
    m = jnp.max(v, axis=0, keepdims=True)
    first = jnp.min(jnp.where(v == m, idx, n), axis=0, keepdims=True)
    return m, first


def _router_body(x_ref, r_ref, rb_ref, o_ref, *, tm):
    ne, ng = N_EXPERTS, N_EXPERT_GROUPS
    per = ne // ng
    logits = lax.dot_general(r_ref[...], x_ref[...], (((1,), (1,)), ((), ())),
                             preferred_element_type=F32, precision=lax.Precision.HIGHEST)
    scores = _sigmoid(logits)
    biased = scores + rb_ref[...]
    gsc = []
    eidx = lax.broadcasted_iota(jnp.int32, (per, tm), 0)
    for g in range(ng):
        v = biased[g * per:(g + 1) * per, :]
        m1, i1 = _first_max(v, eidx, per)
        m2 = jnp.max(jnp.where(eidx == i1, -jnp.inf, v), axis=0, keepdims=True)
        gsc.append(m1 + m2)
    gs = jnp.concatenate(gsc, axis=0)
    gidx = lax.broadcasted_iota(jnp.int32, (ng, tm), 0)
    gsel = jnp.zeros((ng, tm), jnp.bool_)
    for _ in range(TOPK_GROUPS):
        _, first = _first_max(gs, gidx, ng)
        hit = gidx == first
        gsel = jnp.logical_or(gsel, hit)
        gs = jnp.where(hit, -jnp.inf, gs)
    emask = jnp.concatenate([jnp.broadcast_to(gsel[g:g + 1, :], (per, tm)) for g in range(ng)], axis=0)
    cand = jnp.where(emask, biased, -jnp.inf)
    aidx = lax.broadcasted_iota(jnp.int32, (ne, tm), 0)
    picked, ids = [], []
    for _ in range(TOP_K):
        _, first = _first_max(cand, aidx, ne)
        hit = aidx == first
        picked.append(jnp.sum(jnp.where(hit, scores, 0.0), axis=0, keepdims=True))
        ids.append(first.astype(F32))
        cand = jnp.where(hit, -jnp.inf, cand)
    total = picked[0]
    for pk in picked[1:]:
        total = total + pk
    gates = [pk / total * ROUTED_SCALE for pk in picked]
    pad = jnp.zeros((LANES - 2 * TOP_K, tm), F32)
    o_ref[...] = jnp.concatenate(gates + ids + [pad], axis=0).T


def _router(h, router, router_bias, tm=512):
    n, d = h.shape
    ne = router.shape[0]
    return pl.pallas_call(
        functools.partial(_router_body, tm=tm),
        out_shape=jax.ShapeDtypeStruct((n, LANES), F32),
        grid=(n // tm,),
        in_specs=[pl.BlockSpec((tm, d), lambda i: (i, 0)),
                  pl.BlockSpec((ne, d), lambda i: (0, 0)),
                  pl.BlockSpec((ne, 1), lambda i: (0, 0))],
        out_specs=pl.BlockSpec((tm, LANES), lambda i: (i, 0)),
        compiler_params=_params(("parallel",), 48),
        name="moe_router",
    )(h, router.astype(F32), router_bias.astype(F32).reshape(ne, 1))


def _ffn_tile(x, wgu, wd_ref, o_ref, accumulate, n_col=4):
    hdim = wgu.shape[1] // 2
    gu = _dot(x, wgu)
    gte = gu[:, :hdim]
    hid = ((gte * _sigmoid(gte)) * gu[:, hdim:]).astype(BF16)
    d = o_ref.shape[1]
    cw = d // n_col
    for j in range(n_col):
        y = _dot(hid, wd_ref[0, :, j * cw:(j + 1) * cw])
        if accumulate:
            o_ref[:, j * cw:(j + 1) * cw] += y
        else:
            o_ref[:, j * cw:(j + 1) * cw] = y


def _shared_body(x_ref, wgu_ref, wd_ref, o_ref):
    @pl.when(pl.program_id(1) == 0)
    def _():
        o_ref[...] = jnp.zeros_like(o_ref)

    _ffn_tile(x_ref[...], wgu_ref[0], wd_ref, o_ref, accumulate=True)


def _shared_ffn(hb, wgu, wd, tm=512):
    n, d = hb.shape
    return pl.pallas_call(
        _shared_body,
        out_shape=jax.ShapeDtypeStruct((n, d), F32),
        grid=(n // tm, wgu.shape[0]),
        in_specs=[pl.BlockSpec((tm, d), lambda i, e: (i, 0)),
                  pl.BlockSpec((1, d, wgu.shape[2]), lambda i, e: (e, 0, 0)),
                  pl.BlockSpec((1, wd.shape[1], d), lambda i, e: (e, 0, 0))],
        out_specs=pl.BlockSpec((tm, d), lambda i, e: (i, 0)),
        compiler_params=_params(("parallel", "arbitrary"), 56),
        name="moe_shared",
    )(hb, wgu, wd)


MOE_ROW_TILE = 512
DMA_UNROLL = 8


def _start_row_gather(ids_ref, src_hbm, dst, sem, n_rows):
    def body(j, c):
        for u in range(DMA_UNROLL):
            r = j * DMA_UNROLL + u
            pltpu.make_async_copy(src_hbm.at[pl.ds(ids_ref[0, 0, r], 1)], dst.at[pl.ds(r, 1)], sem).start(priority=u % 2)
        return c

    lax.fori_loop(0, n_rows // DMA_UNROLL, body, 0)


def _wait_row_gather(src_hbm, dst, sem, n_rows):
    pltpu.make_async_copy(src_hbm.at[pl.ds(0, n_rows)], dst, sem).wait()


def _moe_sparse_body(te_ref, nv_ref, tok_ref, tokn_ref, h_hbm, wgu_ref, wd_ref, y_ref, xbuf, sem, *, tr):
    i = pl.program_id(0)
    nv = nv_ref[0]
    slot = i % 2

    @pl.when(i == 0)
    def _():
        _start_row_gather(tok_ref, h_hbm, xbuf.at[0], sem.at[0], tr)

    @pl.when(i + 1 < nv)
    def _():
        _start_row_gather(tokn_ref, h_hbm, xbuf.at[1 - slot], sem.at[1 - slot], tr)

    @pl.when(i < nv)
    def _():
        _wait_row_gather(h_hbm, xbuf.at[slot], sem.at[slot], tr)
        _ffn_tile(xbuf[slot].astype(BF16), wgu_ref[0], wd_ref, y_ref, accumulate=False)

    @pl.when(i >= nv)
    def _():
        y_ref[...] = jnp.zeros_like(y_ref)


def _moe_sparse(h, wgu, wd, tok3, tile_expert, n_valid):
    n, d = h.shape
    n_tiles, _, tr = tok3.shape
    last = lambda i, nv: jnp.minimum(i, nv[0] - 1)
    grid_spec = pltpu.PrefetchScalarGridSpec(
        num_scalar_prefetch=2,
        grid=(n_tiles,),
        in_specs=[pl.BlockSpec((1, 1, tr), lambda i, te, nv: (i, 0, 0), memory_space=pltpu.SMEM),
                  pl.BlockSpec((1, 1, tr), lambda i, te, nv: (jnp.minimum(i + 1, n_tiles - 1), 0, 0),
                               memory_space=pltpu.SMEM),
                  pl.BlockSpec(memory_space=pl.ANY),
                  pl.BlockSpec((1, d, wgu.shape[2]), lambda i, te, nv: (te[last(i, nv)], 0, 0)),
                  pl.BlockSpec((1, wd.shape[1], d), lambda i, te, nv: (te[last(i, nv)], 0, 0))],
        out_specs=pl.BlockSpec((tr, d), lambda i, te, nv: (i, 0)),
        scratch_shapes=[pltpu.VMEM((2, tr, d), F32), pltpu.SemaphoreType.DMA((2,))])
    return pl.pallas_call(
        functools.partial(_moe_sparse_body, tr=tr),
        out_shape=jax.ShapeDtypeStruct((n_tiles * tr, d), F32),
        grid_spec=grid_spec,
        compiler_params=_params(("arbitrary",), 60),
        name="moe_sparse",
    )(tile_expert, n_valid, tok3, tok3, h, wgu, wd)


def _moe_combine_body(sl_ref, sln_ref, y_hbm, gate_ref, sh_ref, h_ref, g_ref, b_ref, o_ref, ob_ref, buf, sem,
                      *, tm, alpha, n_steps):
    i = pl.program_id(0)
    slot = i % 2
    rows = TOP_K * tm

    @pl.when(i == 0)
    def _():
        _start_row_gather(sl_ref, y_hbm, buf.at[0], sem.at[0], rows)

    @pl.when(i + 1 < n_steps)
    def _():
        _start_row_gather(sln_ref, y_hbm, buf.at[1 - slot], sem.at[1 - slot], rows)

    _wait_row_gather(y_hbm, buf.at[slot], sem.at[slot], rows)
    acc = sh_ref[...]
    for k in range(TOP_K):
        acc = acc + gate_ref[:, k:k + 1] * buf[slot, k * tm:(k + 1) * tm, :]
    v = alpha * h_ref[...] + acc
    mu = jnp.mean(v, axis=-1, keepdims=True)
    c = v - mu
    var = jnp.mean(c * c, axis=-1, keepdims=True)
    out = c * lax.rsqrt(var + LN_EPS) * g_ref[...] + b_ref[...]
    o_ref[...] = out
    ob_ref[...] = out.astype(BF16)


def _moe_combine(y_sorted, slot3, rout, shared, h, g, b, alpha):
    n, d = h.shape
    n_steps, _, rows = slot3.shape
    tm = rows // TOP_K
    row = lambda i: (i, 0)
    return pl.pallas_call(
        functools.partial(_moe_combine_body, tm=tm, alpha=alpha, n_steps=n_steps),
        out_shape=(jax.ShapeDtypeStruct((n, d), F32), jax.ShapeDtypeStruct((n, d), BF16)),
        grid=(n_steps,),
        in_specs=[pl.BlockSpec((1, 1, rows), lambda i: (i, 0, 0), memory_space=pltpu.SMEM),
                  pl.BlockSpec((1, 1, rows), lambda i: (jnp.minimum(i + 1, n_steps - 1), 0, 0),
                               memory_space=pltpu.SMEM),
                  pl.BlockSpec(memory_space=pl.ANY),
                  pl.BlockSpec((tm, LANES), row),
                  pl.BlockSpec((tm, d), row),
                  pl.BlockSpec((tm, d), row),
                  pl.BlockSpec((1, d), lambda i: (0, 0)),
                  pl.BlockSpec((1, d), lambda i: (0, 0))],
        out_specs=(pl.BlockSpec((tm, d), row), pl.BlockSpec((tm, d), row)),
        scratch_shapes=[pltpu.VMEM((2, rows, d), F32), pltpu.SemaphoreType.DMA((2,))],
        compiler_params=_params(("arbitrary",), 48),
        name="moe_combine",
    )(slot3, slot3, y_sorted, rout, shared, h, g.reshape(1, d), b.reshape(1, d))


def _dispatch_plan(rout, n_experts, tr, tm):
    n = rout.shape[0]
    ids = rout[:, TOP_K:2 * TOP_K].astype(jnp.int32)
    onehot = jnp.sum((ids[:, :, None] == jnp.arange(n_experts)[None, None, :]).astype(jnp.int32), axis=1)
    csum = jnp.cumsum(onehot, axis=0)
    counts = csum[-1]
    padded = (counts + tr - 1) // tr * tr
    seg_end = jnp.cumsum(padded)
    seg_start = seg_end - padded
    rank = jnp.take_along_axis(csum - onehot, ids, axis=1)
    slot = seg_start[ids] + rank
    n_tiles = n * TOP_K // tr + n_experts
    tok = jnp.zeros((n_tiles * tr,), jnp.int32).at[slot.reshape(-1)].set(
        jnp.repeat(jnp.arange(n, dtype=jnp.int32), TOP_K))
    tile_expert = jnp.minimum(jnp.searchsorted(seg_end, jnp.arange(n_tiles) * tr, side="right"),
                              n_experts - 1).astype(jnp.int32)
    n_valid = (seg_end[-1:] // tr).astype(jnp.int32)
    slot3 = slot.reshape(n // tm, tm, TOP_K).transpose(0, 2, 1).reshape(n // tm, 1, TOP_K * tm)
    return tok.reshape(n_tiles, 1, tr), tile_expert, n_valid, slot3.astype(jnp.int32)


def kernel(x, mem, ev_w_in, ev_pool_w, ev_pool_scale, ev_sinks, ev_w_out, rel_bias, od_w_in, hg_lb_logits, od_hg_norm, od_a_re, od_a_im, od_log_step, od_b_re, od_b_im, od_c_re, od_c_im, od_d_skip, od_w_glu, od_b_glu, od_w_out, xa_wq, xa_wk, xa_wv, xa_wo, moe_router, moe_bias, moe_w_gate, moe_w_up, moe_w_down, sh_w_gate, sh_w_up, sh_w_down, ln_g, ln_b):
    bsz, seq, d = x.shape
    n = bsz * seq
    depth = ln_g.shape[0]
    alpha = (2 * depth) ** 0.25
    memb = mem.astype(BF16).reshape(bsz * mem.shape[1], d)

    h = x.astype(F32).reshape(n, d)
    hb = h.astype(BF16)
    bias_tab = _bias_table(rel_bias)

    for l in range(depth):
        j = l // 2
        if l % 2 == 0:
            pool_w = ev_pool_w[j]
            pool_width = pool_w.shape[0] * pool_w.shape[1]
            n_q = ev_sinks.shape[1]
            n_kv = n_q // SWA_GQ
            z = _mm(hb, ev_w_in[j].astype(BF16), F32)
            z3 = z.reshape(bsz, seq, z.shape[1])
            y_a = _pool(z3, pool_w.astype(BF16), ev_pool_scale[j].astype(F32))
            y_b = _swa(z3, ev_sinks[j], bias_tab, pool_width, n_q, n_kv)
            y = jnp.concatenate([y_a, y_b], axis=-1).reshape(n, d)
            mix = _mm(y, ev_w_out[j].astype(BF16), F32)
        else:
            n_heads = hg_lb_logits.shape[1] // HG_DIM
            hgw = n_heads * HG_DIM
            ng, nst = od_a_re.shape[1], od_a_re.shape[2]
            s5w = ng * S5_GROUP
            z = _mm(hb, od_w_in[j].astype(BF16), F32)
            z3 = z.reshape(bsz, seq, z.shape[1])
            y_c = _hgrn(z3, hg_lb_logits, od_hg_norm[j], l, n_heads)
            m_mat, p_mat, q_mat, t_re, t_im = _s5_prep(od_a_re[j], od_a_im[j], od_log_step[j], od_b_re[j],
                                                        od_b_im[j], od_c_re[j], od_c_im[j])
            nck = seq // S5_CHUNK
            u = z3[:, :, 4 * hgw:].astype(BF16).reshape(bsz, nck, S5_CHUNK, ng, S5_GROUP)
            ug = u.transpose(3, 0, 1, 2, 4).reshape(ng, bsz, nck, S5_CHUNK * S5_GROUP)
            yg = _s5_scan(ug, m_mat, p_mat, q_mat, t_re, t_im)
            y_s = yg.reshape(ng, bsz, nck, S5_CHUNK, S5_GROUP).transpose(1, 2, 3, 0, 4).reshape(n, s5w)
            y_d = _glu(y_s, z, (4 * hgw) // s5w, od_d_skip[j], od_w_glu[j], od_b_glu[j])
            y = jnp.concatenate([y_c.reshape(n, hgw), y_d], axis=-1)
            mix = _mm(y, od_w_out[j].astype(BF16), F32)
        h, hb = _add_ln(h, mix, ln_g[l, 0], ln_b[l, 0], alpha)

        xw = xa_wq.shape[2]
        kmem = _mm(memb, xa_wk[l].astype(BF16), BF16).reshape(bsz, -1, xw)
        vmem = _mm(memb, xa_wv[l].astype(BF16), BF16).reshape(bsz, -1, xw)
        h, hb = _xattn(h, hb, kmem, vmem, xa_wq[l].astype(BF16), xa_wo[l].astype(BF16),
                       ln_g[l, 1], ln_b[l, 1], alpha, seq)

        n_exp = moe_router.shape[1]
        n_shared = sh_w_gate.shape[2] // EXPERT_HIDDEN
        rout = _router(h, moe_router[l], moe_bias[l])
        tok3, tile_expert, n_valid, slot3 = _dispatch_plan(rout, n_exp, MOE_ROW_TILE, 64)
        split = lambda t: t.reshape(d, n_shared, EXPERT_HIDDEN).transpose(1, 0, 2)
        sh_wgu = jnp.concatenate([split(sh_w_gate[l]), split(sh_w_up[l])], axis=-1).astype(BF16)
        sh_wd = sh_w_down[l].reshape(n_shared, EXPERT_HIDDEN, d).astype(BF16)
        shared = _shared_ffn(hb, sh_wgu, sh_wd)
        wgu = jnp.concatenate([moe_w_gate[l], moe_w_up[l]], axis=-1).astype(BF16)
        y_sorted = _moe_sparse(h, wgu, moe_w_down[l].astype(BF16), tok3, tile_expert, n_valid)
        h, hb = _moe_combine(y_sorted, slot3, rout, shared, h, ln_g[l, 2], ln_b[l, 2], alpha)

    return h.reshape(bsz, seq, d)
```

```python
import functools
import math

import jax
import jax.numpy as jnp
from jax import lax
from jax.experimental import pallas as pl
from jax.experimental.pallas import tpu as pltpu

F32 = jnp.float32
BF16 = jnp.bfloat16

V7X_VMEM_BYTES = 64 * 1024 * 1024
LANES = 128
SUBLANES = 8

POOL_WINDOWS = (2, 4, 8, 16)
SWA_HEAD_DIM = 64
SWA_GQ = 8
SWA_BLOCK = 128
REL_BUCKETS = 32
REL_MAX_DIST = 128
HG_DIM = 128
HG_CHUNK = 64
RMS_EPS = 1e-6
S5_GROUP = 16
S5_STATE = 64
S5_CHUNK = 16
XA_HEADS = 4
XA_HEAD_DIM = 128
N_EXPERTS = 64
TOP_K = 8
N_EXPERT_GROUPS = 8
TOPK_GROUPS = 4
EXPERT_HIDDEN = 256
ROUTED_SCALE = 2.5
LN_EPS = 1e-5


def _params(sem, vmem_mb):
    return pltpu.CompilerParams(dimension_semantics=sem, vmem_limit_bytes=vmem_mb * 1024 * 1024)


def _nt(a, b):
    return lax.dot_general(a, b, (((1,), (1,)), ((), ())), preferred_element_type=F32)


def _tn(a, b):
    return lax.dot_general(a, b, (((0,), (0,)), ((), ())), preferred_element_type=F32)


def _dot(a, b):
    return jnp.dot(a, b, preferred_element_type=F32)


def _sigmoid(x):
    return 0.5 * (jnp.tanh(0.5 * x) + 1.0)


def _mm_body(x_ref, w_ref, o_ref):
    o_ref[...] = _dot(x_ref[...], w_ref[...]).astype(o_ref.dtype)


def _mm(x, w, out_dtype, tm=1024, tn=512):
    m, k = x.shape
    n = w.shape[1]
    tm = min(tm, m)
    tn = min(tn, n)
    return pl.pallas_call(
        _mm_body,
        out_shape=jax.ShapeDtypeStruct((m, n), out_dtype),
        grid=(m // tm, n // tn),
        in_specs=[pl.BlockSpec((tm, k), lambda i, j: (i, 0)),
                  pl.BlockSpec((k, tn), lambda i, j: (0, j))],
        out_specs=pl.BlockSpec((tm, tn), lambda i, j: (i, j)),
        compiler_params=_params(("parallel", "parallel"), 48),
        name="mm",
    )(x, w)


def _mm2_body(a_ref, b_ref, w_ref, o_ref):
    ka = a_ref.shape[1]
    o_ref[...] = (_dot(a_ref[...], w_ref[:ka, :]) + _dot(b_ref[...], w_ref[ka:, :])).astype(o_ref.dtype)


def _mm2(a, b, w, out_dtype, tm=1024, tn=512):
    m, ka = a.shape
    kb = b.shape[1]
    n = w.shape[1]
    return pl.pallas_call(
        _mm2_body,
        out_shape=jax.ShapeDtypeStruct((m, n), out_dtype),
        grid=(m // tm, n // tn),
        in_specs=[pl.BlockSpec((tm, ka), lambda i, j: (i, 0)),
                  pl.BlockSpec((tm, kb), lambda i, j: (i, 0)),
                  pl.BlockSpec((ka + kb, tn), lambda i, j: (0, j))],
        out_specs=pl.BlockSpec((tm, tn), lambda i, j: (i, j)),
        compiler_params=_params(("parallel", "parallel"), 48),
        name="mm2",
    )(a, b, w)


def _add_ln_body(h_ref, y_ref, g_ref, b_ref, o_ref, ob_ref, *, alpha):
    v = alpha * h_ref[...] + y_ref[...].astype(F32)
    mu = jnp.mean(v, axis=-1, keepdims=True)
    c = v - mu
    var = jnp.mean(c * c, axis=-1, keepdims=True)
    out = c * lax.rsqrt(var + LN_EPS) * g_ref[...] + b_ref[...]
    o_ref[...] = out
    ob_ref[...] = out.astype(BF16)


def _add_ln(h, y, g, b, alpha, tm=256):
    n, d = h.shape
    return pl.pallas_call(
        functools.partial(_add_ln_body, alpha=alpha),
        out_shape=(jax.ShapeDtypeStruct((n, d), F32), jax.ShapeDtypeStruct((n, d), BF16)),
        grid=(n // tm,),
        in_specs=[pl.BlockSpec((tm, d), lambda i: (i, 0)),
                  pl.BlockSpec((tm, d), lambda i: (i, 0)),
                  pl.BlockSpec((1, d), lambda i: (0, 0)),
                  pl.BlockSpec((1, d), lambda i: (0, 0))],
        out_specs=(pl.BlockSpec((tm, d), lambda i: (i, 0)),
                   pl.BlockSpec((tm, d), lambda i: (i, 0))),
        compiler_params=_params(("parallel",), 48),
        name="add_ln",
    )(h, y, g.reshape(1, d), b.reshape(1, d))


def _pool_body(u_ref, prev_ref, w_ref, sc_ref, o_ref, *, ts, gw, halo):
    i = pl.program_id(1)
    row = lax.broadcasted_iota(jnp.int32, (ts, gw), 0)
    pos1 = (row + i * ts + 1).astype(F32)
    for g, win in enumerate(POOL_WINDOWS):
        u = u_ref[0, :, g * gw:(g + 1) * gw]
        prev = prev_ref[0, :, g * gw:(g + 1) * gw]
        prev = jnp.where(i == 0, 0.0, prev)
        s = jnp.concatenate([prev, u], axis=0)
        step = 1
        while step < win:
            s = s + pltpu.roll(s, step, axis=0)
            step *= 2
        wsum = s[halo:, :]
        pooled = wsum / jnp.minimum(pos1, float(win)) - u
        y = _dot(pooled.astype(BF16), w_ref[g])
        o_ref[0, :, g * gw:(g + 1) * gw] = (y * sc_ref[:, g * gw:(g + 1) * gw]).astype(o_ref.dtype)


def _pool(z3, pool_w, pool_scale, ts=512):
    bsz, s, _ = z3.shape
    ng, gw, _ = pool_w.shape
    width = ng * gw
    halo = max(POOL_WINDOWS)
    return pl.pallas_call(
        functools.partial(_pool_body, ts=ts, gw=gw, halo=halo),
        out_shape=jax.ShapeDtypeStruct((bsz, s, width), BF16),
        grid=(bsz, s // ts),
        in_specs=[pl.BlockSpec((1, ts, width), lambda b, i: (b, i, 0)),
                  pl.BlockSpec((1, halo, width), lambda b, i: (b, jnp.maximum(i * (ts // halo) - 1, 0), 0)),
                  pl.BlockSpec((ng, gw, gw), lambda b, i: (0, 0, 0)),
                  pl.BlockSpec((1, width), lambda b, i: (0, 0))],
        out_specs=pl.BlockSpec((1, ts, width), lambda b, i: (b, i, 0)),
        compiler_params=_params(("parallel", "parallel"), 48),
        name="pool",
    )(z3, z3, pool_w, pool_scale.reshape(1, width))


def _t5_bucket_table():
    qi = jnp.arange(SWA_BLOCK)[:, None]
    kj = jnp.arange(2 * SWA_BLOCK)[None, :]
    dist = qi + SWA_BLOCK - kj
    band = (dist >= 0) & (dist < SWA_BLOCK)
    d = jnp.maximum(dist, 0)
    exact = REL_BUCKETS // 2
    far = exact + (jnp.log(jnp.maximum(d, exact).astype(F32) / exact)
                   / math.log(REL_MAX_DIST / exact) * (REL_BUCKETS - exact)).astype(jnp.int32)
    bucket = jnp.where(d < exact, d, jnp.minimum(far, REL_BUCKETS - 1))
    return jnp.where(band, bucket, -1).astype(jnp.int32)


def _bias_body(rb_ref, bucket_ref, o_ref):
    h = pl.program_id(0)
    bucket = bucket_ref[...]
    acc = jnp.full(bucket.shape, -jnp.inf, F32)
    for b in range(REL_BUCKETS):
        acc = jnp.where(bucket == b, rb_ref[b, h], acc)
    o_ref[0] = acc


def _bias_table(rel_bias):
    nb, nh = rel_bias.shape
    bucket = _t5_bucket_table()
    return pl.pallas_call(
        _bias_body,
        out_shape=jax.ShapeDtypeStruct((nh, SWA_BLOCK, 2 * SWA_BLOCK), F32),
        grid=(nh,),
        in_specs=[pl.BlockSpec(memory_space=pltpu.SMEM),
                  pl.BlockSpec((SWA_BLOCK, 2 * SWA_BLOCK), lambda h: (0, 0))],
        out_specs=pl.BlockSpec((1, SWA_BLOCK, 2 * SWA_BLOCK), lambda h: (h, 0, 0)),
        compiler_params=_params(("parallel",), 32),
        name="swa_bias",
    )(rel_bias.astype(F32), bucket)


def _swa_body(sink_ref, q_ref, kp_ref, kc_ref, vp_ref, vc_ref, bias_ref, o_ref, *, n_kv):
    n = pl.program_id(1)
    dh = SWA_HEAD_DIM
    col = lax.broadcasted_iota(jnp.int32, (SWA_BLOCK, 2 * SWA_BLOCK), 1)
    no_prev = jnp.logical_and(n == 0, col < SWA_BLOCK)
    scale = dh ** -0.5
    for kv in range(n_kv):
        sl = slice(kv * dh, (kv + 1) * dh)
        k2 = jnp.concatenate([kp_ref[0, :, sl], kc_ref[0, :, sl]], axis=0).astype(BF16)
        v2 = jnp.concatenate([vp_ref[0, :, sl], vc_ref[0, :, sl]], axis=0).astype(BF16)
        for g in range(SWA_GQ):
            h = kv * SWA_GQ + g
            qh = q_ref[0, :, h * dh:(h + 1) * dh].astype(BF16)
            logits = _nt(qh, k2) * scale + bias_ref[h]
            logits = jnp.where(no_prev, -jnp.inf, logits)
            sink = sink_ref[h]
            m = jnp.maximum(jnp.max(logits, axis=-1, keepdims=True), sink)
            p = jnp.exp(logits - m)
            p = p / (jnp.sum(p, axis=-1, keepdims=True) + jnp.exp(sink - m))
            o_ref[0, :, h * dh:(h + 1) * dh] = _dot(p.astype(BF16), v2).astype(o_ref.dtype)


def _swa(z3, sinks, bias_tab, q_off, n_q, n_kv):
    bsz, s, _ = z3.shape
    dh = SWA_HEAD_DIM
    qw, kvw = n_q * dh, n_kv * dh
    qb = q_off // qw
    kb = (q_off + qw) // kvw
    vb = kb + 1
    blk = SWA_BLOCK
    prev = lambda n: jnp.maximum(n - 1, 0)
    return pl.pallas_call(
        functools.partial(_swa_body, n_kv=n_kv),
        out_shape=jax.ShapeDtypeStruct((bsz, s, qw), BF16),
        grid=(bsz, s // blk),
        in_specs=[pl.BlockSpec(memory_space=pltpu.SMEM),
                  pl.BlockSpec((1, blk, qw), lambda b, n: (b, n, qb)),
                  pl.BlockSpec((1, blk, kvw), lambda b, n: (b, prev(n), kb)),
                  pl.BlockSpec((1, blk, kvw), lambda b, n: (b, n, kb)),
                  pl.BlockSpec((1, blk, kvw), lambda b, n: (b, prev(n), vb)),
                  pl.BlockSpec((1, blk, kvw), lambda b, n: (b, n, vb)),
                  pl.BlockSpec((n_q, blk, 2 * blk), lambda b, n: (0, 0, 0))],
        out_specs=pl.BlockSpec((1, blk, qw), lambda b, n: (b, n, 0)),
        compiler_params=_params(("parallel", "parallel"), 48),
        name="swa",
    )(sinks.astype(F32), z3, z3, z3, z3, z3, bias_tab)


def _hgrn_body(q_ref, f_ref, i_ref, g_ref, lbl_ref, nw_ref, o_ref, st_ref, *, n_chunk, layer):
    @pl.when(pl.program_id(2) == 0)
    def _():
        st_ref[...] = jnp.zeros_like(st_ref)

    lg = lbl_ref[:, 0, 0, :]
    e = jnp.exp(lg - jnp.max(lg, axis=0, keepdims=True))
    p = e / jnp.sum(e, axis=0, keepdims=True)
    lb = jnp.sum(p[:layer + 1], axis=0, keepdims=True) - p[0:1]
    nw = nw_ref[...]
    c = HG_CHUNK
    r = lax.broadcasted_iota(jnp.int32, (c, c), 0)
    cc = lax.broadcasted_iota(jnp.int32, (c, c), 1)
    causal = r >= cc
    tri = jnp.where(causal, 1.0, 0.0).astype(BF16)

    d = HG_DIM
    chunks = lambda t: [t[ci * c:(ci + 1) * c] for ci in range(n_chunk)]
    q = q_ref[0]
    iv = chunks(i_ref[0].astype(BF16))
    g = g_ref[0]
    qf = q * _sigmoid(q)
    forget = lb + (1.0 - lb) * _sigmoid(f_ref[0])
    log_f = jnp.log(forget)
    k_in = 1.0 - forget
    hi = log_f.astype(BF16)
    r1 = log_f - hi.astype(F32)
    mid = r1.astype(BF16)
    lo = (r1 - mid.astype(F32)).astype(BF16)
    cum = jnp.concatenate([_dot(tri, a) + _dot(tri, b) + _dot(tri, e3)
                           for a, b, e3 in zip(chunks(hi), chunks(mid), chunks(lo))], axis=0)
    cum3 = cum.reshape(n_chunk, c, d)
    ref = cum3[:, c // 2:c // 2 + 1, :]
    total = cum3[:, c - 1:c, :]
    qa = qf.reshape(n_chunk, c, d) * jnp.exp(cum3 - ref)
    kb = k_in.reshape(n_chunk, c, d) * jnp.exp(ref - cum3)
    qe = chunks((qa * jnp.exp(ref)).astype(BF16).reshape(n_chunk * c, d))
    ku = chunks((kb * jnp.exp(total - ref)).astype(BF16).reshape(n_chunk * c, d))
    qa = chunks(qa.astype(BF16).reshape(n_chunk * c, d))
    kb = chunks(kb.astype(BF16).reshape(n_chunk * c, d))
    decay = jnp.exp(total)
    st = st_ref[...]
    outs = []
    for ci in range(n_chunk):
        scores = jnp.where(causal, _nt(qa[ci], kb[ci]), 0.0)
        o_intra = _dot(scores.astype(BF16), iv[ci])
        o_inter = _nt(qe[ci], st.astype(BF16))
        st = st * decay[ci] + _tn(iv[ci], ku[ci])
        outs.append(o_intra + o_inter)
    st_ref[...] = st
    o = jnp.concatenate(outs, axis=0)
    o = o * lax.rsqrt(jnp.mean(o * o, axis=-1, keepdims=True) + RMS_EPS) * nw
    o_ref[0] = (o * (g * _sigmoid(g))).astype(o_ref.dtype)


def _hgrn(z3, lb_logits, norm_w, layer, n_heads, ts=512):
    bsz, s, _ = z3.shape
    depth = lb_logits.shape[0]
    d = HG_DIM
    spec = lambda off: pl.BlockSpec((1, ts, d), lambda b, h, t: (b, t, off + h))
    return pl.pallas_call(
        functools.partial(_hgrn_body, n_chunk=ts // HG_CHUNK, layer=layer),
        out_shape=jax.ShapeDtypeStruct((bsz, s, n_heads * d), BF16),
        grid=(bsz, n_heads, s // ts),
        in_specs=[spec(0), spec(n_heads), spec(2 * n_heads), spec(3 * n_heads),
                  pl.BlockSpec((depth, 1, 1, d), lambda b, h, t: (0, h, 0, 0)),
                  pl.BlockSpec((1, d), lambda b, h, t: (0, 0))],
        out_specs=pl.BlockSpec((1, ts, d), lambda b, h, t: (b, t, h)),
        scratch_shapes=[pltpu.VMEM((d, d), F32)],
        compiler_params=_params(("parallel", "parallel", "arbitrary"), 32),
        name="hgrn2",
    )(z3, z3, z3, z3, lb_logits.astype(F32).reshape(depth, n_heads, 1, d), norm_w.astype(F32).reshape(1, d))


def _cpow(pw, lam_re, lam_im):
    mag = jnp.exp(pw * lam_re)
    ang = pw * lam_im
    return mag * jnp.cos(ang), mag * jnp.sin(ang)


def _s5_prep_body(ls_ref, ar_ref, ai_ref, arc_ref, aic_ref, br_ref, bi_ref, cr_ref, ci_ref,
                  m_ref, p_ref, q_ref, tr_ref, ti_ref):
    nst = S5_STATE
    lc = S5_CHUNK
    gc = S5_GROUP
    w = lc * gc
    dt = jnp.exp(ls_ref[0])
    a_re, a_im = ar_ref[0], ai_ref[0]
    lam_re, lam_im = dt * a_re, dt * a_im
    ab_re, ab_im = _cpow(1.0, lam_re, lam_im)
    den = a_re * a_re + a_im * a_im
    co_re = ((ab_re - 1.0) * a_re + ab_im * a_im) / den
    co_im = (ab_im * a_re - (ab_re - 1.0) * a_im) / den
    b_re, b_im = br_ref[0], bi_ref[0]
    bb_re = co_re * b_re - co_im * b_im
    bb_im = co_re * b_im + co_im * b_re
    lane = lax.broadcasted_iota(jnp.int32, (w, 2 * nst), 1)
    rowi = lax.broadcasted_iota(jnp.int32, (w, 2 * nst), 0)

    pw = (lc - 1 - rowi // gc).astype(F32)
    ap_re, ap_im = _cpow(pw, lam_re, lam_im)
    tb_re = jnp.concatenate([bb_re] * lc, axis=0)
    tb_im = jnp.concatenate([bb_im] * lc, axis=0)
    p_ref[0] = jnp.where(lane < nst, ap_re * tb_re - ap_im * tb_im,
                         ap_re * tb_im + ap_im * tb_re).astype(p_ref.dtype)

    lam_re_c, lam_im_c = dt * arc_ref[0], dt * aic_ref[0]
    c_re, c_im = cr_ref[0], ci_ref[0]
    lane_q = lax.broadcasted_iota(jnp.int32, (2 * nst, w), 1)
    row_q = lax.broadcasted_iota(jnp.int32, (2 * nst, w), 0)
    tq = (lane_q // gc).astype(F32)

    def c_times_apow(pwq):
        pr, pi = _cpow(pwq, lam_re_c, lam_im_c)
        return jnp.where(row_q < nst, c_re * pr - c_im * pi, -(c_re * pi + c_im * pr))

    q_ref[0] = c_times_apow(tq + 1.0).astype(q_ref.dtype)
    wmat = c_times_apow(tq)

    bbt = jnp.where(lax.broadcasted_iota(jnp.int32, (gc, 2 * nst), 1) < nst, bb_re, bb_im)
    r = jnp.dot(bbt, wmat, preferred_element_type=F32, precision=lax.Precision.HIGHEST)
    lane_r = lax.broadcasted_iota(jnp.int32, (gc, w), 1)
    for t in range(lc):
        blk = r if t == 0 else jnp.where(lane_r >= t * gc, pltpu.roll(r, t * gc, axis=1), 0.0)
        m_ref[0, t * gc:(t + 1) * gc, :] = blk.astype(m_ref.dtype)

    kk = lax.broadcasted_iota(jnp.int32, (16, 2 * nst), 0)
    pws = (lc * jnp.left_shift(1, jnp.minimum(kk, 12))).astype(F32)
    sr, si = _cpow(pws, lam_re, lam_im)
    tr_ref[0] = sr
    ti_ref[0] = jnp.where(lax.broadcasted_iota(jnp.int32, (16, 2 * nst), 1) < nst, -si, si)


def _s5_prep(a_re, a_im, log_step, b_re, b_im, c_re, c_im):
    ng, nst = a_re.shape
    gc, lc = S5_GROUP, S5_CHUNK
    w = lc * gc
    dup = lambda t: jnp.concatenate([t, t], axis=-1)
    ar = dup(a_re.astype(F32))
    ai = dup(a_im.astype(F32))
    brt = dup(jnp.swapaxes(b_re.astype(F32), 1, 2))
    bit = dup(jnp.swapaxes(b_im.astype(F32), 1, 2))
    tile_c = lambda t: jnp.tile(jnp.concatenate([jnp.swapaxes(t.astype(F32), 1, 2)] * 2, axis=1), (1, 1, lc))
    crt, cit = tile_c(c_re), tile_c(c_im)
    g3 = lambda blk: pl.BlockSpec((1,) + blk, lambda g: (g, 0, 0))
    return pl.pallas_call(
        _s5_prep_body,
        out_shape=(jax.ShapeDtypeStruct((ng, w, w), BF16),
                   jax.ShapeDtypeStruct((ng, w, 2 * nst), BF16),
                   jax.ShapeDtypeStruct((ng, 2 * nst, w), BF16),
                   jax.ShapeDtypeStruct((ng, 16, 2 * nst), F32),
                   jax.ShapeDtypeStruct((ng, 16, 2 * nst), F32)),
        grid=(ng,),
        in_specs=[g3((1, 1)), g3((1, 2 * nst)), g3((1, 2 * nst)), g3((2 * nst, 1)), g3((2 * nst, 1)),
                  g3((gc, 2 * nst)), g3((gc, 2 * nst)), g3((2 * nst, w)), g3((2 * nst, w))],
        out_specs=(g3((w, w)), g3((w, 2 * nst)), g3((2 * nst, w)), g3((16, 2 * nst)), g3((16, 2 * nst))),
        compiler_params=_params(("parallel",), 32),
        name="s5_prep",
    )(log_step.astype(F32).reshape(ng, 1, 1), ar.reshape(ng, 1, 2 * nst), ai.reshape(ng, 1, 2 * nst),
      ar.reshape(ng, 2 * nst, 1), ai.reshape(ng, 2 * nst, 1), brt, bit, crt, cit)


def _s5_scan_body(u_ref, m_ref, p_ref, q_ref, tr_ref, ti_ref, y_ref, *, n_steps):
    nst = S5_STATE
    u = u_ref[0, 0]
    x = _dot(u, p_ref[0])
    row = lax.broadcasted_iota(jnp.int32, x.shape, 0)
    tr = tr_ref[0]
    ti = ti_ref[0]
    for k in range(n_steps):
        s = 1 << k
        sh = jnp.where(row >= s, pltpu.roll(x, s, axis=0), 0.0)
        x = x + tr[k:k + 1, :] * sh + ti[k:k + 1, :] * pltpu.roll(sh, nst, axis=1)
    x_in = jnp.where(row >= 1, pltpu.roll(x, 1, axis=0), 0.0)
    y_ref[0, 0] = _dot(u, m_ref[0]) + _dot(x_in.astype(BF16), q_ref[0])


def _s5_scan(ug, m, p, q, tr, ti):
    ng, bsz, nck, w = ug.shape
    nst2 = p.shape[-1]
    n_steps = int(math.log2(nck))
    assert (1 << n_steps) == nck
    g3 = lambda blk: pl.BlockSpec((1,) + blk, lambda g, b: (g, 0, 0))
    return pl.pallas_call(
        functools.partial(_s5_scan_body, n_steps=n_steps),
        out_shape=jax.ShapeDtypeStruct((ng, bsz, nck, w), F32),
        grid=(ng, bsz),
        in_specs=[pl.BlockSpec((1, 1, nck, w), lambda g, b: (g, b, 0, 0)),
                  g3((w, w)), g3((w, nst2)), g3((nst2, w)), g3((16, nst2)), g3((16, nst2))],
        out_specs=pl.BlockSpec((1, 1, nck, w), lambda g, b: (g, b, 0, 0)),
        compiler_params=_params(("parallel", "parallel"), 32),
        name="s5_scan",
    )(ug, m, p, q, tr, ti)


def _glu_body(y_ref, u_ref, d_ref, w_ref, b_ref, o_ref):
    yy = y_ref[...] + d_ref[...] * u_ref[...]
    z = 0.5 * yy * (1.0 + jnp.tanh(math.sqrt(2.0 / math.pi) * (yy + 0.044715 * (yy * yy * yy))))
    gate = _sigmoid(_dot(z.astype(BF16), w_ref[...]) + b_ref[...])
    o_ref[...] = (z * gate).astype(o_ref.dtype)


def _glu(y, z2, u_blk, d_skip, w_glu, b_glu, tm=512):
    n, wd = y.shape
    return pl.pallas_call(
        _glu_body,
        out_shape=jax.ShapeDtypeStruct((n, wd), BF16),
        grid=(n // tm,),
        in_specs=[pl.BlockSpec((tm, wd), lambda i: (i, 0)),
                  pl.BlockSpec((tm, wd), lambda i: (i, u_blk)),
                  pl.BlockSpec((1, wd), lambda i: (0, 0)),
                  pl.BlockSpec((wd, wd), lambda i: (0, 0)),
                  pl.BlockSpec((1, wd), lambda i: (0, 0))],
        out_specs=pl.BlockSpec((tm, wd), lambda i: (i, 0)),
        compiler_params=_params(("parallel",), 32),
        name="s5_glu",
    )(y, z2, d_skip.astype(F32).reshape(1, wd), w_glu.astype(BF16), b_glu.astype(F32).reshape(1, wd))


def _xattn_body(hb_ref, h_ref, wq_ref, k_ref, v_ref, wo_ref, g_ref, b_ref, o_ref, ob_ref, *, alpha):
    d = XA_HEAD_DIM
    q = _dot(hb_ref[...], wq_ref[...]).astype(BF16)
    outs = []
    for hd in range(XA_HEADS):
        sl = slice(hd * d, (hd + 1) * d)
        logits = _nt(q[:, sl], k_ref[0, :, sl]) * (d ** -0.5)
        m = jnp.max(logits, axis=-1, keepdims=True)
        p = jnp.exp(logits - m)
        p = p / jnp.sum(p, axis=-1, keepdims=True)
        outs.append(_dot(p.astype(BF16), v_ref[0, :, sl]))
    o = jnp.concatenate(outs, axis=-1).astype(BF16)
    v = alpha * h_ref[...] + _dot(o, wo_ref[...])
    mu = jnp.mean(v, axis=-1, keepdims=True)
    c = v - mu
    var = jnp.mean(c * c, axis=-1, keepdims=True)
    out = c * lax.rsqrt(var + LN_EPS) * g_ref[...] + b_ref[...]
    o_ref[...] = out
    ob_ref[...] = out.astype(BF16)


def _xattn(h, hb, kmem, vmem, wq, wo, g, b, alpha, seq, tm=256):
    n, d = h.shape
    xw = wq.shape[1]
    ml = kmem.shape[1]
    per_b = seq // tm
    return pl.pallas_call(
        functools.partial(_xattn_body, alpha=alpha),
        out_shape=(jax.ShapeDtypeStruct((n, d), F32), jax.ShapeDtypeStruct((n, d), BF16)),
        grid=(n // tm,),
        in_specs=[pl.BlockSpec((tm, d), lambda i: (i, 0)),
                  pl.BlockSpec((tm, d), lambda i: (i, 0)),
                  pl.BlockSpec((d, xw), lambda i: (0, 0)),
                  pl.BlockSpec((1, ml, xw), lambda i: (i // per_b, 0, 0)),
                  pl.BlockSpec((1, ml, xw), lambda i: (i // per_b, 0, 0)),
                  pl.BlockSpec((xw, d), lambda i: (0, 0)),
                  pl.BlockSpec((1, d), lambda i: (0, 0)),
                  pl.BlockSpec((1, d), lambda i: (0, 0))],
        out_specs=(pl.BlockSpec((tm, d), lambda i: (i, 0)),
                   pl.BlockSpec((tm, d), lambda i: (i, 0))),
        compiler_params=_params(("parallel",), 56),
        name="xattn",
    )(hb, h, wq, kmem, vmem, wo, g.reshape(1, d), b.reshape(1, d))


def _first_max(v, idx, n):
    m = jnp.max(v, axis=0, keepdims=True)
    first = jnp.min(jnp.where(v == m, idx, n), axis=0, keepdims=True)
    return m, first


def _router_body(x_ref, r_ref, rb_ref, o_ref, *, tm):
    ne, ng = N_EXPERTS, N_EXPERT_GROUPS
    per = ne // ng
    logits = lax.dot_general(r_ref[...], x_ref[...], (((1,), (1,)), ((), ())),
                             preferred_element_type=F32, precision=lax.Precision.HIGHEST)
    scores = _sigmoid(logits)
    biased = scores + rb_ref[...]
    gsc = []
    eidx = lax.broadcasted_iota(jnp.int32, (per, tm), 0)
    for g in range(ng):
        v = biased[g * per:(g + 1) * per, :]
        m1, i1 = _first_max(v, eidx, per)
        m2 = jnp.max(jnp.where(eidx == i1, -jnp.inf, v), axis=0, keepdims=True)
        gsc.append(m1 + m2)
    gs = jnp.concatenate(gsc, axis=0)
    gidx = lax.broadcasted_iota(jnp.int32, (ng, tm), 0)
    gsel = jnp.zeros((ng, tm), jnp.bool_)
    for _ in range(TOPK_GROUPS):
        _, first = _first_max(gs, gidx, ng)
        hit = gidx == first
        gsel = jnp.logical_or(gsel, hit)
        gs = jnp.where(hit, -jnp.inf, gs)
    emask = jnp.concatenate([jnp.broadcast_to(gsel[g:g + 1, :], (per, tm)) for g in range(ng)], axis=0)
    cand = jnp.where(emask, biased, -jnp.inf)
    aidx = lax.broadcasted_iota(jnp.int32, (ne, tm), 0)
    picked, ids = [], []
    for _ in range(TOP_K):
        _, first = _first_max(cand, aidx, ne)
        hit = aidx == first
        picked.append(jnp.sum(jnp.where(hit, scores, 0.0), axis=0, keepdims=True))
        ids.append(first.astype(F32))
        cand = jnp.where(hit, -jnp.inf, cand)
    total = picked[0]
    for pk in picked[1:]:
        total = total + pk
    gates = [pk / total * ROUTED_SCALE for pk in picked]
    pad = jnp.zeros((LANES - 2 * TOP_K, tm), F32)
    o_ref[...] = jnp.concatenate(gates + ids + [pad], axis=0).T


def _router(h, router, router_bias, tm=512):
    n, d = h.shape
    ne = router.shape[0]
    return pl.pallas_call(
        functools.partial(_router_body, tm=tm),
        out_shape=jax.ShapeDtypeStruct((n, LANES), F32),
        grid=(n // tm,),
        in_specs=[pl.BlockSpec((tm, d), lambda i: (i, 0)),
                  pl.BlockSpec((ne, d), lambda i: (0, 0)),
                  pl.BlockSpec((ne, 1), lambda i: (0, 0))],
        out_specs=pl.BlockSpec((tm, LANES), lambda i: (i, 0)),
        compiler_params=_params(("parallel",), 48),
        name="moe_router",
    )(h, router.astype(F32), router_bias.astype(F32).reshape(ne, 1))


def _ffn_tile(x, wg, wu, wd, o_ref, accumulate, n_col=4):
    gte = _dot(x, wg)
    hid = ((gte * _sigmoid(gte)) * _dot(x, wu)).astype(BF16)
    d = o_ref.shape[1]
    cw = d // n_col
    for j in range(n_col):
        y = _dot(hid, wd[:, j * cw:(j + 1) * cw])
        if accumulate:
            o_ref[:, j * cw:(j + 1) * cw] += y
        else:
            o_ref[:, j * cw:(j + 1) * cw] = y


def _shared_body(x_ref, wg_ref, wu_ref, wd_ref, o_ref):
    @pl.when(pl.program_id(1) == 0)
    def _():
        o_ref[...] = jnp.zeros_like(o_ref)

    _ffn_tile(x_ref[...], wg_ref[...], wu_ref[...], wd_ref, o_ref, accumulate=True)


def _shared_ffn(hb, wg, wu, wd, tm=512, th=EXPERT_HIDDEN):
    n, d = hb.shape
    return pl.pallas_call(
        _shared_body,
        out_shape=jax.ShapeDtypeStruct((n, d), F32),
        grid=(n // tm, wg.shape[1] // th),
        in_specs=[pl.BlockSpec((tm, d), lambda i, e: (i, 0)),
                  pl.BlockSpec((d, th), lambda i, e: (0, e)),
                  pl.BlockSpec((d, th), lambda i, e: (0, e)),
                  pl.BlockSpec((th, d), lambda i, e: (e, 0))],
        out_specs=pl.BlockSpec((tm, d), lambda i, e: (i, 0)),
        compiler_params=_params(("parallel", "arbitrary"), 56),
        name="moe_shared",
    )(hb, wg, wu, wd)


MOE_ROW_TILE = 512
DMA_UNROLL = 8


def _start_row_gather(ids_ref, src_hbm, dst, sem, n_rows):
    def body(j, c):
        for u in range(DMA_UNROLL):
            r = j * DMA_UNROLL + u
            pltpu.make_async_copy(src_hbm.at[pl.ds(ids_ref[0, 0, r], 1)], dst.at[pl.ds(r, 1)], sem).start(priority=u % 2)
        return c

    lax.fori_loop(0, n_rows // DMA_UNROLL, body, 0)


def _wait_row_gather(src_hbm, dst, sem, n_rows):
    pltpu.make_async_copy(src_hbm.at[pl.ds(0, n_rows)], dst, sem).wait()


def _moe_sparse_body(te_ref, nv_ref, tok_ref, tokn_ref, h_hbm, wg_ref, wu_ref, wd_ref, y_ref, xbuf, sem, *, tr):
    i = pl.program_id(0)
    nv = nv_ref[0]
    slot = i % 2

    @pl.when(i == 0)
    def _():
        _start_row_gather(tok_ref, h_hbm, xbuf.at[0], sem.at[0], tr)

    @pl.when(i + 1 < nv)
    def _():
        _start_row_gather(tokn_ref, h_hbm, xbuf.at[1 - slot], sem.at[1 - slot], tr)

    @pl.when(i < nv)
    def _():
        _wait_row_gather(h_hbm, xbuf.at[slot], sem.at[slot], tr)
        _ffn_tile(xbuf[slot].astype(BF16), wg_ref[0], wu_ref[0], wd_ref.at[0], y_ref, accumulate=False)

    @pl.when(i >= nv)
    def _():
        y_ref[...] = jnp.zeros_like(y_ref)


def _moe_sparse(h, wg, wu, wd, tok3, tile_expert, n_valid):
    n, d = h.shape
    n_tiles, _, tr = tok3.shape
    last = lambda i, nv: jnp.minimum(i, nv[0] - 1)
    grid_spec = pltpu.PrefetchScalarGridSpec(
        num_scalar_prefetch=2,
        grid=(n_tiles,),
        in_specs=[pl.BlockSpec((1, 1, tr), lambda i, te, nv: (i, 0, 0), memory_space=pltpu.SMEM),
                  pl.BlockSpec((1, 1, tr), lambda i, te, nv: (jnp.minimum(i + 1, n_tiles - 1), 0, 0),
                               memory_space=pltpu.SMEM),
                  pl.BlockSpec(memory_space=pl.ANY),
                  pl.BlockSpec((1, d, wg.shape[2]), lambda i, te, nv: (te[last(i, nv)], 0, 0)),
                  pl.BlockSpec((1, d, wu.shape[2]), lambda i, te, nv: (te[last(i, nv)], 0, 0)),
                  pl.BlockSpec((1, wd.shape[1], d), lambda i, te, nv: (te[last(i, nv)], 0, 0))],
        out_specs=pl.BlockSpec((tr, d), lambda i, te, nv: (i, 0)),
        scratch_shapes=[pltpu.VMEM((2, tr, d), F32), pltpu.SemaphoreType.DMA((2,))])
    return pl.pallas_call(
        functools.partial(_moe_sparse_body, tr=tr),
        out_shape=jax.ShapeDtypeStruct((n_tiles * tr, d), F32),
        grid_spec=grid_spec,
        compiler_params=_params(("arbitrary",), 60),
        name="moe_sparse",
    )(tile_expert, n_valid, tok3, tok3, h, wg, wu, wd)


def _moe_combine_body(sl_ref, sln_ref, y_hbm, gate_ref, sh_ref, h_ref, g_ref, b_ref, o_ref, ob_ref, buf, sem,
                      *, tm, alpha, n_steps):
    i = pl.program_id(0)
    slot = i % 2
    rows = TOP_K * tm

    @pl.when(i == 0)
    def _():
        _start_row_gather(sl_ref, y_hbm, buf.at[0], sem.at[0], rows)

    @pl.when(i + 1 < n_steps)
    def _():
        _start_row_gather(sln_ref, y_hbm, buf.at[1 - slot], sem.at[1 - slot], rows)

    _wait_row_gather(y_hbm, buf.at[slot], sem.at[slot], rows)
    acc = sh_ref[...]
    for k in range(TOP_K):
        acc = acc + gate_ref[:, k:k + 1] * buf[slot, k * tm:(k + 1) * tm, :]
    v = alpha * h_ref[...] + acc
    mu = jnp.mean(v, axis=-1, keepdims=True)
    c = v - mu
    var = jnp.mean(c * c, axis=-1, keepdims=True)
    out = c * lax.rsqrt(var + LN_EPS) * g_ref[...] + b_ref[...]
    o_ref[...] = out
    ob_ref[...] = out.astype(BF16)


def _moe_combine(y_sorted, slot3, rout, shared, h, g, b, alpha):
    n, d = h.shape
    n_steps, _, rows = slot3.shape
    tm = rows // TOP_K
    row = lambda i: (i, 0)
    return pl.pallas_call(
        functools.partial(_moe_combine_body, tm=tm, alpha=alpha, n_steps=n_steps),
        out_shape=(jax.ShapeDtypeStruct((n, d), F32), jax.ShapeDtypeStruct((n, d), BF16)),
        grid=(n_steps,),
        in_specs=[pl.BlockSpec((1, 1, rows), lambda i: (i, 0, 0), memory_space=pltpu.SMEM),
                  pl.BlockSpec((1, 1, rows), lambda i: (jnp.minimum(i + 1, n_steps - 1), 0, 0),
                               memory_space=pltpu.SMEM),
                  pl.BlockSpec(memory_space=pl.ANY),
                  pl.BlockSpec((tm, LANES), row),
                  pl.BlockSpec((tm, d), row),
                  pl.BlockSpec((tm, d), row),
                  pl.BlockSpec((1, d), lambda i: (0, 0)),
                  pl.BlockSpec((1, d), lambda i: (0, 0))],
        out_specs=(pl.BlockSpec((tm, d), row), pl.BlockSpec((tm, d), row)),
        scratch_shapes=[pltpu.VMEM((2, rows, d), F32), pltpu.SemaphoreType.DMA((2,))],
        compiler_params=_params(("arbitrary",), 48),
        name="moe_combine",
    )(slot3, slot3, y_sorted, rout, shared, h, g.reshape(1, d), b.reshape(1, d))


def _dispatch_plan(rout, n_experts, tr, tm):
    n = rout.shape[0]
    ids = rout[:, TOP_K:2 * TOP_K].astype(jnp.int32)
    onehot = jnp.sum((ids[:, :, None] == jnp.arange(n_experts)[None, None, :]).astype(jnp.int32), axis=1)
    csum = jnp.cumsum(onehot, axis=0)
    counts = csum[-1]
    padded = (counts + tr - 1) // tr * tr
    seg_end = jnp.cumsum(padded)
    seg_start = seg_end - padded
    rank = jnp.take_along_axis(csum - onehot, ids, axis=1)
    slot = seg_start[ids] + rank
    n_tiles = n * TOP_K // tr + n_experts
    tok = jnp.zeros((n_tiles * tr,), jnp.int32).at[slot.reshape(-1)].set(
        jnp.repeat(jnp.arange(n, dtype=jnp.int32), TOP_K), unique_indices=True)
    tile_start = jnp.arange(n_tiles, dtype=jnp.int32) * tr
    tile_expert = jnp.minimum(jnp.sum((seg_end[None, :] <= tile_start[:, None]).astype(jnp.int32), axis=1),
                              n_experts - 1)
    n_valid = (seg_end[-1:] // tr).astype(jnp.int32)
    slot3 = slot.reshape(n // tm, tm, TOP_K).transpose(0, 2, 1).reshape(n // tm, 1, TOP_K * tm)
    return tok.reshape(n_tiles, 1, tr), tile_expert, n_valid, slot3.astype(jnp.int32)


def kernel(x, mem, ev_w_in, ev_pool_w, ev_pool_scale, ev_sinks, ev_w_out, rel_bias, od_w_in, hg_lb_logits, od_hg_norm, od_a_re, od_a_im, od_log_step, od_b_re, od_b_im, od_c_re, od_c_im, od_d_skip, od_w_glu, od_b_glu, od_w_out, xa_wq, xa_wk, xa_wv, xa_wo, moe_router, moe_bias, moe_w_gate, moe_w_up, moe_w_down, sh_w_gate, sh_w_up, sh_w_down, ln_g, ln_b):
    bsz, seq, d = x.shape
    n = bsz * seq
    depth = ln_g.shape[0]
    alpha = (2 * depth) ** 0.25
    memb = mem.astype(BF16).reshape(bsz * mem.shape[1], d)

    h = x.astype(F32).reshape(n, d)
    hb = h.astype(BF16)
    bias_tab = _bias_table(rel_bias)

    for l in range(depth):
        j = l // 2
        if l % 2 == 0:
            pool_w = ev_pool_w[j]
            pool_width = pool_w.shape[0] * pool_w.shape[1]
            n_q = ev_sinks.shape[1]
            n_kv = n_q // SWA_GQ
            z = _mm(hb, ev_w_in[j].astype(BF16), F32)
            z3 = z.reshape(bsz, seq, z.shape[1])
            y_a = _pool(z3, pool_w.astype(BF16), ev_pool_scale[j].astype(F32))
            y_b = _swa(z3, ev_sinks[j], bias_tab, pool_width, n_q, n_kv)
            mix = _mm2(y_a.reshape(n, -1), y_b.reshape(n, -1), ev_w_out[j].astype(BF16), F32)
        else:
            n_heads = hg_lb_logits.shape[1] // HG_DIM
            hgw = n_heads * HG_DIM
            ng, nst = od_a_re.shape[1], od_a_re.shape[2]
            s5w = ng * S5_GROUP
            z = _mm(hb, od_w_in[j].astype(BF16), F32)
            z3 = z.reshape(bsz, seq, z.shape[1])
            y_c = _hgrn(z3, hg_lb_logits, od_hg_norm[j], l, n_heads)
            m_mat, p_mat, q_mat, t_re, t_im = _s5_prep(od_a_re[j], od_a_im[j], od_log_step[j], od_b_re[j],
                                                        od_b_im[j], od_c_re[j], od_c_im[j])
            nck = seq // S5_CHUNK
            u = z3[:, :, 4 * hgw:].astype(BF16).reshape(bsz, nck, S5_CHUNK, ng, S5_GROUP)
            ug = u.transpose(3, 0, 1, 2, 4).reshape(ng, bsz, nck, S5_CHUNK * S5_GROUP)
            yg = _s5_scan(ug, m_mat, p_mat, q_mat, t_re, t_im)
            y_s = yg.reshape(ng, bsz, nck, S5_CHUNK, S5_GROUP).transpose(1, 2, 3, 0, 4).reshape(n, s5w)
            y_d = _glu(y_s, z, (4 * hgw) // s5w, od_d_skip[j], od_w_glu[j], od_b_glu[j])
            mix = _mm2(y_c.reshape(n, hgw), y_d, od_w_out[j].astype(BF16), F32)
        h, hb = _add_ln(h, mix, ln_g[l, 0], ln_b[l, 0], alpha)

        xw = xa_wq.shape[2]
        kmem = _mm(memb, xa_wk[l].astype(BF16), BF16).reshape(bsz, -1, xw)
        vmem = _mm(memb, xa_wv[l].astype(BF16), BF16).reshape(bsz, -1, xw)
        h, hb = _xattn(h, hb, kmem, vmem, xa_wq[l].astype(BF16), xa_wo[l].astype(BF16),
                       ln_g[l, 1], ln_b[l, 1], alpha, seq)

        n_exp = moe_router.shape[1]
        n_shared = sh_w_gate.shape[2] // EXPERT_HIDDEN
        rout = _router(h, moe_router[l], moe_bias[l])
        tok3, tile_expert, n_valid, slot3 = _dispatch_plan(rout, n_exp, MOE_ROW_TILE, 64)
        shared = _shared_ffn(hb, sh_w_gate[l].astype(BF16), sh_w_up[l].astype(BF16), sh_w_down[l].astype(BF16))
        y_sorted = _moe_sparse(h, moe_w_gate[l].astype(BF16), moe_w_up[l].astype(BF16),
                               moe_w_down[l].astype(BF16), tok3, tile_expert, n_valid)
        h, hb = _moe_combine(y_sorted, slot3, rout, shared, h, ln_g[l, 2], ln_b[l, 2], alpha)

    return h.reshape(bsz, seq, d)
```

```python
import functools
import math

import jax
import jax.numpy as jnp
from jax import lax
from jax.experimental import pallas as pl
from jax.experimental.pallas import tpu as pltpu

F32 = jnp.float32
BF16 = jnp.bfloat16

V7X_VMEM_BYTES = 64 * 1024 * 1024
LANES = 128
SUBLANES = 8

POOL_WINDOWS = (2, 4, 8, 16)
SWA_HEAD_DIM = 64
SWA_GQ = 8
SWA_BLOCK = 128
REL_BUCKETS = 32
REL_MAX_DIST = 128
HG_DIM = 128
HG_CHUNK = 64
RMS_EPS = 1e-6
S5_GROUP = 16
S5_STATE = 64
S5_CHUNK = 16
XA_HEADS = 4
XA_HEAD_DIM = 128
N_EXPERTS = 64
TOP_K = 8
N_EXPERT_GROUPS = 8
TOPK_GROUPS = 4
EXPERT_HIDDEN = 256
ROUTED_SCALE = 2.5
LN_EPS = 1e-5


def _params(sem, vmem_mb):
    return pltpu.CompilerParams(dimension_semantics=sem, vmem_limit_bytes=vmem_mb * 1024 * 1024)


def _nt(a, b):
    return lax.dot_general(a, b, (((1,), (1,)), ((), ())), preferred_element_type=F32)


def _tn(a, b):
    return lax.dot_general(a, b, (((0,), (0,)), ((), ())), preferred_element_type=F32)


def _dot(a, b):
    return jnp.dot(a, b, preferred_element_type=F32)


def _sigmoid(x):
    return 0.5 * (jnp.tanh(0.5 * x) + 1.0)


def _mm_body(x_ref, w_ref, o_ref):
    o_ref[...] = _dot(x_ref[...], w_ref[...]).astype(o_ref.dtype)


def _mm(x, w, out_dtype, tm=1024, tn=512):
    m, k = x.shape
    n = w.shape[1]
    tm = min(tm, m)
    tn = min(tn, n)
    return pl.pallas_call(
        _mm_body,
        out_shape=jax.ShapeDtypeStruct((m, n), out_dtype),
        grid=(m // tm, n // tn),
        in_specs=[pl.BlockSpec((tm, k), lambda i, j: (i, 0)),
                  pl.BlockSpec((k, tn), lambda i, j: (0, j))],
        out_specs=pl.BlockSpec((tm, tn), lambda i, j: (i, j)),
        compiler_params=_params(("parallel", "parallel"), 48),
        name="mm",
    )(x, w)


def _mm2_body(a_ref, b_ref, w_ref, o_ref):
    ka = a_ref.shape[1]
    o_ref[...] = (_dot(a_ref[...], w_ref[:ka, :]) + _dot(b_ref[...], w_ref[ka:, :])).astype(o_ref.dtype)


def _mm2(a, b, w, out_dtype, tm=1024, tn=512):
    m, ka = a.shape
    kb = b.shape[1]
    n = w.shape[1]
    return pl.pallas_call(
        _mm2_body,
        out_shape=jax.ShapeDtypeStruct((m, n), out_dtype),
        grid=(m // tm, n // tn),
        in_specs=[pl.BlockSpec((tm, ka), lambda i, j: (i, 0)),
                  pl.BlockSpec((tm, kb), lambda i, j: (i, 0)),
                  pl.BlockSpec((ka + kb, tn), lambda i, j: (0, j))],
        out_specs=pl.BlockSpec((tm, tn), lambda i, j: (i, j)),
        compiler_params=_params(("parallel", "parallel"), 48),
        name="mm2",
    )(a, b, w)


def _add_ln_body(h_ref, y_ref, g_ref, b_ref, o_ref, ob_ref, *, alpha):
    v = alpha * h_ref[...] + y_ref[...].astype(F32)
    mu = jnp.mean(v, axis=-1, keepdims=True)
    c = v - mu
    var = jnp.mean(c * c, axis=-1, keepdims=True)
    out = c * lax.rsqrt(var + LN_EPS) * g_ref[...] + b_ref[...]
    o_ref[...] = out
    ob_ref[...] = out.astype(BF16)


def _add_ln(h, y, g, b, alpha, tm=256):
    n, d = h.shape
    return pl.pallas_call(
        functools.partial(_add_ln_body, alpha=alpha),
        out_shape=(jax.ShapeDtypeStruct((n, d), F32), jax.ShapeDtypeStruct((n, d), BF16)),
        grid=(n // tm,),
        in_specs=[pl.BlockSpec((tm, d), lambda i: (i, 0)),
                  pl.BlockSpec((tm, d), lambda i: (i, 0)),
                  pl.BlockSpec((1, d), lambda i: (0, 0)),
                  pl.BlockSpec((1, d), lambda i: (0, 0))],
        out_specs=(pl.BlockSpec((tm, d), lambda i: (i, 0)),
                   pl.BlockSpec((tm, d), lambda i: (i, 0))),
        compiler_params=_params(("parallel",), 48),
        name="add_ln",
    )(h, y, g.reshape(1, d), b.reshape(1, d))


def _pool_body(u_ref, prev_ref, w_ref, sc_ref, o_ref, *, ts, gw, halo):
    i = pl.program_id(1)
    row = lax.broadcasted_iota(jnp.int32, (ts, gw), 0)
    pos1 = (row + i * ts + 1).astype(F32)
    for g, win in enumerate(POOL_WINDOWS):
        u = u_ref[0, :, g * gw:(g + 1) * gw]
        prev = prev_ref[0, :, g * gw:(g + 1) * gw]
        prev = jnp.where(i == 0, 0.0, prev)
        s = jnp.concatenate([prev, u], axis=0)
        step = 1
        while step < win:
            s = s + pltpu.roll(s, step, axis=0)
            step *= 2
        wsum = s[halo:, :]
        pooled = wsum / jnp.minimum(pos1, float(win)) - u
        y = _dot(pooled.astype(BF16), w_ref[g])
        o_ref[0, :, g * gw:(g + 1) * gw] = (y * sc_ref[:, g * gw:(g + 1) * gw]).astype(o_ref.dtype)


def _pool(z3, pool_w, pool_scale, ts=512):
    bsz, s, _ = z3.shape
    ng, gw, _ = pool_w.shape
    width = ng * gw
    halo = max(POOL_WINDOWS)
    return pl.pallas_call(
        functools.partial(_pool_body, ts=ts, gw=gw, halo=halo),
        out_shape=jax.ShapeDtypeStruct((bsz, s, width), BF16),
        grid=(bsz, s // ts),
        in_specs=[pl.BlockSpec((1, ts, width), lambda b, i: (b, i, 0)),
                  pl.BlockSpec((1, halo, width), lambda b, i: (b, jnp.maximum(i * (ts // halo) - 1, 0), 0)),
                  pl.BlockSpec((ng, gw, gw), lambda b, i: (0, 0, 0)),
                  pl.BlockSpec((1, width), lambda b, i: (0, 0))],
        out_specs=pl.BlockSpec((1, ts, width), lambda b, i: (b, i, 0)),
        compiler_params=_params(("parallel", "parallel"), 48),
        name="pool",
    )(z3, z3, pool_w, pool_scale.reshape(1, width))


def _t5_bucket_table():
    qi = jnp.arange(SWA_BLOCK)[:, None]
    kj = jnp.arange(2 * SWA_BLOCK)[None, :]
    dist = qi + SWA_BLOCK - kj
    band = (dist >= 0) & (dist < SWA_BLOCK)
    d = jnp.maximum(dist, 0)
    exact = REL_BUCKETS // 2
    far = exact + (jnp.log(jnp.maximum(d, exact).astype(F32) / exact)
                   / math.log(REL_MAX_DIST / exact) * (REL_BUCKETS - exact)).astype(jnp.int32)
    bucket = jnp.where(d < exact, d, jnp.minimum(far, REL_BUCKETS - 1))
    return jnp.where(band, bucket, -1).astype(jnp.int32)


def _bias_body(rb_ref, bucket_ref, o_ref):
    h = pl.program_id(0)
    bucket = bucket_ref[...]
    acc = jnp.full(bucket.shape, -jnp.inf, F32)
    for b in range(REL_BUCKETS):
        acc = jnp.where(bucket == b, rb_ref[b, h], acc)
    o_ref[0] = acc


def _bias_table(rel_bias):
    nb, nh = rel_bias.shape
    bucket = _t5_bucket_table()
    return pl.pallas_call(
        _bias_body,
        out_shape=jax.ShapeDtypeStruct((nh, SWA_BLOCK, 2 * SWA_BLOCK), F32),
        grid=(nh,),
        in_specs=[pl.BlockSpec(memory_space=pltpu.SMEM),
                  pl.BlockSpec((SWA_BLOCK, 2 * SWA_BLOCK), lambda h: (0, 0))],
        out_specs=pl.BlockSpec((1, SWA_BLOCK, 2 * SWA_BLOCK), lambda h: (h, 0, 0)),
        compiler_params=_params(("parallel",), 32),
        name="swa_bias",
    )(rel_bias.astype(F32), bucket)


def _swa_body(sink_ref, q_ref, kp_ref, kc_ref, vp_ref, vc_ref, bias_ref, o_ref, *, n_kv):
    n = pl.program_id(1)
    dh = SWA_HEAD_DIM
    col = lax.broadcasted_iota(jnp.int32, (SWA_BLOCK, 2 * SWA_BLOCK), 1)
    no_prev = jnp.logical_and(n == 0, col < SWA_BLOCK)
    scale = dh ** -0.5
    for kv in range(n_kv):
        sl = slice(kv * dh, (kv + 1) * dh)
        k2 = jnp.concatenate([kp_ref[0, :, sl], kc_ref[0, :, sl]], axis=0).astype(BF16)
        v2 = jnp.concatenate([vp_ref[0, :, sl], vc_ref[0, :, sl]], axis=0).astype(BF16)
        for g in range(SWA_GQ):
            h = kv * SWA_GQ + g
            qh = q_ref[0, :, h * dh:(h + 1) * dh].astype(BF16)
            logits = _nt(qh, k2) * scale + bias_ref[h]
            logits = jnp.where(no_prev, -jnp.inf, logits)
            sink = sink_ref[h]
            m = jnp.maximum(jnp.max(logits, axis=-1, keepdims=True), sink)
            p = jnp.exp(logits - m)
            p = p / (jnp.sum(p, axis=-1, keepdims=True) + jnp.exp(sink - m))
            o_ref[0, :, h * dh:(h + 1) * dh] = _dot(p.astype(BF16), v2).astype(o_ref.dtype)


def _swa(z3, sinks, bias_tab, q_off, n_q, n_kv):
    bsz, s, _ = z3.shape
    dh = SWA_HEAD_DIM
    qw, kvw = n_q * dh, n_kv * dh
    qb = q_off // qw
    kb = (q_off + qw) // kvw
    vb = kb + 1
    blk = SWA_BLOCK
    prev = lambda n: jnp.maximum(n - 1, 0)
    return pl.pallas_call(
        functools.partial(_swa_body, n_kv=n_kv),
        out_shape=jax.ShapeDtypeStruct((bsz, s, qw), BF16),
        grid=(bsz, s // blk),
        in_specs=[pl.BlockSpec(memory_space=pltpu.SMEM),
                  pl.BlockSpec((1, blk, qw), lambda b, n: (b, n, qb)),
                  pl.BlockSpec((1, blk, kvw), lambda b, n: (b, prev(n), kb)),
                  pl.BlockSpec((1, blk, kvw), lambda b, n: (b, n, kb)),
                  pl.BlockSpec((1, blk, kvw), lambda b, n: (b, prev(n), vb)),
                  pl.BlockSpec((1, blk, kvw), lambda b, n: (b, n, vb)),
                  pl.BlockSpec((n_q, blk, 2 * blk), lambda b, n: (0, 0, 0))],
        out_specs=pl.BlockSpec((1, blk, qw), lambda b, n: (b, n, 0)),
        compiler_params=_params(("parallel", "parallel"), 48),
        name="swa",
    )(sinks.astype(F32), z3, z3, z3, z3, z3, bias_tab)


def _hgrn_body(q_ref, f_ref, i_ref, g_ref, lbl_ref, nw_ref, o_ref, st_ref, *, n_chunk, layer):
    @pl.when(pl.program_id(2) == 0)
    def _():
        st_ref[...] = jnp.zeros_like(st_ref)

    lg = lbl_ref[:, 0, 0, :]
    e = jnp.exp(lg - jnp.max(lg, axis=0, keepdims=True))
    p = e / jnp.sum(e, axis=0, keepdims=True)
    lb = jnp.sum(p[:layer + 1], axis=0, keepdims=True) - p[0:1]
    nw = nw_ref[...]
    c = HG_CHUNK
    r = lax.broadcasted_iota(jnp.int32, (c, c), 0)
    cc = lax.broadcasted_iota(jnp.int32, (c, c), 1)
    causal = r >= cc
    tri = jnp.where(causal, 1.0, 0.0).astype(BF16)

    d = HG_DIM
    chunks = lambda t: [t[ci * c:(ci + 1) * c] for ci in range(n_chunk)]
    q = q_ref[0]
    iv = chunks(i_ref[0].astype(BF16))
    g = g_ref[0]
    qf = q * _sigmoid(q)
    forget = lb + (1.0 - lb) * _sigmoid(f_ref[0])
    log_f = jnp.log(forget)
    k_in = 1.0 - forget
    hi = log_f.astype(BF16)
    r1 = log_f - hi.astype(F32)
    mid = r1.astype(BF16)
    lo = (r1 - mid.astype(F32)).astype(BF16)
    cum = jnp.concatenate([_dot(tri, a) + _dot(tri, b) + _dot(tri, e3)
                           for a, b, e3 in zip(chunks(hi), chunks(mid), chunks(lo))], axis=0)
    cum3 = cum.reshape(n_chunk, c, d)
    ref = cum3[:, c // 2:c // 2 + 1, :]
    total = cum3[:, c - 1:c, :]
    qa = qf.reshape(n_chunk, c, d) * jnp.exp(cum3 - ref)
    kb = k_in.reshape(n_chunk, c, d) * jnp.exp(ref - cum3)
    qe = chunks((qa * jnp.exp(ref)).astype(BF16).reshape(n_chunk * c, d))
    ku = chunks((kb * jnp.exp(total - ref)).astype(BF16).reshape(n_chunk * c, d))
    qa = chunks(qa.astype(BF16).reshape(n_chunk * c, d))
    kb = chunks(kb.astype(BF16).reshape(n_chunk * c, d))
    decay = jnp.exp(total)
    st = st_ref[...]
    outs = []
    for ci in range(n_chunk):
        scores = jnp.where(causal, _nt(qa[ci], kb[ci]), 0.0)
        o_intra = _dot(scores.astype(BF16), iv[ci])
        o_inter = _nt(qe[ci], st.astype(BF16))
        st = st * decay[ci] + _tn(iv[ci], ku[ci])
        outs.append(o_intra + o_inter)
    st_ref[...] = st
    o = jnp.concatenate(outs, axis=0)
    o = o * lax.rsqrt(jnp.mean(o * o, axis=-1, keepdims=True) + RMS_EPS) * nw
    o_ref[0] = (o * (g * _sigmoid(g))).astype(o_ref.dtype)


def _hgrn(z3, lb_logits, norm_w, layer, n_heads, ts=512):
    bsz, s, _ = z3.shape
    depth = lb_logits.shape[0]
    d = HG_DIM
    spec = lambda off: pl.BlockSpec((1, ts, d), lambda b, h, t: (b, t, off + h))
    return pl.pallas_call(
        functools.partial(_hgrn_body, n_chunk=ts // HG_CHUNK, layer=layer),
        out_shape=jax.ShapeDtypeStruct((bsz, s, n_heads * d), BF16),
        grid=(bsz, n_heads, s // ts),
        in_specs=[spec(0), spec(n_heads), spec(2 * n_heads), spec(3 * n_heads),
                  pl.BlockSpec((depth, 1, 1, d), lambda b, h, t: (0, h, 0, 0)),
                  pl.BlockSpec((1, d), lambda b, h, t: (0, 0))],
        out_specs=pl.BlockSpec((1, ts, d), lambda b, h, t: (b, t, h)),
        scratch_shapes=[pltpu.VMEM((d, d), F32)],
        compiler_params=_params(("parallel", "parallel", "arbitrary"), 32),
        name="hgrn2",
    )(z3, z3, z3, z3, lb_logits.astype(F32).reshape(depth, n_heads, 1, d), norm_w.astype(F32).reshape(1, d))


def _cpow(pw, lam_re, lam_im):
    mag = jnp.exp(pw * lam_re)
    ang = pw * lam_im
    return mag * jnp.cos(ang), mag * jnp.sin(ang)


def _s5_prep_body(ls_ref, ar_ref, ai_ref, arc_ref, aic_ref, br_ref, bi_ref, cr_ref, ci_ref,
                  m_ref, p_ref, q_ref, tr_ref, ti_ref):
    nst = S5_STATE
    lc = S5_CHUNK
    gc = S5_GROUP
    w = lc * gc
    dt = jnp.exp(ls_ref[0])
    a_re, a_im = ar_ref[0], ai_ref[0]
    lam_re, lam_im = dt * a_re, dt * a_im
    ab_re, ab_im = _cpow(1.0, lam_re, lam_im)
    den = a_re * a_re + a_im * a_im
    co_re = ((ab_re - 1.0) * a_re + ab_im * a_im) / den
    co_im = (ab_im * a_re - (ab_re - 1.0) * a_im) / den
    b_re, b_im = br_ref[0], bi_ref[0]
    bb_re = co_re * b_re - co_im * b_im
    bb_im = co_re * b_im + co_im * b_re
    lane = lax.broadcasted_iota(jnp.int32, (w, 2 * nst), 1)
    rowi = lax.broadcasted_iota(jnp.int32, (w, 2 * nst), 0)

    pw = (lc - 1 - rowi // gc).astype(F32)
    ap_re, ap_im = _cpow(pw, lam_re, lam_im)
    tb_re = jnp.concatenate([bb_re] * lc, axis=0)
    tb_im = jnp.concatenate([bb_im] * lc, axis=0)
    p_ref[0] = jnp.where(lane < nst, ap_re * tb_re - ap_im * tb_im,
                         ap_re * tb_im + ap_im * tb_re).astype(p_ref.dtype)

    lam_re_c, lam_im_c = dt * arc_ref[0], dt * aic_ref[0]
    c_re, c_im = cr_ref[0], ci_ref[0]
    lane_q = lax.broadcasted_iota(jnp.int32, (2 * nst, w), 1)
    row_q = lax.broadcasted_iota(jnp.int32, (2 * nst, w), 0)
    tq = (lane_q // gc).astype(F32)

    def c_times_apow(pwq):
        pr, pi = _cpow(pwq, lam_re_c, lam_im_c)
        return jnp.where(row_q < nst, c_re * pr - c_im * pi, -(c_re * pi + c_im * pr))

    q_ref[0] = c_times_apow(tq + 1.0).astype(q_ref.dtype)
    wmat = c_times_apow(tq)

    bbt = jnp.where(lax.broadcasted_iota(jnp.int32, (gc, 2 * nst), 1) < nst, bb_re, bb_im)
    r = jnp.dot(bbt, wmat, preferred_element_type=F32, precision=lax.Precision.HIGHEST)
    lane_r = lax.broadcasted_iota(jnp.int32, (gc, w), 1)
    for t in range(lc):
        blk = r if t == 0 else jnp.where(lane_r >= t * gc, pltpu.roll(r, t * gc, axis=1), 0.0)
        m_ref[0, t * gc:(t + 1) * gc, :] = blk.astype(m_ref.dtype)

    kk = lax.broadcasted_iota(jnp.int32, (16, 2 * nst), 0)
    pws = (lc * jnp.left_shift(1, jnp.minimum(kk, 12))).astype(F32)
    sr, si = _cpow(pws, lam_re, lam_im)
    tr_ref[0] = sr
    ti_ref[0] = jnp.where(lax.broadcasted_iota(jnp.int32, (16, 2 * nst), 1) < nst, -si, si)


def _s5_prep(a_re, a_im, log_step, b_re, b_im, c_re, c_im):
    ng, nst = a_re.shape
    gc, lc = S5_GROUP, S5_CHUNK
    w = lc * gc
    dup = lambda t: jnp.concatenate([t, t], axis=-1)
    ar = dup(a_re.astype(F32))
    ai = dup(a_im.astype(F32))
    brt = dup(jnp.swapaxes(b_re.astype(F32), 1, 2))
    bit = dup(jnp.swapaxes(b_im.astype(F32), 1, 2))
    tile_c = lambda t: jnp.tile(jnp.concatenate([jnp.swapaxes(t.astype(F32), 1, 2)] * 2, axis=1), (1, 1, lc))
    crt, cit = tile_c(c_re), tile_c(c_im)
    g3 = lambda blk: pl.BlockSpec((1,) + blk, lambda g: (g, 0, 0))
    return pl.pallas_call(
        _s5_prep_body,
        out_shape=(jax.ShapeDtypeStruct((ng, w, w), BF16),
                   jax.ShapeDtypeStruct((ng, w, 2 * nst), BF16),
                   jax.ShapeDtypeStruct((ng, 2 * nst, w), BF16),
                   jax.ShapeDtypeStruct((ng, 16, 2 * nst), F32),
                   jax.ShapeDtypeStruct((ng, 16, 2 * nst), F32)),
        grid=(ng,),
        in_specs=[g3((1, 1)), g3((1, 2 * nst)), g3((1, 2 * nst)), g3((2 * nst, 1)), g3((2 * nst, 1)),
                  g3((gc, 2 * nst)), g3((gc, 2 * nst)), g3((2 * nst, w)), g3((2 * nst, w))],
        out_specs=(g3((w, w)), g3((w, 2 * nst)), g3((2 * nst, w)), g3((16, 2 * nst)), g3((16, 2 * nst))),
        compiler_params=_params(("parallel",), 32),
        name="s5_prep",
    )(log_step.astype(F32).reshape(ng, 1, 1), ar.reshape(ng, 1, 2 * nst), ai.reshape(ng, 1, 2 * nst),
      ar.reshape(ng, 2 * nst, 1), ai.reshape(ng, 2 * nst, 1), brt, bit, crt, cit)


def _s5_scan_body(u_ref, m_ref, p_ref, q_ref, tr_ref, ti_ref, y_ref, *, n_steps):
    nst = S5_STATE
    u = u_ref[0, 0]
    x = _dot(u, p_ref[0])
    row = lax.broadcasted_iota(jnp.int32, x.shape, 0)
    tr = tr_ref[0]
    ti = ti_ref[0]
    for k in range(n_steps):
        s = 1 << k
        sh = jnp.where(row >= s, pltpu.roll(x, s, axis=0), 0.0)
        x = x + tr[k:k + 1, :] * sh + ti[k:k + 1, :] * pltpu.roll(sh, nst, axis=1)
    x_in = jnp.where(row >= 1, pltpu.roll(x, 1, axis=0), 0.0)
    y_ref[0, 0] = _dot(u, m_ref[0]) + _dot(x_in.astype(BF16), q_ref[0])


def _s5_scan(ug, m, p, q, tr, ti):
    ng, bsz, nck, w = ug.shape
    nst2 = p.shape[-1]
    n_steps = int(math.log2(nck))
    assert (1 << n_steps) == nck
    g3 = lambda blk: pl.BlockSpec((1,) + blk, lambda g, b: (g, 0, 0))
    return pl.pallas_call(
        functools.partial(_s5_scan_body, n_steps=n_steps),
        out_shape=jax.ShapeDtypeStruct((ng, bsz, nck, w), F32),
        grid=(ng, bsz),
        in_specs=[pl.BlockSpec((1, 1, nck, w), lambda g, b: (g, b, 0, 0)),
                  g3((w, w)), g3((w, nst2)), g3((nst2, w)), g3((16, nst2)), g3((16, nst2))],
        out_specs=pl.BlockSpec((1, 1, nck, w), lambda g, b: (g, b, 0, 0)),
        compiler_params=_params(("parallel", "parallel"), 32),
        name="s5_scan",
    )(ug, m, p, q, tr, ti)


def _glu_body(y_ref, u_ref, d_ref, w_ref, b_ref, o_ref):
    yy = y_ref[...] + d_ref[...] * u_ref[...]
    z = 0.5 * yy * (1.0 + jnp.tanh(math.sqrt(2.0 / math.pi) * (yy + 0.044715 * (yy * yy * yy))))
    gate = _sigmoid(_dot(z.astype(BF16), w_ref[...]) + b_ref[...])
    o_ref[...] = (z * gate).astype(o_ref.dtype)


def _glu(y, z2, u_blk, d_skip, w_glu, b_glu, tm=512):
    n, wd = y.shape
    return pl.pallas_call(
        _glu_body,
        out_shape=jax.ShapeDtypeStruct((n, wd), BF16),
        grid=(n // tm,),
        in_specs=[pl.BlockSpec((tm, wd), lambda i: (i, 0)),
                  pl.BlockSpec((tm, wd), lambda i: (i, u_blk)),
                  pl.BlockSpec((1, wd), lambda i: (0, 0)),
                  pl.BlockSpec((wd, wd), lambda i: (0, 0)),
                  pl.BlockSpec((1, wd), lambda i: (0, 0))],
        out_specs=pl.BlockSpec((tm, wd), lambda i: (i, 0)),
        compiler_params=_params(("parallel",), 32),
        name="s5_glu",
    )(y, z2, d_skip.astype(F32).reshape(1, wd), w_glu.astype(BF16), b_glu.astype(F32).reshape(1, wd))


def _pack_halves(v):
    c = v.shape[1] // 2
    return pltpu.pack_elementwise([v[:, :c], v[:, c:]], packed_dtype=BF16)


def _unpack_halves(w):
    return tuple(pltpu.unpack_elementwise(w, index=i, packed_dtype=BF16, unpacked_dtype=F32) for i in (0, 1))


def _xattn_body(hb_ref, h_ref, wq_ref, k_ref, v_ref, wo_ref, g_ref, b_ref, o_ref, ob_ref, op_ref, *, alpha):
    d = XA_HEAD_DIM
    q = _dot(hb_ref[...], wq_ref[...]).astype(BF16)
    outs = []
    for hd in range(XA_HEADS):
        sl = slice(hd * d, (hd + 1) * d)
        logits = _nt(q[:, sl], k_ref[0, :, sl]) * (d ** -0.5)
        m = jnp.max(logits, axis=-1, keepdims=True)
        p = jnp.exp(logits - m)
        p = p / jnp.sum(p, axis=-1, keepdims=True)
        outs.append(_dot(p.astype(BF16), v_ref[0, :, sl]))
    o = jnp.concatenate(outs, axis=-1).astype(BF16)
    v = alpha * h_ref[...] + _dot(o, wo_ref[...])
    mu = jnp.mean(v, axis=-1, keepdims=True)
    c = v - mu
    var = jnp.mean(c * c, axis=-1, keepdims=True)
    out = c * lax.rsqrt(var + LN_EPS) * g_ref[...] + b_ref[...]
    o_ref[...] = out
    ob_ref[...] = out.astype(BF16)
    op_ref[...] = _pack_halves(out)


def _xattn(h, hb, kmem, vmem, wq, wo, g, b, alpha, seq, tm=256):
    n, d = h.shape
    xw = wq.shape[1]
    ml = kmem.shape[1]
    per_b = seq // tm
    return pl.pallas_call(
        functools.partial(_xattn_body, alpha=alpha),
        out_shape=(jax.ShapeDtypeStruct((n, d), F32), jax.ShapeDtypeStruct((n, d), BF16),
                   jax.ShapeDtypeStruct((n, d // 2), jnp.uint32)),
        grid=(n // tm,),
        in_specs=[pl.BlockSpec((tm, d), lambda i: (i, 0)),
                  pl.BlockSpec((tm, d), lambda i: (i, 0)),
                  pl.BlockSpec((d, xw), lambda i: (0, 0)),
                  pl.BlockSpec((1, ml, xw), lambda i: (i // per_b, 0, 0)),
                  pl.BlockSpec((1, ml, xw), lambda i: (i // per_b, 0, 0)),
                  pl.BlockSpec((xw, d), lambda i: (0, 0)),
                  pl.BlockSpec((1, d), lambda i: (0, 0)),
                  pl.BlockSpec((1, d), lambda i: (0, 0))],
        out_specs=(pl.BlockSpec((tm, d), lambda i: (i, 0)),
                   pl.BlockSpec((tm, d), lambda i: (i, 0)),
                   pl.BlockSpec((tm, d // 2), lambda i: (i, 0))),
        compiler_params=_params(("parallel",), 56),
        name="xattn",
    )(hb, h, wq, kmem, vmem, wo, g.reshape(1, d), b.reshape(1, d))


def _first_max(v, idx, n):
    m = jnp.max(v, axis=0, keepdims=True)
    first = jnp.min(jnp.where(v == m, idx, n), axis=0, keepdims=True)
    return m, first


def _router_body(x_ref, r_ref, rb_ref, o_ref, *, tm):
    ne, ng = N_EXPERTS, N_EXPERT_GROUPS
    per = ne // ng
    logits = lax.dot_general(r_ref[...], x_ref[...], (((1,), (1,)), ((), ())),
                             preferred_element_type=F32, precision=lax.Precision.HIGHEST)
    scores = _sigmoid(logits)
    biased = scores + rb_ref[...]
    gsc = []
    eidx = lax.broadcasted_iota(jnp.int32, (per, tm), 0)
    for g in range(ng):
        v = biased[g * per:(g + 1) * per, :]
        m1, i1 = _first_max(v, eidx, per)
        m2 = jnp.max(jnp.where(eidx == i1, -jnp.inf, v), axis=0, keepdims=True)
        gsc.append(m1 + m2)
    gs = jnp.concatenate(gsc, axis=0)
    gidx = lax.broadcasted_iota(jnp.int32, (ng, tm), 0)
    gsel = jnp.zeros((ng, tm), jnp.bool_)
    for _ in range(TOPK_GROUPS):
        _, first = _first_max(gs, gidx, ng)
        hit = gidx == first
        gsel = jnp.logical_or(gsel, hit)
        gs = jnp.where(hit, -jnp.inf, gs)
    emask = jnp.concatenate([jnp.broadcast_to(gsel[g:g + 1, :], (per, tm)) for g in range(ng)], axis=0)
    cand = jnp.where(emask, biased, -jnp.inf)
    aidx = lax.broadcasted_iota(jnp.int32, (ne, tm), 0)
    picked, ids = [], []
    for _ in range(TOP_K):
        _, first = _first_max(cand, aidx, ne)
        hit = aidx == first
        picked.append(jnp.sum(jnp.where(hit, scores, 0.0), axis=0, keepdims=True))
        ids.append(first.astype(F32))
        cand = jnp.where(hit, -jnp.inf, cand)
    total = picked[0]
    for pk in picked[1:]:
        total = total + pk
    gates = [pk / total * ROUTED_SCALE for pk in picked]
    pad = jnp.zeros((LANES - 2 * TOP_K, tm), F32)
    o_ref[...] = jnp.concatenate(gates + ids + [pad], axis=0).T


def _router(h, router, router_bias, tm=512):
    n, d = h.shape
    ne = router.shape[0]
    return pl.pallas_call(
        functools.partial(_router_body, tm=tm),
        out_shape=jax.ShapeDtypeStruct((n, LANES), F32),
        grid=(n // tm,),
        in_specs=[pl.BlockSpec((tm, d), lambda i: (i, 0)),
                  pl.BlockSpec((ne, d), lambda i: (0, 0)),
                  pl.BlockSpec((ne, 1), lambda i: (0, 0))],
        out_specs=pl.BlockSpec((tm, LANES), lambda i: (i, 0)),
        compiler_params=_params(("parallel",), 48),
        name="moe_router",
    )(h, router.astype(F32), router_bias.astype(F32).reshape(ne, 1))


def _ffn_tile(x, wg, wu, wd, o_ref, accumulate, n_col=4):
    gte = _dot(x, wg)
    hid = ((gte * _sigmoid(gte)) * _dot(x, wu)).astype(BF16)
    d = o_ref.shape[1]
    cw = d // n_col
    for j in range(n_col):
        y = _dot(hid, wd[:, j * cw:(j + 1) * cw])
        if accumulate:
            o_ref[:, j * cw:(j + 1) * cw] += y
        else:
            o_ref[:, j * cw:(j + 1) * cw] = y


def _shared_body(x_ref, wg_ref, wu_ref, wd_ref, o_ref):
    @pl.when(pl.program_id(1) == 0)
    def _():
        o_ref[...] = jnp.zeros_like(o_ref)

    _ffn_tile(x_ref[...], wg_ref[...], wu_ref[...], wd_ref, o_ref, accumulate=True)


def _shared_ffn(hb, wg, wu, wd, tm=512, th=EXPERT_HIDDEN):
    n, d = hb.shape
    return pl.pallas_call(
        _shared_body,
        out_shape=jax.ShapeDtypeStruct((n, d), F32),
        grid=(n // tm, wg.shape[1] // th),
        in_specs=[pl.BlockSpec((tm, d), lambda i, e: (i, 0)),
                  pl.BlockSpec((d, th), lambda i, e: (0, e)),
                  pl.BlockSpec((d, th), lambda i, e: (0, e)),
                  pl.BlockSpec((th, d), lambda i, e: (e, 0))],
        out_specs=pl.BlockSpec((tm, d), lambda i, e: (i, 0)),
        compiler_params=_params(("parallel", "arbitrary"), 56),
        name="moe_shared",
    )(hb, wg, wu, wd)


MOE_ROW_TILE = 512
MOE_COMBINE_TOKENS = 128


def _start_row_gather(ids_ref, src_hbm, dst, sem, n_rows):
    for r in range(n_rows):
        pltpu.make_async_copy(src_hbm.at[pl.ds(ids_ref[0, 0, r], 1)], dst.at[pl.ds(r, 1)], sem).start(priority=r % 2)


def _wait_row_gather(src_hbm, dst, sem, n_rows):
    pltpu.make_async_copy(src_hbm.at[pl.ds(0, n_rows)], dst, sem).wait()


def _moe_sparse_body(te_ref, nv_ref, tok_ref, tokn_ref, h_hbm, wgf_ref, wuf_ref, wdf_ref, y_ref,
                     xbuf, sem, wg_ref, wu_ref, wd_ref, *, tr, n_col):
    i = pl.program_id(0)
    nv = nv_ref[0]
    slot = i % 2

    new_expert = jnp.logical_or(i == 0, te_ref[i] != te_ref[jnp.maximum(i - 1, 0)])

    @pl.when(jnp.logical_and(i < nv, new_expert))
    def _():
        wg_ref[0] = wgf_ref[0].astype(BF16)
        wu_ref[0] = wuf_ref[0].astype(BF16)
        wd_ref[0] = wdf_ref[0].astype(BF16)

    @pl.when(i == 0)
    def _():
        _start_row_gather(tok_ref, h_hbm, xbuf.at[0], sem.at[0], tr)

    @pl.when(i + 1 < nv)
    def _():
        _start_row_gather(tokn_ref, h_hbm, xbuf.at[1 - slot], sem.at[1 - slot], tr)

    @pl.when(i < nv)
    def _():
        _wait_row_gather(h_hbm, xbuf.at[slot], sem.at[slot], tr)
        x_lo, x_hi = _unpack_halves(xbuf[slot])
        x_lo, x_hi = x_lo.astype(BF16), x_hi.astype(BF16)
        half = x_lo.shape[1]
        gte = _dot(x_lo, wg_ref[0, :half, :]) + _dot(x_hi, wg_ref[0, half:, :])
        up = _dot(x_lo, wu_ref[0, :half, :]) + _dot(x_hi, wu_ref[0, half:, :])
        hid = ((gte * _sigmoid(gte)) * up).astype(BF16)
        cw = half // n_col
        for j in range(n_col):
            lo = _dot(hid, wd_ref[0, :, j * cw:(j + 1) * cw])
            hi = _dot(hid, wd_ref[0, :, half + j * cw:half + (j + 1) * cw])
            y_ref[:, j * cw:(j + 1) * cw] = pltpu.pack_elementwise([lo, hi], packed_dtype=BF16)

    @pl.when(i >= nv)
    def _():
        y_ref[...] = jnp.zeros_like(y_ref)


def _moe_sparse(h, wg, wu, wd, tok3, tile_expert, n_valid):
    n, d = h.shape
    n_tiles, _, tr = tok3.shape
    dm = 2 * d
    last = lambda i, nv: jnp.minimum(i, nv[0] - 1)
    grid_spec = pltpu.PrefetchScalarGridSpec(
        num_scalar_prefetch=2,
        grid=(n_tiles,),
        in_specs=[pl.BlockSpec((1, 1, tr), lambda i, te, nv: (i, 0, 0), memory_space=pltpu.SMEM),
                  pl.BlockSpec((1, 1, tr), lambda i, te, nv: (jnp.minimum(i + 1, n_tiles - 1), 0, 0),
                               memory_space=pltpu.SMEM),
                  pl.BlockSpec(memory_space=pl.ANY),
                  pl.BlockSpec((1, dm, wg.shape[2]), lambda i, te, nv: (te[last(i, nv)], 0, 0)),
                  pl.BlockSpec((1, dm, wu.shape[2]), lambda i, te, nv: (te[last(i, nv)], 0, 0)),
                  pl.BlockSpec((1, wd.shape[1], dm), lambda i, te, nv: (te[last(i, nv)], 0, 0))],
        out_specs=pl.BlockSpec((tr, d), lambda i, te, nv: (i, 0)),
        scratch_shapes=[pltpu.VMEM((2, tr, d), jnp.uint32), pltpu.SemaphoreType.DMA((2,)),
                        pltpu.VMEM((1,) + wg.shape[1:], BF16), pltpu.VMEM((1,) + wu.shape[1:], BF16),
                        pltpu.VMEM((1,) + wd.shape[1:], BF16)])
    return pl.pallas_call(
        functools.partial(_moe_sparse_body, tr=tr, n_col=2),
        out_shape=jax.ShapeDtypeStruct((n_tiles * tr, d), jnp.uint32),
        grid_spec=grid_spec,
        compiler_params=_params(("arbitrary",), 60),
        name="moe_sparse",
    )(tile_expert, n_valid, tok3, tok3, h, wg, wu, wd)


def _moe_combine_body(sl_ref, sln_ref, y_hbm, gate_ref, sh_ref, h_ref, g_ref, b_ref, o_ref, ob_ref, buf, sem,
                      *, tm, alpha, n_steps):
    i = pl.program_id(0)
    slot = i % 2
    rows = TOP_K * tm

    @pl.when(i == 0)
    def _():
        _start_row_gather(sl_ref, y_hbm, buf.at[0], sem.at[0], rows)

    @pl.when(i + 1 < n_steps)
    def _():
        _start_row_gather(sln_ref, y_hbm, buf.at[1 - slot], sem.at[1 - slot], rows)

    _wait_row_gather(y_hbm, buf.at[slot], sem.at[slot], rows)
    d = h_ref.shape[1]
    half = d // 2
    acc_lo = sh_ref[:, :half]
    acc_hi = sh_ref[:, half:]
    for k in range(TOP_K):
        lo, hi = _unpack_halves(buf[slot, k * tm:(k + 1) * tm, :])
        gk = gate_ref[:, k:k + 1]
        acc_lo = acc_lo + gk * lo
        acc_hi = acc_hi + gk * hi
    v_lo = alpha * h_ref[:, :half] + acc_lo
    v_hi = alpha * h_ref[:, half:] + acc_hi
    mu = (jnp.sum(v_lo, axis=-1, keepdims=True) + jnp.sum(v_hi, axis=-1, keepdims=True)) / d
    c_lo = v_lo - mu
    c_hi = v_hi - mu
    var = (jnp.sum(c_lo * c_lo, axis=-1, keepdims=True) + jnp.sum(c_hi * c_hi, axis=-1, keepdims=True)) / d
    rs = lax.rsqrt(var + LN_EPS)
    out_lo = c_lo * rs * g_ref[:, :half] + b_ref[:, :half]
    out_hi = c_hi * rs * g_ref[:, half:] + b_ref[:, half:]
    o_ref[:, :half] = out_lo
    o_ref[:, half:] = out_hi
    ob_ref[:, :half] = out_lo.astype(BF16)
    ob_ref[:, half:] = out_hi.astype(BF16)


def _moe_combine(y_sorted, slot3, rout, shared, h, g, b, alpha):
    n, d = h.shape
    n_steps, _, rows = slot3.shape
    tm = rows // TOP_K
    row = lambda i: (i, 0)
    return pl.pallas_call(
        functools.partial(_moe_combine_body, tm=tm, alpha=alpha, n_steps=n_steps),
        out_shape=(jax.ShapeDtypeStruct((n, d), F32), jax.ShapeDtypeStruct((n, d), BF16)),
        grid=(n_steps,),
        in_specs=[pl.BlockSpec((1, 1, rows), lambda i: (i, 0, 0), memory_space=pltpu.SMEM),
                  pl.BlockSpec((1, 1, rows), lambda i: (jnp.minimum(i + 1, n_steps - 1), 0, 0),
                               memory_space=pltpu.SMEM),
                  pl.BlockSpec(memory_space=pl.ANY),
                  pl.BlockSpec((tm, LANES), row),
                  pl.BlockSpec((tm, d), row),
                  pl.BlockSpec((tm, d), row),
                  pl.BlockSpec((1, d), lambda i: (0, 0)),
                  pl.BlockSpec((1, d), lambda i: (0, 0))],
        out_specs=(pl.BlockSpec((tm, d), row), pl.BlockSpec((tm, d), row)),
        scratch_shapes=[pltpu.VMEM((2, rows, d // 2), jnp.uint32), pltpu.SemaphoreType.DMA((2,))],
        compiler_params=_params(("arbitrary",), 48),
        name="moe_combine",
    )(slot3, slot3, y_sorted, rout, shared, h, g.reshape(1, d), b.reshape(1, d))


def _dispatch_plan(rout, n_experts, tr, tm):
    n = rout.shape[0]
    ids = rout[:, TOP_K:2 * TOP_K].astype(jnp.int32)
    onehot = jnp.sum((ids[:, :, None] == jnp.arange(n_experts)[None, None, :]).astype(jnp.int32), axis=1)
    csum = jnp.cumsum(onehot, axis=0)
    counts = csum[-1]
    padded = (counts + tr - 1) // tr * tr
    seg_end = jnp.cumsum(padded)
    seg_start = seg_end - padded
    rank = jnp.take_along_axis(csum - onehot, ids, axis=1)
    slot = seg_start[ids] + rank
    n_tiles = n * TOP_K // tr + n_experts
    tok = jnp.zeros((n_tiles * tr,), jnp.int32).at[slot.reshape(-1)].set(
        jnp.repeat(jnp.arange(n, dtype=jnp.int32), TOP_K), unique_indices=True)
    tile_start = jnp.arange(n_tiles, dtype=jnp.int32) * tr
    tile_expert = jnp.minimum(jnp.sum((seg_end[None, :] <= tile_start[:, None]).astype(jnp.int32), axis=1),
                              n_experts - 1)
    n_valid = (seg_end[-1:] // tr).astype(jnp.int32)
    slot3 = slot.reshape(n // tm, tm, TOP_K).transpose(0, 2, 1).reshape(n // tm, 1, TOP_K * tm)
    return tok.reshape(n_tiles, 1, tr), tile_expert, n_valid, slot3.astype(jnp.int32)


def kernel(x, mem, ev_w_in, ev_pool_w, ev_pool_scale, ev_sinks, ev_w_out, rel_bias, od_w_in, hg_lb_logits, od_hg_norm, od_a_re, od_a_im, od_log_step, od_b_re, od_b_im, od_c_re, od_c_im, od_d_skip, od_w_glu, od_b_glu, od_w_out, xa_wq, xa_wk, xa_wv, xa_wo, moe_router, moe_bias, moe_w_gate, moe_w_up, moe_w_down, sh_w_gate, sh_w_up, sh_w_down, ln_g, ln_b):
    bsz, seq, d = x.shape
    n = bsz * seq
    depth = ln_g.shape[0]
    alpha = (2 * depth) ** 0.25
    memb = mem.astype(BF16).reshape(bsz * mem.shape[1], d)

    h = x.astype(F32).reshape(n, d)
    hb = h.astype(BF16)
    bias_tab = _bias_table(rel_bias)

    for l in range(depth):
        j = l // 2
        if l % 2 == 0:
            pool_w = ev_pool_w[j]
            pool_width = pool_w.shape[0] * pool_w.shape[1]
            n_q = ev_sinks.shape[1]
            n_kv = n_q // SWA_GQ
            z = _mm(hb, ev_w_in[j].astype(BF16), F32)
            z3 = z.reshape(bsz, seq, z.shape[1])
            y_a = _pool(z3, pool_w.astype(BF16), ev_pool_scale[j].astype(F32))
            y_b = _swa(z3, ev_sinks[j], bias_tab, pool_width, n_q, n_kv)
            mix = _mm2(y_a.reshape(n, -1), y_b.reshape(n, -1), ev_w_out[j].astype(BF16), F32)
        else:
            n_heads = hg_lb_logits.shape[1] // HG_DIM
            hgw = n_heads * HG_DIM
            ng, nst = od_a_re.shape[1], od_a_re.shape[2]
            s5w = ng * S5_GROUP
            z = _mm(hb, od_w_in[j].astype(BF16), F32)
            z3 = z.reshape(bsz, seq, z.shape[1])
            y_c = _hgrn(z3, hg_lb_logits, od_hg_norm[j], l, n_heads)
            m_mat, p_mat, q_mat, t_re, t_im = _s5_prep(od_a_re[j], od_a_im[j], od_log_step[j], od_b_re[j],
                                                        od_b_im[j], od_c_re[j], od_c_im[j])
            nck = seq // S5_CHUNK
            u = z3[:, :, 4 * hgw:].astype(BF16).reshape(bsz, nck, S5_CHUNK, ng, S5_GROUP)
            ug = u.transpose(3, 0, 1, 2, 4).reshape(ng, bsz, nck, S5_CHUNK * S5_GROUP)
            yg = _s5_scan(ug, m_mat, p_mat, q_mat, t_re, t_im)
            y_s = yg.reshape(ng, bsz, nck, S5_CHUNK, S5_GROUP).transpose(1, 2, 3, 0, 4).reshape(n, s5w)
            y_d = _glu(y_s, z, (4 * hgw) // s5w, od_d_skip[j], od_w_glu[j], od_b_glu[j])
            mix = _mm2(y_c.reshape(n, hgw), y_d, od_w_out[j].astype(BF16), F32)
        h, hb = _add_ln(h, mix, ln_g[l, 0], ln_b[l, 0], alpha)

        xw = xa_wq.shape[2]
        kmem = _mm(memb, xa_wk[l].astype(BF16), BF16).reshape(bsz, -1, xw)
        vmem = _mm(memb, xa_wv[l].astype(BF16), BF16).reshape(bsz, -1, xw)
        h, hb, hp = _xattn(h, hb, kmem, vmem, xa_wq[l].astype(BF16), xa_wo[l].astype(BF16),
                           ln_g[l, 1], ln_b[l, 1], alpha, seq)

        n_exp = moe_router.shape[1]
        rout = _router(h, moe_router[l], moe_bias[l])
        tok3, tile_expert, n_valid, slot3 = _dispatch_plan(rout, n_exp, MOE_ROW_TILE, MOE_COMBINE_TOKENS)
        shared = _shared_ffn(hb, sh_w_gate[l].astype(BF16), sh_w_up[l].astype(BF16), sh_w_down[l].astype(BF16))
        y_sorted = _moe_sparse(hp, moe_w_gate[l].astype(F32), moe_w_up[l].astype(F32), moe_w_down[l].astype(F32),
                               tok3, tile_expert, n_valid)
        h, hb = _moe_combine(y_sorted, slot3, rout, shared, h, ln_g[l, 2], ln_b[l, 2], alpha)

    return h.reshape(bsz, seq, d)
```

```python
import functools
import math

import jax
import jax.numpy as jnp
from jax import lax
from jax.experimental import pallas as pl
from jax.experimental.pallas import tpu as pltpu

F32 = jnp.float32
BF16 = jnp.bfloat16

V7X_VMEM_BYTES = 64 * 1024 * 1024
LANES = 128
SUBLANES = 8

POOL_WINDOWS = (2, 4, 8, 16)
SWA_HEAD_DIM = 64
SWA_GQ = 8
SWA_BLOCK = 128
REL_BUCKETS = 32
REL_MAX_DIST = 128
HG_DIM = 128
HG_CHUNK = 64
RMS_EPS = 1e-6
S5_GROUP = 16
S5_STATE = 64
S5_CHUNK = 16
XA_HEADS = 4
XA_HEAD_DIM = 128
N_EXPERTS = 64
TOP_K = 8
N_EXPERT_GROUPS = 8
TOPK_GROUPS = 4
EXPERT_HIDDEN = 256
ROUTED_SCALE = 2.5
LN_EPS = 1e-5


def _params(sem, vmem_mb):
    return pltpu.CompilerParams(dimension_semantics=sem, vmem_limit_bytes=vmem_mb * 1024 * 1024)


def _nt(a, b):
    return lax.dot_general(a, b, (((1,), (1,)), ((), ())), preferred_element_type=F32)


def _tn(a, b):
    return lax.dot_general(a, b, (((0,), (0,)), ((), ())), preferred_element_type=F32)


def _dot(a, b):
    return jnp.dot(a, b, preferred_element_type=F32)


def _sigmoid(x):
    return 0.5 * (jnp.tanh(0.5 * x) + 1.0)


def _mm_body(x_ref, w_ref, o_ref):
    o_ref[...] = _dot(x_ref[...], w_ref[...]).astype(o_ref.dtype)


def _mm(x, w, out_dtype, tm=1024, tn=512):
    m, k = x.shape
    n = w.shape[1]
    tm = min(tm, m)
    tn = min(tn, n)
    return pl.pallas_call(
        _mm_body,
        out_shape=jax.ShapeDtypeStruct((m, n), out_dtype),
        grid=(m // tm, n // tn),
        in_specs=[pl.BlockSpec((tm, k), lambda i, j: (i, 0)),
                  pl.BlockSpec((k, tn), lambda i, j: (0, j))],
        out_specs=pl.BlockSpec((tm, tn), lambda i, j: (i, j)),
        compiler_params=_params(("parallel", "parallel"), 48),
        name="mm",
    )(x, w)


def _mm2_body(a_ref, b_ref, w_ref, o_ref):
    ka = a_ref.shape[1]
    o_ref[...] = (_dot(a_ref[...], w_ref[:ka, :]) + _dot(b_ref[...], w_ref[ka:, :])).astype(o_ref.dtype)


def _mm2(a, b, w, out_dtype, tm=1024, tn=512):
    m, ka = a.shape
    kb = b.shape[1]
    n = w.shape[1]
    return pl.pallas_call(
        _mm2_body,
        out_shape=jax.ShapeDtypeStruct((m, n), out_dtype),
        grid=(m // tm, n // tn),
        in_specs=[pl.BlockSpec((tm, ka), lambda i, j: (i, 0)),
                  pl.BlockSpec((tm, kb), lambda i, j: (i, 0)),
                  pl.BlockSpec((ka + kb, tn), lambda i, j: (0, j))],
        out_specs=pl.BlockSpec((tm, tn), lambda i, j: (i, j)),
        compiler_params=_params(("parallel", "parallel"), 48),
        name="mm2",
    )(a, b, w)


def _add_ln_body(h_ref, y_ref, g_ref, b_ref, o_ref, ob_ref, *, alpha):
    v = alpha * h_ref[...] + y_ref[...].astype(F32)
    mu = jnp.mean(v, axis=-1, keepdims=True)
    c = v - mu
    var = jnp.mean(c * c, axis=-1, keepdims=True)
    out = c * lax.rsqrt(var + LN_EPS) * g_ref[...] + b_ref[...]
    o_ref[...] = out
    ob_ref[...] = out.astype(BF16)


def _add_ln(h, y, g, b, alpha, tm=256):
    n, d = h.shape
    return pl.pallas_call(
        functools.partial(_add_ln_body, alpha=alpha),
        out_shape=(jax.ShapeDtypeStruct((n, d), F32), jax.ShapeDtypeStruct((n, d), BF16)),
        grid=(n // tm,),
        in_specs=[pl.BlockSpec((tm, d), lambda i: (i, 0)),
                  pl.BlockSpec((tm, d), lambda i: (i, 0)),
                  pl.BlockSpec((1, d), lambda i: (0, 0)),
                  pl.BlockSpec((1, d), lambda i: (0, 0))],
        out_specs=(pl.BlockSpec((tm, d), lambda i: (i, 0)),
                   pl.BlockSpec((tm, d), lambda i: (i, 0))),
        compiler_params=_params(("parallel",), 48),
        name="add_ln",
    )(h, y, g.reshape(1, d), b.reshape(1, d))


def _pool_body(u_ref, prev_ref, w_ref, sc_ref, o_ref, *, ts, gw, halo):
    i = pl.program_id(1)
    row = lax.broadcasted_iota(jnp.int32, (ts, gw), 0)
    pos1 = (row + i * ts + 1).astype(F32)
    for g, win in enumerate(POOL_WINDOWS):
        u = u_ref[0, :, g * gw:(g + 1) * gw]
        prev = prev_ref[0, :, g * gw:(g + 1) * gw]
        prev = jnp.where(i == 0, 0.0, prev)
        s = jnp.concatenate([prev, u], axis=0)
        step = 1
        while step < win:
            s = s + pltpu.roll(s, step, axis=0)
            step *= 2
        wsum = s[halo:, :]
        pooled = wsum / jnp.minimum(pos1, float(win)) - u
        y = _dot(pooled.astype(BF16), w_ref[g])
        o_ref[0, :, g * gw:(g + 1) * gw] = (y * sc_ref[:, g * gw:(g + 1) * gw]).astype(o_ref.dtype)


def _pool(z3, pool_w, pool_scale, ts=512):
    bsz, s, _ = z3.shape
    ng, gw, _ = pool_w.shape
    width = ng * gw
    halo = max(POOL_WINDOWS)
    return pl.pallas_call(
        functools.partial(_pool_body, ts=ts, gw=gw, halo=halo),
        out_shape=jax.ShapeDtypeStruct((bsz, s, width), BF16),
        grid=(bsz, s // ts),
        in_specs=[pl.BlockSpec((1, ts, width), lambda b, i: (b, i, 0)),
                  pl.BlockSpec((1, halo, width), lambda b, i: (b, jnp.maximum(i * (ts // halo) - 1, 0), 0)),
                  pl.BlockSpec((ng, gw, gw), lambda b, i: (0, 0, 0)),
                  pl.BlockSpec((1, width), lambda b, i: (0, 0))],
        out_specs=pl.BlockSpec((1, ts, width), lambda b, i: (b, i, 0)),
        compiler_params=_params(("parallel", "parallel"), 48),
        name="pool",
    )(z3, z3, pool_w, pool_scale.reshape(1, width))


def _t5_bucket_table():
    qi = jnp.arange(SWA_BLOCK)[:, None]
    kj = jnp.arange(2 * SWA_BLOCK)[None, :]
    dist = qi + SWA_BLOCK - kj
    band = (dist >= 0) & (dist < SWA_BLOCK)
    d = jnp.maximum(dist, 0)
    exact = REL_BUCKETS // 2
    far = exact + (jnp.log(jnp.maximum(d, exact).astype(F32) / exact)
                   / math.log(REL_MAX_DIST / exact) * (REL_BUCKETS - exact)).astype(jnp.int32)
    bucket = jnp.where(d < exact, d, jnp.minimum(far, REL_BUCKETS - 1))
    return jnp.where(band, bucket, -1).astype(jnp.int32)


def _bias_body(rb_ref, bucket_ref, o_ref):
    h = pl.program_id(0)
    bucket = bucket_ref[...]
    acc = jnp.full(bucket.shape, -jnp.inf, F32)
    for b in range(REL_BUCKETS):
        acc = jnp.where(bucket == b, rb_ref[b, h], acc)
    o_ref[0] = acc


def _bias_table(rel_bias):
    nb, nh = rel_bias.shape
    bucket = _t5_bucket_table()
    return pl.pallas_call(
        _bias_body,
        out_shape=jax.ShapeDtypeStruct((nh, SWA_BLOCK, 2 * SWA_BLOCK), F32),
        grid=(nh,),
        in_specs=[pl.BlockSpec(memory_space=pltpu.SMEM),
                  pl.BlockSpec((SWA_BLOCK, 2 * SWA_BLOCK), lambda h: (0, 0))],
        out_specs=pl.BlockSpec((1, SWA_BLOCK, 2 * SWA_BLOCK), lambda h: (h, 0, 0)),
        compiler_params=_params(("parallel",), 32),
        name="swa_bias",
    )(rel_bias.astype(F32), bucket)


def _swa_body(sink_ref, q_ref, kp_ref, kc_ref, vp_ref, vc_ref, bias_ref, o_ref, *, n_kv):
    n = pl.program_id(1)
    dh = SWA_HEAD_DIM
    col = lax.broadcasted_iota(jnp.int32, (SWA_BLOCK, 2 * SWA_BLOCK), 1)
    no_prev = jnp.logical_and(n == 0, col < SWA_BLOCK)
    scale = dh ** -0.5
    for kv in range(n_kv):
        sl = slice(kv * dh, (kv + 1) * dh)
        k2 = jnp.concatenate([kp_ref[0, :, sl], kc_ref[0, :, sl]], axis=0).astype(BF16)
        v2 = jnp.concatenate([vp_ref[0, :, sl], vc_ref[0, :, sl]], axis=0).astype(BF16)
        for g in range(SWA_GQ):
            h = kv * SWA_GQ + g
            qh = q_ref[0, :, h * dh:(h + 1) * dh].astype(BF16)
            logits = _nt(qh, k2) * scale + bias_ref[h]
            logits = jnp.where(no_prev, -jnp.inf, logits)
            sink = sink_ref[h]
            m = jnp.maximum(jnp.max(logits, axis=-1, keepdims=True), sink)
            p = jnp.exp(logits - m)
            p = p / (jnp.sum(p, axis=-1, keepdims=True) + jnp.exp(sink - m))
            o_ref[0, :, h * dh:(h + 1) * dh] = _dot(p.astype(BF16), v2).astype(o_ref.dtype)


def _swa(z3, sinks, bias_tab, q_off, n_q, n_kv):
    bsz, s, _ = z3.shape
    dh = SWA_HEAD_DIM
    qw, kvw = n_q * dh, n_kv * dh
    qb = q_off // qw
    kb = (q_off + qw) // kvw
    vb = kb + 1
    blk = SWA_BLOCK
    prev = lambda n: jnp.maximum(n - 1, 0)
    return pl.pallas_call(
        functools.partial(_swa_body, n_kv=n_kv),
        out_shape=jax.ShapeDtypeStruct((bsz, s, qw), BF16),
        grid=(bsz, s // blk),
        in_specs=[pl.BlockSpec(memory_space=pltpu.SMEM),
                  pl.BlockSpec((1, blk, qw), lambda b, n: (b, n, qb)),
                  pl.BlockSpec((1, blk, kvw), lambda b, n: (b, prev(n), kb)),
                  pl.BlockSpec((1, blk, kvw), lambda b, n: (b, n, kb)),
                  pl.BlockSpec((1, blk, kvw), lambda b, n: (b, prev(n), vb)),
                  pl.BlockSpec((1, blk, kvw), lambda b, n: (b, n, vb)),
                  pl.BlockSpec((n_q, blk, 2 * blk), lambda b, n: (0, 0, 0))],
        out_specs=pl.BlockSpec((1, blk, qw), lambda b, n: (b, n, 0)),
        compiler_params=_params(("parallel", "parallel"), 48),
        name="swa",
    )(sinks.astype(F32), z3, z3, z3, z3, z3, bias_tab)


def _hgrn_body(q_ref, f_ref, i_ref, g_ref, lbl_ref, nw_ref, o_ref, st_ref, *, n_chunk, layer):
    @pl.when(pl.program_id(2) == 0)
    def _():
        st_ref[...] = jnp.zeros_like(st_ref)

    lg = lbl_ref[:, 0, 0, :]
    e = jnp.exp(lg - jnp.max(lg, axis=0, keepdims=True))
    p = e / jnp.sum(e, axis=0, keepdims=True)
    lb = jnp.sum(p[:layer + 1], axis=0, keepdims=True) - p[0:1]
    nw = nw_ref[...]
    c = HG_CHUNK
    r = lax.broadcasted_iota(jnp.int32, (c, c), 0)
    cc = lax.broadcasted_iota(jnp.int32, (c, c), 1)
    causal = r >= cc
    tri = jnp.where(causal, 1.0, 0.0).astype(BF16)

    d = HG_DIM
    chunks = lambda t: [t[ci * c:(ci + 1) * c] for ci in range(n_chunk)]
    q = q_ref[0]
    iv = chunks(i_ref[0].astype(BF16))
    g = g_ref[0]
    qf = q * _sigmoid(q)
    forget = lb + (1.0 - lb) * _sigmoid(f_ref[0])
    log_f = jnp.log(forget)
    k_in = 1.0 - forget
    hi = log_f.astype(BF16)
    r1 = log_f - hi.astype(F32)
    mid = r1.astype(BF16)
    lo = (r1 - mid.astype(F32)).astype(BF16)
    cum = jnp.concatenate([_dot(tri, a) + _dot(tri, b) + _dot(tri, e3)
                           for a, b, e3 in zip(chunks(hi), chunks(mid), chunks(lo))], axis=0)
    cum3 = cum.reshape(n_chunk, c, d)
    ref = cum3[:, c // 2:c // 2 + 1, :]
    total = cum3[:, c - 1:c, :]
    qa = qf.reshape(n_chunk, c, d) * jnp.exp(cum3 - ref)
    kb = k_in.reshape(n_chunk, c, d) * jnp.exp(ref - cum3)
    qe = chunks((qa * jnp.exp(ref)).astype(BF16).reshape(n_chunk * c, d))
    ku = chunks((kb * jnp.exp(total - ref)).astype(BF16).reshape(n_chunk * c, d))
    qa = chunks(qa.astype(BF16).reshape(n_chunk * c, d))
    kb = chunks(kb.astype(BF16).reshape(n_chunk * c, d))
    decay = jnp.exp(total)
    st = st_ref[...]
    outs = []
    for ci in range(n_chunk):
        scores = jnp.where(causal, _nt(qa[ci], kb[ci]), 0.0)
        o_intra = _dot(scores.astype(BF16), iv[ci])
        o_inter = _nt(qe[ci], st.astype(BF16))
        st = st * decay[ci] + _tn(iv[ci], ku[ci])
        outs.append(o_intra + o_inter)
    st_ref[...] = st
    o = jnp.concatenate(outs, axis=0)
    o = o * lax.rsqrt(jnp.mean(o * o, axis=-1, keepdims=True) + RMS_EPS) * nw
    o_ref[0] = (o * (g * _sigmoid(g))).astype(o_ref.dtype)


def _hgrn(z3, lb_logits, norm_w, layer, n_heads, ts=512):
    bsz, s, _ = z3.shape
    depth = lb_logits.shape[0]
    d = HG_DIM
    spec = lambda off: pl.BlockSpec((1, ts, d), lambda b, h, t: (b, t, off + h))
    return pl.pallas_call(
        functools.partial(_hgrn_body, n_chunk=ts // HG_CHUNK, layer=layer),
        out_shape=jax.ShapeDtypeStruct((bsz, s, n_heads * d), BF16),
        grid=(bsz, n_heads, s // ts),
        in_specs=[spec(0), spec(n_heads), spec(2 * n_heads), spec(3 * n_heads),
                  pl.BlockSpec((depth, 1, 1, d), lambda b, h, t: (0, h, 0, 0)),
                  pl.BlockSpec((1, d), lambda b, h, t: (0, 0))],
        out_specs=pl.BlockSpec((1, ts, d), lambda b, h, t: (b, t, h)),
        scratch_shapes=[pltpu.VMEM((d, d), F32)],
        compiler_params=_params(("parallel", "parallel", "arbitrary"), 32),
        name="hgrn2",
    )(z3, z3, z3, z3, lb_logits.astype(F32).reshape(depth, n_heads, 1, d), norm_w.astype(F32).reshape(1, d))


def _cpow(pw, lam_re, lam_im):
    mag = jnp.exp(pw * lam_re)
    ang = pw * lam_im
    return mag * jnp.cos(ang), mag * jnp.sin(ang)


def _s5_prep_body(ls_ref, ar_ref, ai_ref, arc_ref, aic_ref, br_ref, bi_ref, cr_ref, ci_ref,
                  m_ref, p_ref, q_ref, tr_ref, ti_ref):
    nst = S5_STATE
    lc = S5_CHUNK
    gc = S5_GROUP
    w = lc * gc
    dt = jnp.exp(ls_ref[0])
    a_re, a_im = ar_ref[0], ai_ref[0]
    lam_re, lam_im = dt * a_re, dt * a_im
    ab_re, ab_im = _cpow(1.0, lam_re, lam_im)
    den = a_re * a_re + a_im * a_im
    co_re = ((ab_re - 1.0) * a_re + ab_im * a_im) / den
    co_im = (ab_im * a_re - (ab_re - 1.0) * a_im) / den
    b_re, b_im = br_ref[0], bi_ref[0]
    bb_re = co_re * b_re - co_im * b_im
    bb_im = co_re * b_im + co_im * b_re
    lane = lax.broadcasted_iota(jnp.int32, (w, 2 * nst), 1)
    rowi = lax.broadcasted_iota(jnp.int32, (w, 2 * nst), 0)

    pw = (lc - 1 - rowi // gc).astype(F32)
    ap_re, ap_im = _cpow(pw, lam_re, lam_im)
    tb_re = jnp.concatenate([bb_re] * lc, axis=0)
    tb_im = jnp.concatenate([bb_im] * lc, axis=0)
    p_ref[0] = jnp.where(lane < nst, ap_re * tb_re - ap_im * tb_im,
                         ap_re * tb_im + ap_im * tb_re).astype(p_ref.dtype)

    lam_re_c, lam_im_c = dt * arc_ref[0], dt * aic_ref[0]
    c_re, c_im = cr_ref[0], ci_ref[0]
    lane_q = lax.broadcasted_iota(jnp.int32, (2 * nst, w), 1)
    row_q = lax.broadcasted_iota(jnp.int32, (2 * nst, w), 0)
    tq = (lane_q // gc).astype(F32)

    def c_times_apow(pwq):
        pr, pi = _cpow(pwq, lam_re_c, lam_im_c)
        return jnp.where(row_q < nst, c_re * pr - c_im * pi, -(c_re * pi + c_im * pr))

    q_ref[0] = c_times_apow(tq + 1.0).astype(q_ref.dtype)
    wmat = c_times_apow(tq)

    bbt = jnp.where(lax.broadcasted_iota(jnp.int32, (gc, 2 * nst), 1) < nst, bb_re, bb_im)
    r = jnp.dot(bbt, wmat, preferred_element_type=F32, precision=lax.Precision.HIGHEST)
    lane_r = lax.broadcasted_iota(jnp.int32, (gc, w), 1)
    for t in range(lc):
        blk = r if t == 0 else jnp.where(lane_r >= t * gc, pltpu.roll(r, t * gc, axis=1), 0.0)
        m_ref[0, t * gc:(t + 1) * gc, :] = blk.astype(m_ref.dtype)

    kk = lax.broadcasted_iota(jnp.int32, (16, 2 * nst), 0)
    pws = (lc * jnp.left_shift(1, jnp.minimum(kk, 12))).astype(F32)
    sr, si = _cpow(pws, lam_re, lam_im)
    tr_ref[0] = sr
    ti_ref[0] = jnp.where(lax.broadcasted_iota(jnp.int32, (16, 2 * nst), 1) < nst, -si, si)


def _s5_prep(a_re, a_im, log_step, b_re, b_im, c_re, c_im):
    ng, nst = a_re.shape
    gc, lc = S5_GROUP, S5_CHUNK
    w = lc * gc
    dup = lambda t: jnp.concatenate([t, t], axis=-1)
    ar = dup(a_re.astype(F32))
    ai = dup(a_im.astype(F32))
    brt = dup(jnp.swapaxes(b_re.astype(F32), 1, 2))
    bit = dup(jnp.swapaxes(b_im.astype(F32), 1, 2))
    tile_c = lambda t: jnp.tile(jnp.concatenate([jnp.swapaxes(t.astype(F32), 1, 2)] * 2, axis=1), (1, 1, lc))
    crt, cit = tile_c(c_re), tile_c(c_im)
    g3 = lambda blk: pl.BlockSpec((1,) + blk, lambda g: (g, 0, 0))
    return pl.pallas_call(
        _s5_prep_body,
        out_shape=(jax.ShapeDtypeStruct((ng, w, w), BF16),
                   jax.ShapeDtypeStruct((ng, w, 2 * nst), BF16),
                   jax.ShapeDtypeStruct((ng, 2 * nst, w), BF16),
                   jax.ShapeDtypeStruct((ng, 16, 2 * nst), F32),
                   jax.ShapeDtypeStruct((ng, 16, 2 * nst), F32)),
        grid=(ng,),
        in_specs=[g3((1, 1)), g3((1, 2 * nst)), g3((1, 2 * nst)), g3((2 * nst, 1)), g3((2 * nst, 1)),
                  g3((gc, 2 * nst)), g3((gc, 2 * nst)), g3((2 * nst, w)), g3((2 * nst, w))],
        out_specs=(g3((w, w)), g3((w, 2 * nst)), g3((2 * nst, w)), g3((16, 2 * nst)), g3((16, 2 * nst))),
        compiler_params=_params(("parallel",), 32),
        name="s5_prep",
    )(log_step.astype(F32).reshape(ng, 1, 1), ar.reshape(ng, 1, 2 * nst), ai.reshape(ng, 1, 2 * nst),
      ar.reshape(ng, 2 * nst, 1), ai.reshape(ng, 2 * nst, 1), brt, bit, crt, cit)


def _s5_scan_body(u_ref, m_ref, p_ref, q_ref, tr_ref, ti_ref, y_ref, *, n_steps):
    nst = S5_STATE
    u = u_ref[0, 0]
    x = _dot(u, p_ref[0])
    row = lax.broadcasted_iota(jnp.int32, x.shape, 0)
    tr = tr_ref[0]
    ti = ti_ref[0]
    for k in range(n_steps):
        s = 1 << k
        sh = jnp.where(row >= s, pltpu.roll(x, s, axis=0), 0.0)
        x = x + tr[k:k + 1, :] * sh + ti[k:k + 1, :] * pltpu.roll(sh, nst, axis=1)
    x_in = jnp.where(row >= 1, pltpu.roll(x, 1, axis=0), 0.0)
    y_ref[0, 0] = _dot(u, m_ref[0]) + _dot(x_in.astype(BF16), q_ref[0])


def _s5_scan(ug, m, p, q, tr, ti):
    ng, bsz, nck, w = ug.shape
    nst2 = p.shape[-1]
    n_steps = int(math.log2(nck))
    assert (1 << n_steps) == nck
    g3 = lambda blk: pl.BlockSpec((1,) + blk, lambda g, b: (g, 0, 0))
    return pl.pallas_call(
        functools.partial(_s5_scan_body, n_steps=n_steps),
        out_shape=jax.ShapeDtypeStruct((ng, bsz, nck, w), F32),
        grid=(ng, bsz),
        in_specs=[pl.BlockSpec((1, 1, nck, w), lambda g, b: (g, b, 0, 0)),
                  g3((w, w)), g3((w, nst2)), g3((nst2, w)), g3((16, nst2)), g3((16, nst2))],
        out_specs=pl.BlockSpec((1, 1, nck, w), lambda g, b: (g, b, 0, 0)),
        compiler_params=_params(("parallel", "parallel"), 32),
        name="s5_scan",
    )(ug, m, p, q, tr, ti)


def _glu_body(y_ref, u_ref, d_ref, w_ref, b_ref, o_ref):
    yy = y_ref[...] + d_ref[...] * u_ref[...]
    z = 0.5 * yy * (1.0 + jnp.tanh(math.sqrt(2.0 / math.pi) * (yy + 0.044715 * (yy * yy * yy))))
    gate = _sigmoid(_dot(z.astype(BF16), w_ref[...]) + b_ref[...])
    o_ref[...] = (z * gate).astype(o_ref.dtype)


def _glu(y, z2, u_blk, d_skip, w_glu, b_glu, tm=512):
    n, wd = y.shape
    return pl.pallas_call(
        _glu_body,
        out_shape=jax.ShapeDtypeStruct((n, wd), BF16),
        grid=(n // tm,),
        in_specs=[pl.BlockSpec((tm, wd), lambda i: (i, 0)),
                  pl.BlockSpec((tm, wd), lambda i: (i, u_blk)),
                  pl.BlockSpec((1, wd), lambda i: (0, 0)),
                  pl.BlockSpec((wd, wd), lambda i: (0, 0)),
                  pl.BlockSpec((1, wd), lambda i: (0, 0))],
        out_specs=pl.BlockSpec((tm, wd), lambda i: (i, 0)),
        compiler_params=_params(("parallel",), 32),
        name="s5_glu",
    )(y, z2, d_skip.astype(F32).reshape(1, wd), w_glu.astype(BF16), b_glu.astype(F32).reshape(1, wd))


def _pack_halves(v):
    c = v.shape[1] // 2
    return pltpu.pack_elementwise([v[:, :c], v[:, c:]], packed_dtype=BF16)


def _unpack_halves(w):
    return tuple(pltpu.unpack_elementwise(w, index=i, packed_dtype=BF16, unpacked_dtype=F32) for i in (0, 1))


def _xattn_body(hb_ref, h_ref, wq_ref, k_ref, v_ref, wo_ref, g_ref, b_ref, o_ref, ob_ref, op_ref, *, alpha):
    d = XA_HEAD_DIM
    q = _dot(hb_ref[...], wq_ref[...]).astype(BF16)
    outs = []
    for hd in range(XA_HEADS):
        sl = slice(hd * d, (hd + 1) * d)
        logits = _nt(q[:, sl], k_ref[0, :, sl]) * (d ** -0.5)
        m = jnp.max(logits, axis=-1, keepdims=True)
        p = jnp.exp(logits - m)
        p = p / jnp.sum(p, axis=-1, keepdims=True)
        outs.append(_dot(p.astype(BF16), v_ref[0, :, sl]))
    o = jnp.concatenate(outs, axis=-1).astype(BF16)
    v = alpha * h_ref[...] + _dot(o, wo_ref[...])
    mu = jnp.mean(v, axis=-1, keepdims=True)
    c = v - mu
    var = jnp.mean(c * c, axis=-1, keepdims=True)
    out = c * lax.rsqrt(var + LN_EPS) * g_ref[...] + b_ref[...]
    o_ref[...] = out
    ob_ref[...] = out.astype(BF16)
    op_ref[...] = _pack_halves(out)


def _xattn(h, hb, kmem, vmem, wq, wo, g, b, alpha, seq, tm=256):
    n, d = h.shape
    xw = wq.shape[1]
    ml = kmem.shape[1]
    per_b = seq // tm
    return pl.pallas_call(
        functools.partial(_xattn_body, alpha=alpha),
        out_shape=(jax.ShapeDtypeStruct((n, d), F32), jax.ShapeDtypeStruct((n, d), BF16),
                   jax.ShapeDtypeStruct((n, d // 2), jnp.uint32)),
        grid=(n // tm,),
        in_specs=[pl.BlockSpec((tm, d), lambda i: (i, 0)),
                  pl.BlockSpec((tm, d), lambda i: (i, 0)),
                  pl.BlockSpec((d, xw), lambda i: (0, 0)),
                  pl.BlockSpec((1, ml, xw), lambda i: (i // per_b, 0, 0)),
                  pl.BlockSpec((1, ml, xw), lambda i: (i // per_b, 0, 0)),
                  pl.BlockSpec((xw, d), lambda i: (0, 0)),
                  pl.BlockSpec((1, d), lambda i: (0, 0)),
                  pl.BlockSpec((1, d), lambda i: (0, 0))],
        out_specs=(pl.BlockSpec((tm, d), lambda i: (i, 0)),
                   pl.BlockSpec((tm, d), lambda i: (i, 0)),
                   pl.BlockSpec((tm, d // 2), lambda i: (i, 0))),
        compiler_params=_params(("parallel",), 56),
        name="xattn",
    )(hb, h, wq, kmem, vmem, wo, g.reshape(1, d), b.reshape(1, d))


def _first_max(v, idx, n):
    m = jnp.max(v, axis=0, keepdims=True)
    first = jnp.min(jnp.where(v == m, idx, n), axis=0, keepdims=True)
    return m, first


def _router_body(x_ref, r_ref, rb_ref, o_ref, *, tm):
    ne, ng = N_EXPERTS, N_EXPERT_GROUPS
    per = ne // ng
    logits = lax.dot_general(r_ref[...], x_ref[...], (((1,), (1,)), ((), ())),
                             preferred_element_type=F32, precision=lax.Precision.HIGHEST)
    scores = _sigmoid(logits)
    biased = scores + rb_ref[...]
    gsc = []
    eidx = lax.broadcasted_iota(jnp.int32, (per, tm), 0)
    for g in range(ng):
        v = biased[g * per:(g + 1) * per, :]
        m1, i1 = _first_max(v, eidx, per)
        m2 = jnp.max(jnp.where(eidx == i1, -jnp.inf, v), axis=0, keepdims=True)
        gsc.append(m1 + m2)
    gs = jnp.concatenate(gsc, axis=0)
    gidx = lax.broadcasted_iota(jnp.int32, (ng, tm), 0)
    gsel = jnp.zeros((ng, tm), jnp.bool_)
    for _ in range(TOPK_GROUPS):
        _, first = _first_max(gs, gidx, ng)
        hit = gidx == first
        gsel = jnp.logical_or(gsel, hit)
        gs = jnp.where(hit, -jnp.inf, gs)
    emask = jnp.concatenate([jnp.broadcast_to(gsel[g:g + 1, :], (per, tm)) for g in range(ng)], axis=0)
    cand = jnp.where(emask, biased, -jnp.inf)
    aidx = lax.broadcasted_iota(jnp.int32, (ne, tm), 0)
    picked, ids = [], []
    for _ in range(TOP_K):
        _, first = _first_max(cand, aidx, ne)
        hit = aidx == first
        picked.append(jnp.sum(jnp.where(hit, scores, 0.0), axis=0, keepdims=True))
        ids.append(first.astype(F32))
        cand = jnp.where(hit, -jnp.inf, cand)
    total = picked[0]
    for pk in picked[1:]:
        total = total + pk
    gates = [pk / total * ROUTED_SCALE for pk in picked]
    pad = jnp.zeros((LANES - 2 * TOP_K, tm), F32)
    o_ref[...] = jnp.concatenate(gates + ids + [pad], axis=0).T


def _router(h, router, router_bias, tm=512):
    n, d = h.shape
    ne = router.shape[0]
    return pl.pallas_call(
        functools.partial(_router_body, tm=tm),
        out_shape=jax.ShapeDtypeStruct((n, LANES), F32),
        grid=(n // tm,),
        in_specs=[pl.BlockSpec((tm, d), lambda i: (i, 0)),
                  pl.BlockSpec((ne, d), lambda i: (0, 0)),
                  pl.BlockSpec((ne, 1), lambda i: (0, 0))],
        out_specs=pl.BlockSpec((tm, LANES), lambda i: (i, 0)),
        compiler_params=_params(("parallel",), 48),
        name="moe_router",
    )(h, router.astype(F32), router_bias.astype(F32).reshape(ne, 1))


def _ffn_tile(x, wg, wu, wd, o_ref, accumulate, n_col=4):
    gte = _dot(x, wg)
    hid = ((gte * _sigmoid(gte)) * _dot(x, wu)).astype(BF16)
    d = o_ref.shape[1]
    cw = d // n_col
    for j in range(n_col):
        y = _dot(hid, wd[:, j * cw:(j + 1) * cw])
        if accumulate:
            o_ref[:, j * cw:(j + 1) * cw] += y
        else:
            o_ref[:, j * cw:(j + 1) * cw] = y


def _shared_body(x_ref, wg_ref, wu_ref, wd_ref, o_ref):
    @pl.when(pl.program_id(1) == 0)
    def _():
        o_ref[...] = jnp.zeros_like(o_ref)

    _ffn_tile(x_ref[...], wg_ref[...], wu_ref[...], wd_ref, o_ref, accumulate=True)


def _shared_ffn(hb, wg, wu, wd, tm=512, th=EXPERT_HIDDEN):
    n, d = hb.shape
    return pl.pallas_call(
        _shared_body,
        out_shape=jax.ShapeDtypeStruct((n, d), F32),
        grid=(n // tm, wg.shape[1] // th),
        in_specs=[pl.BlockSpec((tm, d), lambda i, e: (i, 0)),
                  pl.BlockSpec((d, th), lambda i, e: (0, e)),
                  pl.BlockSpec((d, th), lambda i, e: (0, e)),
                  pl.BlockSpec((th, d), lambda i, e: (e, 0))],
        out_specs=pl.BlockSpec((tm, d), lambda i, e: (i, 0)),
        compiler_params=_params(("parallel", "arbitrary"), 56),
        name="moe_shared",
    )(hb, wg, wu, wd)


MOE_ROW_TILE = 512
MOE_COMBINE_TOKENS = 128


def _start_row_gather(ids_ref, src_hbm, dst, sem, n_rows, first=0):
    for r in range(first, n_rows):
        pltpu.make_async_copy(src_hbm.at[pl.ds(ids_ref[0, 0, r], 1)], dst.at[pl.ds(r, 1)], sem).start(priority=r % 2)


def _wait_row_gather(src_hbm, dst, sem, n_rows):
    pltpu.make_async_copy(src_hbm.at[pl.ds(0, n_rows)], dst, sem).wait()


def _moe_sparse_body(te_ref, nv_ref, tok_ref, tokn_ref, h_hbm, wgf_ref, wuf_ref, wdf_ref, y_ref,
                     xbuf, sem, wg_ref, wu_ref, wd_ref, *, tr, n_col):
    i = pl.program_id(0)
    nv = nv_ref[0]
    slot = i % 2

    new_expert = jnp.logical_or(i == 0, te_ref[i] != te_ref[jnp.maximum(i - 1, 0)])

    @pl.when(jnp.logical_and(i < nv, new_expert))
    def _():
        wg_ref[...] = wgf_ref[0, 0].astype(BF16)
        wu_ref[...] = wuf_ref[0, 0].astype(BF16)
        wd_ref[...] = wdf_ref[0, 0].astype(BF16)

    @pl.when(i == 0)
    def _():
        _start_row_gather(tok_ref, h_hbm, xbuf.at[0], sem.at[0], tr)

    n_batch = 4 + 2 * n_col
    per = tr // n_batch

    def prefetch(b):
        _start_row_gather(tokn_ref, h_hbm, xbuf.at[1 - slot], sem.at[1 - slot], (b + 1) * per, first=b * per)

    @pl.when(i < nv)
    def _():
        _wait_row_gather(h_hbm, xbuf.at[slot], sem.at[slot], tr)
        x_lo, x_hi = _unpack_halves(xbuf[slot])
        x_lo, x_hi = x_lo.astype(BF16), x_hi.astype(BF16)
        half = x_lo.shape[1]
        prefetch(0)
        gte = _dot(x_lo, wg_ref[:half, :])
        prefetch(1)
        gte = gte + _dot(x_hi, wg_ref[half:, :])
        prefetch(2)
        up = _dot(x_lo, wu_ref[:half, :])
        prefetch(3)
        up = up + _dot(x_hi, wu_ref[half:, :])
        hid = ((gte * _sigmoid(gte)) * up).astype(BF16)
        cw = half // n_col
        for j in range(n_col):
            prefetch(4 + 2 * j)
            lo = _dot(hid, wd_ref[:, j * cw:(j + 1) * cw])
            prefetch(5 + 2 * j)
            hi = _dot(hid, wd_ref[:, half + j * cw:half + (j + 1) * cw])
            y_ref[:, j * cw:(j + 1) * cw] = pltpu.pack_elementwise([lo, hi], packed_dtype=BF16)

    @pl.when(i == nv)
    def _():
        _wait_row_gather(h_hbm, xbuf.at[slot], sem.at[slot], tr)

    @pl.when(i >= nv)
    def _():
        y_ref[...] = jnp.zeros_like(y_ref)


def _moe_sparse(h, wg, wu, wd, layer, tok3, tile_expert, n_valid):
    n, d = h.shape
    n_tiles, _, tr = tok3.shape
    dm = 2 * d
    last = lambda i, nv: jnp.minimum(i, nv[0] - 1)
    wspec = lambda w: pl.BlockSpec((1, 1) + w.shape[2:], lambda i, te, nv: (layer, te[last(i, nv)], 0, 0))
    grid_spec = pltpu.PrefetchScalarGridSpec(
        num_scalar_prefetch=2,
        grid=(n_tiles,),
        in_specs=[pl.BlockSpec((1, 1, tr), lambda i, te, nv: (i, 0, 0), memory_space=pltpu.SMEM),
                  pl.BlockSpec((1, 1, tr), lambda i, te, nv: (jnp.minimum(i + 1, n_tiles - 1), 0, 0),
                               memory_space=pltpu.SMEM),
                  pl.BlockSpec(memory_space=pl.ANY),
                  wspec(wg), wspec(wu), wspec(wd)],
        out_specs=pl.BlockSpec((tr, d), lambda i, te, nv: (i, 0)),
        scratch_shapes=[pltpu.VMEM((2, tr, d), jnp.uint32), pltpu.SemaphoreType.DMA((2,)),
                        pltpu.VMEM(wg.shape[2:], BF16), pltpu.VMEM(wu.shape[2:], BF16),
                        pltpu.VMEM(wd.shape[2:], BF16)])
    return pl.pallas_call(
        functools.partial(_moe_sparse_body, tr=tr, n_col=2),
        out_shape=jax.ShapeDtypeStruct((n_tiles * tr, d), jnp.uint32),
        grid_spec=grid_spec,
        compiler_params=_params(("arbitrary",), 60),
        name="moe_sparse",
    )(tile_expert, n_valid, tok3, tok3, h, wg, wu, wd)


def _moe_combine_body(sl_ref, sln_ref, y_hbm, gate_ref, sh_ref, h_ref, g_ref, b_ref, o_ref, ob_ref, buf, sem,
                      *, tm, alpha, n_steps):
    i = pl.program_id(0)
    slot = i % 2
    rows = TOP_K * tm

    @pl.when(i == 0)
    def _():
        _start_row_gather(sl_ref, y_hbm, buf.at[0], sem.at[0], rows)

    @pl.when(i + 1 < n_steps)
    def _():
        _start_row_gather(sln_ref, y_hbm, buf.at[1 - slot], sem.at[1 - slot], rows)

    _wait_row_gather(y_hbm, buf.at[slot], sem.at[slot], rows)
    d = h_ref.shape[1]
    half = d // 2
    acc_lo = sh_ref[:, :half]
    acc_hi = sh_ref[:, half:]
    for k in range(TOP_K):
        lo, hi = _unpack_halves(buf[slot, k * tm:(k + 1) * tm, :])
        gk = gate_ref[:, k:k + 1]
        acc_lo = acc_lo + gk * lo
        acc_hi = acc_hi + gk * hi
    v_lo = alpha * h_ref[:, :half] + acc_lo
    v_hi = alpha * h_ref[:, half:] + acc_hi
    mu = (jnp.sum(v_lo, axis=-1, keepdims=True) + jnp.sum(v_hi, axis=-1, keepdims=True)) / d
    c_lo = v_lo - mu
    c_hi = v_hi - mu
    var = (jnp.sum(c_lo * c_lo, axis=-1, keepdims=True) + jnp.sum(c_hi * c_hi, axis=-1, keepdims=True)) / d
    rs = lax.rsqrt(var + LN_EPS)
    out_lo = c_lo * rs * g_ref[:, :half] + b_ref[:, :half]
    out_hi = c_hi * rs * g_ref[:, half:] + b_ref[:, half:]
    o_ref[:, :half] = out_lo
    o_ref[:, half:] = out_hi
    ob_ref[:, :half] = out_lo.astype(BF16)
    ob_ref[:, half:] = out_hi.astype(BF16)


def _moe_combine(y_sorted, slot3, rout, shared, h, g, b, alpha):
    n, d = h.shape
    n_steps, _, rows = slot3.shape
    tm = rows // TOP_K
    row = lambda i: (i, 0)
    return pl.pallas_call(
        functools.partial(_moe_combine_body, tm=tm, alpha=alpha, n_steps=n_steps),
        out_shape=(jax.ShapeDtypeStruct((n, d), F32), jax.ShapeDtypeStruct((n, d), BF16)),
        grid=(n_steps,),
        in_specs=[pl.BlockSpec((1, 1, rows), lambda i: (i, 0, 0), memory_space=pltpu.SMEM),
                  pl.BlockSpec((1, 1, rows), lambda i: (jnp.minimum(i + 1, n_steps - 1), 0, 0),
                               memory_space=pltpu.SMEM),
                  pl.BlockSpec(memory_space=pl.ANY),
                  pl.BlockSpec((tm, LANES), row),
                  pl.BlockSpec((tm, d), row),
                  pl.BlockSpec((tm, d), row),
                  pl.BlockSpec((1, d), lambda i: (0, 0)),
                  pl.BlockSpec((1, d), lambda i: (0, 0))],
        out_specs=(pl.BlockSpec((tm, d), row), pl.BlockSpec((tm, d), row)),
        scratch_shapes=[pltpu.VMEM((2, rows, d // 2), jnp.uint32), pltpu.SemaphoreType.DMA((2,))],
        compiler_params=_params(("arbitrary",), 48),
        name="moe_combine",
    )(slot3, slot3, y_sorted, rout, shared, h, g.reshape(1, d), b.reshape(1, d))


def _dispatch_plan(rout, n_experts, tr, tm):
    n = rout.shape[0]
    ids = rout[:, TOP_K:2 * TOP_K].astype(jnp.int32)
    onehot = jnp.sum((ids[:, :, None] == jnp.arange(n_experts)[None, None, :]).astype(jnp.int32), axis=1)
    csum = jnp.cumsum(onehot, axis=0)
    counts = csum[-1]
    padded = (counts + tr - 1) // tr * tr
    seg_end = jnp.cumsum(padded)
    seg_start = seg_end - padded
    rank = jnp.take_along_axis(csum - onehot, ids, axis=1)
    slot = seg_start[ids] + rank
    n_tiles = n * TOP_K // tr + n_experts + 1
    tok = jnp.zeros((n_tiles * tr,), jnp.int32).at[slot.reshape(-1)].set(
        jnp.repeat(jnp.arange(n, dtype=jnp.int32), TOP_K), unique_indices=True)
    tile_start = jnp.arange(n_tiles, dtype=jnp.int32) * tr
    tile_expert = jnp.minimum(jnp.sum((seg_end[None, :] <= tile_start[:, None]).astype(jnp.int32), axis=1),
                              n_experts - 1)
    n_valid = (seg_end[-1:] // tr).astype(jnp.int32)
    slot3 = slot.reshape(n // tm, tm, TOP_K).transpose(0, 2, 1).reshape(n // tm, 1, TOP_K * tm)
    return tok.reshape(n_tiles, 1, tr), tile_expert, n_valid, slot3.astype(jnp.int32)


def kernel(x, mem, ev_w_in, ev_pool_w, ev_pool_scale, ev_sinks, ev_w_out, rel_bias, od_w_in, hg_lb_logits, od_hg_norm, od_a_re, od_a_im, od_log_step, od_b_re, od_b_im, od_c_re, od_c_im, od_d_skip, od_w_glu, od_b_glu, od_w_out, xa_wq, xa_wk, xa_wv, xa_wo, moe_router, moe_bias, moe_w_gate, moe_w_up, moe_w_down, sh_w_gate, sh_w_up, sh_w_down, ln_g, ln_b):
    bsz, seq, d = x.shape
    n = bsz * seq
    depth = ln_g.shape[0]
    alpha = (2 * depth) ** 0.25
    memb = mem.astype(BF16).reshape(bsz * mem.shape[1], d)

    h = x.astype(F32).reshape(n, d)
    hb = h.astype(BF16)
    bias_tab = _bias_table(rel_bias)

    for l in range(depth):
        j = l // 2
        if l % 2 == 0:
            pool_w = ev_pool_w[j]
            pool_width = pool_w.shape[0] * pool_w.shape[1]
            n_q = ev_sinks.shape[1]
            n_kv = n_q // SWA_GQ
            z = _mm(hb, ev_w_in[j].astype(BF16), F32)
            z3 = z.reshape(bsz, seq, z.shape[1])
            y_a = _pool(z3, pool_w.astype(BF16), ev_pool_scale[j].astype(F32))
            y_b = _swa(z3, ev_sinks[j], bias_tab, pool_width, n_q, n_kv)
            mix = _mm2(y_a.reshape(n, -1), y_b.reshape(n, -1), ev_w_out[j].astype(BF16), F32)
        else:
            n_heads = hg_lb_logits.shape[1] // HG_DIM
            hgw = n_heads * HG_DIM
            ng, nst = od_a_re.shape[1], od_a_re.shape[2]
            s5w = ng * S5_GROUP
            z = _mm(hb, od_w_in[j].astype(BF16), F32)
            z3 = z.reshape(bsz, seq, z.shape[1])
            y_c = _hgrn(z3, hg_lb_logits, od_hg_norm[j], l, n_heads)
            m_mat, p_mat, q_mat, t_re, t_im = _s5_prep(od_a_re[j], od_a_im[j], od_log_step[j], od_b_re[j],
                                                        od_b_im[j], od_c_re[j], od_c_im[j])
            nck = seq // S5_CHUNK
            u = z3[:, :, 4 * hgw:].astype(BF16).reshape(bsz, nck, S5_CHUNK, ng, S5_GROUP)
            ug = u.transpose(3, 0, 1, 2, 4).reshape(ng, bsz, nck, S5_CHUNK * S5_GROUP)
            yg = _s5_scan(ug, m_mat, p_mat, q_mat, t_re, t_im)
            y_s = yg.reshape(ng, bsz, nck, S5_CHUNK, S5_GROUP).transpose(1, 2, 3, 0, 4).reshape(n, s5w)
            y_d = _glu(y_s, z, (4 * hgw) // s5w, od_d_skip[j], od_w_glu[j], od_b_glu[j])
            mix = _mm2(y_c.reshape(n, hgw), y_d, od_w_out[j].astype(BF16), F32)
        h, hb = _add_ln(h, mix, ln_g[l, 0], ln_b[l, 0], alpha)

        xw = xa_wq.shape[2]
        kmem = _mm(memb, xa_wk[l].astype(BF16), BF16).reshape(bsz, -1, xw)
        vmem = _mm(memb, xa_wv[l].astype(BF16), BF16).reshape(bsz, -1, xw)
        h, hb, hp = _xattn(h, hb, kmem, vmem, xa_wq[l].astype(BF16), xa_wo[l].astype(BF16),
                           ln_g[l, 1], ln_b[l, 1], alpha, seq)

        n_exp = moe_router.shape[1]
        rout = _router(h, moe_router[l], moe_bias[l])
        tok3, tile_expert, n_valid, slot3 = _dispatch_plan(rout, n_exp, MOE_ROW_TILE, MOE_COMBINE_TOKENS)
        shared = _shared_ffn(hb, sh_w_gate[l].astype(BF16), sh_w_up[l].astype(BF16), sh_w_down[l].astype(BF16))
        y_sorted = _moe_sparse(hp, moe_w_gate.astype(F32), moe_w_up.astype(F32), moe_w_down.astype(F32), l,
                               tok3, tile_expert, n_valid)
        h, hb = _moe_combine(y_sorted, slot3, rout, shared, h, ln_g[l, 2], ln_b[l, 2], alpha)

    return h.reshape(bsz, seq, d)
```

```python
import functools
import math

import jax
import jax.numpy as jnp
from jax import lax
from jax.experimental import pallas as pl
from jax.experimental.pallas import tpu as pltpu

F32 = jnp.float32
BF16 = jnp.bfloat16

V7X_VMEM_BYTES = 64 * 1024 * 1024
LANES = 128
SUBLANES = 8

POOL_WINDOWS = (2, 4, 8, 16)
SWA_HEAD_DIM = 64
SWA_GQ = 8
SWA_BLOCK = 128
REL_BUCKETS = 32
REL_MAX_DIST = 128
HG_DIM = 128
HG_CHUNK = 64
RMS_EPS = 1e-6
S5_GROUP = 16
S5_STATE = 64
S5_CHUNK = 16
XA_HEADS = 4
XA_HEAD_DIM = 128
N_EXPERTS = 64
TOP_K = 8
N_EXPERT_GROUPS = 8
TOPK_GROUPS = 4
EXPERT_HIDDEN = 256
ROUTED_SCALE = 2.5
LN_EPS = 1e-5


def _params(sem, vmem_mb):
    return pltpu.CompilerParams(dimension_semantics=sem, vmem_limit_bytes=vmem_mb * 1024 * 1024)


def _nt(a, b):
    return lax.dot_general(a, b, (((1,), (1,)), ((), ())), preferred_element_type=F32)


def _tn(a, b):
    return lax.dot_general(a, b, (((0,), (0,)), ((), ())), preferred_element_type=F32)


def _dot(a, b):
    return jnp.dot(a, b, preferred_element_type=F32)


def _sigmoid(x):
    return 0.5 * (jnp.tanh(0.5 * x) + 1.0)


def _mm_body(x_ref, w_ref, o_ref):
    o_ref[...] = _dot(x_ref[...], w_ref[...]).astype(o_ref.dtype)


def _mm(x, w, out_dtype, tm=1024, tn=512):
    m, k = x.shape
    n = w.shape[1]
    tm = min(tm, m)
    tn = min(tn, n)
    return pl.pallas_call(
        _mm_body,
        out_shape=jax.ShapeDtypeStruct((m, n), out_dtype),
        grid=(m // tm, n // tn),
        in_specs=[pl.BlockSpec((tm, k), lambda i, j: (i, 0)),
                  pl.BlockSpec((k, tn), lambda i, j: (0, j))],
        out_specs=pl.BlockSpec((tm, tn), lambda i, j: (i, j)),
        compiler_params=_params(("parallel", "parallel"), 48),
        name="mm",
    )(x, w)


def _mm2_body(a_ref, b_ref, w_ref, o_ref):
    ka = a_ref.shape[1]
    o_ref[...] = (_dot(a_ref[...], w_ref[:ka, :]) + _dot(b_ref[...], w_ref[ka:, :])).astype(o_ref.dtype)


def _mm2(a, b, w, out_dtype, tm=1024, tn=512):
    m, ka = a.shape
    kb = b.shape[1]
    n = w.shape[1]
    return pl.pallas_call(
        _mm2_body,
        out_shape=jax.ShapeDtypeStruct((m, n), out_dtype),
        grid=(m // tm, n // tn),
        in_specs=[pl.BlockSpec((tm, ka), lambda i, j: (i, 0)),
                  pl.BlockSpec((tm, kb), lambda i, j: (i, 0)),
                  pl.BlockSpec((ka + kb, tn), lambda i, j: (0, j))],
        out_specs=pl.BlockSpec((tm, tn), lambda i, j: (i, j)),
        compiler_params=_params(("parallel", "parallel"), 48),
        name="mm2",
    )(a, b, w)


def _add_ln_body(h_ref, y_ref, g_ref, b_ref, o_ref, ob_ref, *, alpha):
    v = alpha * h_ref[...] + y_ref[...].astype(F32)
    mu = jnp.mean(v, axis=-1, keepdims=True)
    c = v - mu
    var = jnp.mean(c * c, axis=-1, keepdims=True)
    out = c * lax.rsqrt(var + LN_EPS) * g_ref[...] + b_ref[...]
    o_ref[...] = out
    ob_ref[...] = out.astype(BF16)


def _add_ln(h, y, g, b, alpha, tm=256):
    n, d = h.shape
    return pl.pallas_call(
        functools.partial(_add_ln_body, alpha=alpha),
        out_shape=(jax.ShapeDtypeStruct((n, d), F32), jax.ShapeDtypeStruct((n, d), BF16)),
        grid=(n // tm,),
        in_specs=[pl.BlockSpec((tm, d), lambda i: (i, 0)),
                  pl.BlockSpec((tm, d), lambda i: (i, 0)),
                  pl.BlockSpec((1, d), lambda i: (0, 0)),
                  pl.BlockSpec((1, d), lambda i: (0, 0))],
        out_specs=(pl.BlockSpec((tm, d), lambda i: (i, 0)),
                   pl.BlockSpec((tm, d), lambda i: (i, 0))),
        compiler_params=_params(("parallel",), 48),
        name="add_ln",
    )(h, y, g.reshape(1, d), b.reshape(1, d))


def _pool_body(u_ref, prev_ref, w_ref, sc_ref, o_ref, *, ts, gw, halo):
    i = pl.program_id(1)
    row = lax.broadcasted_iota(jnp.int32, (ts, gw), 0)
    pos1 = (row + i * ts + 1).astype(F32)
    for g, win in enumerate(POOL_WINDOWS):
        u = u_ref[0, :, g * gw:(g + 1) * gw]
        prev = prev_ref[0, :, g * gw:(g + 1) * gw]
        prev = jnp.where(i == 0, 0.0, prev)
        s = jnp.concatenate([prev, u], axis=0)
        step = 1
        while step < win:
            s = s + pltpu.roll(s, step, axis=0)
            step *= 2
        wsum = s[halo:, :]
        pooled = wsum / jnp.minimum(pos1, float(win)) - u
        y = _dot(pooled.astype(BF16), w_ref[g])
        o_ref[0, :, g * gw:(g + 1) * gw] = (y * sc_ref[:, g * gw:(g + 1) * gw]).astype(o_ref.dtype)


def _pool(z3, pool_w, pool_scale, ts=512):
    bsz, s, _ = z3.shape
    ng, gw, _ = pool_w.shape
    width = ng * gw
    halo = max(POOL_WINDOWS)
    return pl.pallas_call(
        functools.partial(_pool_body, ts=ts, gw=gw, halo=halo),
        out_shape=jax.ShapeDtypeStruct((bsz, s, width), BF16),
        grid=(bsz, s // ts),
        in_specs=[pl.BlockSpec((1, ts, width), lambda b, i: (b, i, 0)),
                  pl.BlockSpec((1, halo, width), lambda b, i: (b, jnp.maximum(i * (ts // halo) - 1, 0), 0)),
                  pl.BlockSpec((ng, gw, gw), lambda b, i: (0, 0, 0)),
                  pl.BlockSpec((1, width), lambda b, i: (0, 0))],
        out_specs=pl.BlockSpec((1, ts, width), lambda b, i: (b, i, 0)),
        compiler_params=_params(("parallel", "parallel"), 48),
        name="pool",
    )(z3, z3, pool_w, pool_scale.reshape(1, width))


def _t5_bucket_table():
    qi = jnp.arange(SWA_BLOCK)[:, None]
    kj = jnp.arange(2 * SWA_BLOCK)[None, :]
    dist = qi + SWA_BLOCK - kj
    band = (dist >= 0) & (dist < SWA_BLOCK)
    d = jnp.maximum(dist, 0)
    exact = REL_BUCKETS // 2
    far = exact + (jnp.log(jnp.maximum(d, exact).astype(F32) / exact)
                   / math.log(REL_MAX_DIST / exact) * (REL_BUCKETS - exact)).astype(jnp.int32)
    bucket = jnp.where(d < exact, d, jnp.minimum(far, REL_BUCKETS - 1))
    return jnp.where(band, bucket, -1).astype(jnp.int32)


def _bias_body(rb_ref, bucket_ref, o_ref):
    h = pl.program_id(0)
    bucket = bucket_ref[...]
    acc = jnp.full(bucket.shape, -jnp.inf, F32)
    for b in range(REL_BUCKETS):
        acc = jnp.where(bucket == b, rb_ref[b, h], acc)
    o_ref[0] = acc


def _bias_table(rel_bias):
    nb, nh = rel_bias.shape
    bucket = _t5_bucket_table()
    return pl.pallas_call(
        _bias_body,
        out_shape=jax.ShapeDtypeStruct((nh, SWA_BLOCK, 2 * SWA_BLOCK), F32),
        grid=(nh,),
        in_specs=[pl.BlockSpec(memory_space=pltpu.SMEM),
                  pl.BlockSpec((SWA_BLOCK, 2 * SWA_BLOCK), lambda h: (0, 0))],
        out_specs=pl.BlockSpec((1, SWA_BLOCK, 2 * SWA_BLOCK), lambda h: (h, 0, 0)),
        compiler_params=_params(("parallel",), 32),
        name="swa_bias",
    )(rel_bias.astype(F32), bucket)


def _swa_body(sink_ref, q_ref, kp_ref, kc_ref, vp_ref, vc_ref, bias_ref, o_ref, *, n_kv):
    n = pl.program_id(1)
    dh = SWA_HEAD_DIM
    col = lax.broadcasted_iota(jnp.int32, (SWA_BLOCK, 2 * SWA_BLOCK), 1)
    no_prev = jnp.logical_and(n == 0, col < SWA_BLOCK)
    scale = dh ** -0.5
    for kv in range(n_kv):
        sl = slice(kv * dh, (kv + 1) * dh)
        k2 = jnp.concatenate([kp_ref[0, :, sl], kc_ref[0, :, sl]], axis=0).astype(BF16)
        v2 = jnp.concatenate([vp_ref[0, :, sl], vc_ref[0, :, sl]], axis=0).astype(BF16)
        for g in range(SWA_GQ):
            h = kv * SWA_GQ + g
            qh = q_ref[0, :, h * dh:(h + 1) * dh].astype(BF16)
            logits = _nt(qh, k2) * scale + bias_ref[h]
            logits = jnp.where(no_prev, -jnp.inf, logits)
            sink = sink_ref[h]
            m = jnp.maximum(jnp.max(logits, axis=-1, keepdims=True), sink)
            p = jnp.exp(logits - m)
            p = p / (jnp.sum(p, axis=-1, keepdims=True) + jnp.exp(sink - m))
            o_ref[0, :, h * dh:(h + 1) * dh] = _dot(p.astype(BF16), v2).astype(o_ref.dtype)


def _swa(z3, sinks, bias_tab, q_off, n_q, n_kv):
    bsz, s, _ = z3.shape
    dh = SWA_HEAD_DIM
    qw, kvw = n_q * dh, n_kv * dh
    qb = q_off // qw
    kb = (q_off + qw) // kvw
    vb = kb + 1
    blk = SWA_BLOCK
    prev = lambda n: jnp.maximum(n - 1, 0)
    return pl.pallas_call(
        functools.partial(_swa_body, n_kv=n_kv),
        out_shape=jax.ShapeDtypeStruct((bsz, s, qw), BF16),
        grid=(bsz, s // blk),
        in_specs=[pl.BlockSpec(memory_space=pltpu.SMEM),
                  pl.BlockSpec((1, blk, qw), lambda b, n: (b, n, qb)),
                  pl.BlockSpec((1, blk, kvw), lambda b, n: (b, prev(n), kb)),
                  pl.BlockSpec((1, blk, kvw), lambda b, n: (b, n, kb)),
                  pl.BlockSpec((1, blk, kvw), lambda b, n: (b, prev(n), vb)),
                  pl.BlockSpec((1, blk, kvw), lambda b, n: (b, n, vb)),
                  pl.BlockSpec((n_q, blk, 2 * blk), lambda b, n: (0, 0, 0))],
        out_specs=pl.BlockSpec((1, blk, qw), lambda b, n: (b, n, 0)),
        compiler_params=_params(("parallel", "parallel"), 48),
        name="swa",
    )(sinks.astype(F32), z3, z3, z3, z3, z3, bias_tab)


def _hgrn_body(q_ref, f_ref, i_ref, g_ref, lbl_ref, nw_ref, o_ref, st_ref, *, n_chunk, layer):
    @pl.when(pl.program_id(2) == 0)
    def _():
        st_ref[...] = jnp.zeros_like(st_ref)

    lg = lbl_ref[:, 0, 0, :]
    e = jnp.exp(lg - jnp.max(lg, axis=0, keepdims=True))
    p = e / jnp.sum(e, axis=0, keepdims=True)
    lb = jnp.sum(p[:layer + 1], axis=0, keepdims=True) - p[0:1]
    nw = nw_ref[...]
    c = HG_CHUNK
    r = lax.broadcasted_iota(jnp.int32, (c, c), 0)
    cc = lax.broadcasted_iota(jnp.int32, (c, c), 1)
    causal = r >= cc
    tri = jnp.where(causal, 1.0, 0.0).astype(BF16)

    d = HG_DIM
    chunks = lambda t: [t[ci * c:(ci + 1) * c] for ci in range(n_chunk)]
    q = q_ref[0]
    iv = chunks(i_ref[0].astype(BF16))
    g = g_ref[0]
    qf = q * _sigmoid(q)
    forget = lb + (1.0 - lb) * _sigmoid(f_ref[0])
    log_f = jnp.log(forget)
    k_in = 1.0 - forget
    hi = log_f.astype(BF16)
    r1 = log_f - hi.astype(F32)
    mid = r1.astype(BF16)
    lo = (r1 - mid.astype(F32)).astype(BF16)
    cum = jnp.concatenate([_dot(tri, a) + _dot(tri, b) + _dot(tri, e3)
                           for a, b, e3 in zip(chunks(hi), chunks(mid), chunks(lo))], axis=0)
    cum3 = cum.reshape(n_chunk, c, d)
    ref = cum3[:, c // 2:c // 2 + 1, :]
    total = cum3[:, c - 1:c, :]
    qa = qf.reshape(n_chunk, c, d) * jnp.exp(cum3 - ref)
    kb = k_in.reshape(n_chunk, c, d) * jnp.exp(ref - cum3)
    qe = chunks((qa * jnp.exp(ref)).astype(BF16).reshape(n_chunk * c, d))
    ku = chunks((kb * jnp.exp(total - ref)).astype(BF16).reshape(n_chunk * c, d))
    qa = chunks(qa.astype(BF16).reshape(n_chunk * c, d))
    kb = chunks(kb.astype(BF16).reshape(n_chunk * c, d))
    decay = jnp.exp(total)
    st = st_ref[...]
    outs = []
    for ci in range(n_chunk):
        scores = jnp.where(causal, _nt(qa[ci], kb[ci]), 0.0)
        o_intra = _dot(scores.astype(BF16), iv[ci])
        o_inter = _nt(qe[ci], st.astype(BF16))
        st = st * decay[ci] + _tn(iv[ci], ku[ci])
        outs.append(o_intra + o_inter)
    st_ref[...] = st
    o = jnp.concatenate(outs, axis=0)
    o = o * lax.rsqrt(jnp.mean(o * o, axis=-1, keepdims=True) + RMS_EPS) * nw
    o_ref[0] = (o * (g * _sigmoid(g))).astype(o_ref.dtype)


def _hgrn(z3, lb_logits, norm_w, layer, n_heads, ts=512):
    bsz, s, _ = z3.shape
    depth = lb_logits.shape[0]
    d = HG_DIM
    spec = lambda off: pl.BlockSpec((1, ts, d), lambda b, h, t: (b, t, off + h))
    return pl.pallas_call(
        functools.partial(_hgrn_body, n_chunk=ts // HG_CHUNK, layer=layer),
        out_shape=jax.ShapeDtypeStruct((bsz, s, n_heads * d), BF16),
        grid=(bsz, n_heads, s // ts),
        in_specs=[spec(0), spec(n_heads), spec(2 * n_heads), spec(3 * n_heads),
                  pl.BlockSpec((depth, 1, 1, d), lambda b, h, t: (0, h, 0, 0)),
                  pl.BlockSpec((1, d), lambda b, h, t: (0, 0))],
        out_specs=pl.BlockSpec((1, ts, d), lambda b, h, t: (b, t, h)),
        scratch_shapes=[pltpu.VMEM((d, d), F32)],
        compiler_params=_params(("parallel", "parallel", "arbitrary"), 32),
        name="hgrn2",
    )(z3, z3, z3, z3, lb_logits.astype(F32).reshape(depth, n_heads, 1, d), norm_w.astype(F32).reshape(1, d))


def _cpow(pw, lam_re, lam_im):
    mag = jnp.exp(pw * lam_re)
    ang = pw * lam_im
    return mag * jnp.cos(ang), mag * jnp.sin(ang)


def _s5_prep_body(ls_ref, ar_ref, ai_ref, arc_ref, aic_ref, br_ref, bi_ref, cr_ref, ci_ref,
                  m_ref, p_ref, q_ref, tr_ref, ti_ref):
    nst = S5_STATE
    lc = S5_CHUNK
    gc = S5_GROUP
    w = lc * gc
    dt = jnp.exp(ls_ref[0])
    a_re, a_im = ar_ref[0], ai_ref[0]
    lam_re, lam_im = dt * a_re, dt * a_im
    ab_re, ab_im = _cpow(1.0, lam_re, lam_im)
    den = a_re * a_re + a_im * a_im
    co_re = ((ab_re - 1.0) * a_re + ab_im * a_im) / den
    co_im = (ab_im * a_re - (ab_re - 1.0) * a_im) / den
    b_re, b_im = br_ref[0], bi_ref[0]
    bb_re = co_re * b_re - co_im * b_im
    bb_im = co_re * b_im + co_im * b_re
    lane = lax.broadcasted_iota(jnp.int32, (w, 2 * nst), 1)
    rowi = lax.broadcasted_iota(jnp.int32, (w, 2 * nst), 0)

    pw = (lc - 1 - rowi // gc).astype(F32)
    ap_re, ap_im = _cpow(pw, lam_re, lam_im)
    tb_re = jnp.concatenate([bb_re] * lc, axis=0)
    tb_im = jnp.concatenate([bb_im] * lc, axis=0)
    p_ref[0] = jnp.where(lane < nst, ap_re * tb_re - ap_im * tb_im,
                         ap_re * tb_im + ap_im * tb_re).astype(p_ref.dtype)

    lam_re_c, lam_im_c = dt * arc_ref[0], dt * aic_ref[0]
    c_re, c_im = cr_ref[0], ci_ref[0]
    lane_q = lax.broadcasted_iota(jnp.int32, (2 * nst, w), 1)
    row_q = lax.broadcasted_iota(jnp.int32, (2 * nst, w), 0)
    tq = (lane_q // gc).astype(F32)

    def c_times_apow(pwq):
        pr, pi = _cpow(pwq, lam_re_c, lam_im_c)
        return jnp.where(row_q < nst, c_re * pr - c_im * pi, -(c_re * pi + c_im * pr))

    q_ref[0] = c_times_apow(tq + 1.0).astype(q_ref.dtype)
    wmat = c_times_apow(tq)

    bbt = jnp.where(lax.broadcasted_iota(jnp.int32, (gc, 2 * nst), 1) < nst, bb_re, bb_im)
    r = jnp.dot(bbt, wmat, preferred_element_type=F32, precision=lax.Precision.HIGHEST)
    lane_r = lax.broadcasted_iota(jnp.int32, (gc, w), 1)
    for t in range(lc):
        blk = r if t == 0 else jnp.where(lane_r >= t * gc, pltpu.roll(r, t * gc, axis=1), 0.0)
        m_ref[0, t * gc:(t + 1) * gc, :] = blk.astype(m_ref.dtype)

    kk = lax.broadcasted_iota(jnp.int32, (16, 2 * nst), 0)
    pws = (lc * jnp.left_shift(1, jnp.minimum(kk, 12))).astype(F32)
    sr, si = _cpow(pws, lam_re, lam_im)
    tr_ref[0] = sr
    ti_ref[0] = jnp.where(lax.broadcasted_iota(jnp.int32, (16, 2 * nst), 1) < nst, -si, si)


def _s5_prep(a_re, a_im, log_step, b_re, b_im, c_re, c_im):
    ng, nst = a_re.shape
    gc, lc = S5_GROUP, S5_CHUNK
    w = lc * gc
    dup = lambda t: jnp.concatenate([t, t], axis=-1)
    ar = dup(a_re.astype(F32))
    ai = dup(a_im.astype(F32))
    brt = dup(jnp.swapaxes(b_re.astype(F32), 1, 2))
    bit = dup(jnp.swapaxes(b_im.astype(F32), 1, 2))
    tile_c = lambda t: jnp.tile(jnp.concatenate([jnp.swapaxes(t.astype(F32), 1, 2)] * 2, axis=1), (1, 1, lc))
    crt, cit = tile_c(c_re), tile_c(c_im)
    g3 = lambda blk: pl.BlockSpec((1,) + blk, lambda g: (g, 0, 0))
    return pl.pallas_call(
        _s5_prep_body,
        out_shape=(jax.ShapeDtypeStruct((ng, w, w), BF16),
                   jax.ShapeDtypeStruct((ng, w, 2 * nst), BF16),
                   jax.ShapeDtypeStruct((ng, 2 * nst, w), BF16),
                   jax.ShapeDtypeStruct((ng, 16, 2 * nst), F32),
                   jax.ShapeDtypeStruct((ng, 16, 2 * nst), F32)),
        grid=(ng,),
        in_specs=[g3((1, 1)), g3((1, 2 * nst)), g3((1, 2 * nst)), g3((2 * nst, 1)), g3((2 * nst, 1)),
                  g3((gc, 2 * nst)), g3((gc, 2 * nst)), g3((2 * nst, w)), g3((2 * nst, w))],
        out_specs=(g3((w, w)), g3((w, 2 * nst)), g3((2 * nst, w)), g3((16, 2 * nst)), g3((16, 2 * nst))),
        compiler_params=_params(("parallel",), 32),
        name="s5_prep",
    )(log_step.astype(F32).reshape(ng, 1, 1), ar.reshape(ng, 1, 2 * nst), ai.reshape(ng, 1, 2 * nst),
      ar.reshape(ng, 2 * nst, 1), ai.reshape(ng, 2 * nst, 1), brt, bit, crt, cit)


def _s5_scan_body(u_ref, m_ref, p_ref, q_ref, tr_ref, ti_ref, y_ref, *, n_steps):
    nst = S5_STATE
    u = u_ref[0, 0]
    x = _dot(u, p_ref[0])
    row = lax.broadcasted_iota(jnp.int32, x.shape, 0)
    tr = tr_ref[0]
    ti = ti_ref[0]
    for k in range(n_steps):
        s = 1 << k
        sh = jnp.where(row >= s, pltpu.roll(x, s, axis=0), 0.0)
        x = x + tr[k:k + 1, :] * sh + ti[k:k + 1, :] * pltpu.roll(sh, nst, axis=1)
    x_in = jnp.where(row >= 1, pltpu.roll(x, 1, axis=0), 0.0)
    y_ref[0, 0] = _dot(u, m_ref[0]) + _dot(x_in.astype(BF16), q_ref[0])


def _s5_scan(ug, m, p, q, tr, ti):
    ng, bsz, nck, w = ug.shape
    nst2 = p.shape[-1]
    n_steps = int(math.log2(nck))
    assert (1 << n_steps) == nck
    g3 = lambda blk: pl.BlockSpec((1,) + blk, lambda g, b: (g, 0, 0))
    return pl.pallas_call(
        functools.partial(_s5_scan_body, n_steps=n_steps),
        out_shape=jax.ShapeDtypeStruct((ng, bsz, nck, w), F32),
        grid=(ng, bsz),
        in_specs=[pl.BlockSpec((1, 1, nck, w), lambda g, b: (g, b, 0, 0)),
                  g3((w, w)), g3((w, nst2)), g3((nst2, w)), g3((16, nst2)), g3((16, nst2))],
        out_specs=pl.BlockSpec((1, 1, nck, w), lambda g, b: (g, b, 0, 0)),
        compiler_params=_params(("parallel", "parallel"), 32),
        name="s5_scan",
    )(ug, m, p, q, tr, ti)


def _glu_body(y_ref, u_ref, d_ref, w_ref, b_ref, o_ref):
    yy = y_ref[...] + d_ref[...] * u_ref[...]
    z = 0.5 * yy * (1.0 + jnp.tanh(math.sqrt(2.0 / math.pi) * (yy + 0.044715 * (yy * yy * yy))))
    gate = _sigmoid(_dot(z.astype(BF16), w_ref[...]) + b_ref[...])
    o_ref[...] = (z * gate).astype(o_ref.dtype)


def _glu(y, z2, u_blk, d_skip, w_glu, b_glu, tm=512):
    n, wd = y.shape
    return pl.pallas_call(
        _glu_body,
        out_shape=jax.ShapeDtypeStruct((n, wd), BF16),
        grid=(n // tm,),
        in_specs=[pl.BlockSpec((tm, wd), lambda i: (i, 0)),
                  pl.BlockSpec((tm, wd), lambda i: (i, u_blk)),
                  pl.BlockSpec((1, wd), lambda i: (0, 0)),
                  pl.BlockSpec((wd, wd), lambda i: (0, 0)),
                  pl.BlockSpec((1, wd), lambda i: (0, 0))],
        out_specs=pl.BlockSpec((tm, wd), lambda i: (i, 0)),
        compiler_params=_params(("parallel",), 32),
        name="s5_glu",
    )(y, z2, d_skip.astype(F32).reshape(1, wd), w_glu.astype(BF16), b_glu.astype(F32).reshape(1, wd))


def _pack_halves(v):
    c = v.shape[1] // 2
    return pltpu.pack_elementwise([v[:, :c], v[:, c:]], packed_dtype=BF16)


def _unpack_halves(w):
    return tuple(pltpu.unpack_elementwise(w, index=i, packed_dtype=BF16, unpacked_dtype=F32) for i in (0, 1))


def _xattn_body(hb_ref, h_ref, wq_ref, k_ref, v_ref, wo_ref, g_ref, b_ref, o_ref, ob_ref, op_ref, *, alpha):
    d = XA_HEAD_DIM
    q = _dot(hb_ref[...], wq_ref[...]).astype(BF16)
    outs = []
    for hd in range(XA_HEADS):
        sl = slice(hd * d, (hd + 1) * d)
        logits = _nt(q[:, sl], k_ref[0, :, sl]) * (d ** -0.5)
        m = jnp.max(logits, axis=-1, keepdims=True)
        p = jnp.exp(logits - m)
        p = p / jnp.sum(p, axis=-1, keepdims=True)
        outs.append(_dot(p.astype(BF16), v_ref[0, :, sl]))
    o = jnp.concatenate(outs, axis=-1).astype(BF16)
    v = alpha * h_ref[...] + _dot(o, wo_ref[...])
    mu = jnp.mean(v, axis=-1, keepdims=True)
    c = v - mu
    var = jnp.mean(c * c, axis=-1, keepdims=True)
    out = c * lax.rsqrt(var + LN_EPS) * g_ref[...] + b_ref[...]
    o_ref[...] = out
    ob_ref[...] = out.astype(BF16)
    op_ref[...] = _pack_halves(out)


def _xattn(h, hb, kmem, vmem, wq, wo, g, b, alpha, seq, tm=256):
    n, d = h.shape
    xw = wq.shape[1]
    ml = kmem.shape[1]
    per_b = seq // tm
    return pl.pallas_call(
        functools.partial(_xattn_body, alpha=alpha),
        out_shape=(jax.ShapeDtypeStruct((n, d), F32), jax.ShapeDtypeStruct((n, d), BF16),
                   jax.ShapeDtypeStruct((n, d // 2), jnp.uint32)),
        grid=(n // tm,),
        in_specs=[pl.BlockSpec((tm, d), lambda i: (i, 0)),
                  pl.BlockSpec((tm, d), lambda i: (i, 0)),
                  pl.BlockSpec((d, xw), lambda i: (0, 0)),
                  pl.BlockSpec((1, ml, xw), lambda i: (i // per_b, 0, 0)),
                  pl.BlockSpec((1, ml, xw), lambda i: (i // per_b, 0, 0)),
                  pl.BlockSpec((xw, d), lambda i: (0, 0)),
                  pl.BlockSpec((1, d), lambda i: (0, 0)),
                  pl.BlockSpec((1, d), lambda i: (0, 0))],
        out_specs=(pl.BlockSpec((tm, d), lambda i: (i, 0)),
                   pl.BlockSpec((tm, d), lambda i: (i, 0)),
                   pl.BlockSpec((tm, d // 2), lambda i: (i, 0))),
        compiler_params=_params(("parallel",), 56),
        name="xattn",
    )(hb, h, wq, kmem, vmem, wo, g.reshape(1, d), b.reshape(1, d))


def _first_max(v, idx, n):
    m = jnp.max(v, axis=0, keepdims=True)
    first = jnp.min(jnp.where(v == m, idx, n), axis=0, keepdims=True)
    return m, first


def _router_body(x_ref, r_ref, rb_ref, o_ref, *, tm):
    ne, ng = N_EXPERTS, N_EXPERT_GROUPS
    per = ne // ng
    logits = lax.dot_general(r_ref[...], x_ref[...], (((1,), (1,)), ((), ())),
                             preferred_element_type=F32, precision=lax.Precision.HIGHEST)
    scores = _sigmoid(logits)
    biased = scores + rb_ref[...]
    gsc = []
    eidx = lax.broadcasted_iota(jnp.int32, (per, tm), 0)
    for g in range(ng):
        v = biased[g * per:(g + 1) * per, :]
        m1, i1 = _first_max(v, eidx, per)
        m2 = jnp.max(jnp.where(eidx == i1, -jnp.inf, v), axis=0, keepdims=True)
        gsc.append(m1 + m2)
    gs = jnp.concatenate(gsc, axis=0)
    gidx = lax.broadcasted_iota(jnp.int32, (ng, tm), 0)
    gsel = jnp.zeros((ng, tm), jnp.bool_)
    for _ in range(TOPK_GROUPS):
        _, first = _first_max(gs, gidx, ng)
        hit = gidx == first
        gsel = jnp.logical_or(gsel, hit)
        gs = jnp.where(hit, -jnp.inf, gs)
    emask = jnp.concatenate([jnp.broadcast_to(gsel[g:g + 1, :], (per, tm)) for g in range(ng)], axis=0)
    cand = jnp.where(emask, biased, -jnp.inf)
    aidx = lax.broadcasted_iota(jnp.int32, (ne, tm), 0)
    picked, ids = [], []
    for _ in range(TOP_K):
        _, first = _first_max(cand, aidx, ne)
        hit = aidx == first
        picked.append(jnp.sum(jnp.where(hit, scores, 0.0), axis=0, keepdims=True))
        ids.append(first.astype(F32))
        cand = jnp.where(hit, -jnp.inf, cand)
    total = picked[0]
    for pk in picked[1:]:
        total = total + pk
    gates = [pk / total * ROUTED_SCALE for pk in picked]
    pad = jnp.zeros((LANES - 2 * TOP_K, tm), F32)
    o_ref[...] = jnp.concatenate(gates + ids + [pad], axis=0).T


def _router(h, router, router_bias, tm=512):
    n, d = h.shape
    ne = router.shape[0]
    return pl.pallas_call(
        functools.partial(_router_body, tm=tm),
        out_shape=jax.ShapeDtypeStruct((n, LANES), F32),
        grid=(n // tm,),
        in_specs=[pl.BlockSpec((tm, d), lambda i: (i, 0)),
                  pl.BlockSpec((ne, d), lambda i: (0, 0)),
                  pl.BlockSpec((ne, 1), lambda i: (0, 0))],
        out_specs=pl.BlockSpec((tm, LANES), lambda i: (i, 0)),
        compiler_params=_params(("parallel",), 48),
        name="moe_router",
    )(h, router.astype(F32), router_bias.astype(F32).reshape(ne, 1))


def _ffn_tile(x, wg, wu, wd, o_ref, accumulate, n_col=4):
    gte = _dot(x, wg)
    hid = ((gte * _sigmoid(gte)) * _dot(x, wu)).astype(BF16)
    d = o_ref.shape[1]
    cw = d // n_col
    for j in range(n_col):
        y = _dot(hid, wd[:, j * cw:(j + 1) * cw])
        if accumulate:
            o_ref[:, j * cw:(j + 1) * cw] += y
        else:
            o_ref[:, j * cw:(j + 1) * cw] = y


def _shared_body(x_ref, wg_ref, wu_ref, wd_ref, o_ref):
    @pl.when(pl.program_id(1) == 0)
    def _():
        o_ref[...] = jnp.zeros_like(o_ref)

    _ffn_tile(x_ref[...], wg_ref[...], wu_ref[...], wd_ref, o_ref, accumulate=True)


def _shared_ffn(hb, wg, wu, wd, tm=512, th=EXPERT_HIDDEN):
    n, d = hb.shape
    return pl.pallas_call(
        _shared_body,
        out_shape=jax.ShapeDtypeStruct((n, d), F32),
        grid=(n // tm, wg.shape[1] // th),
        in_specs=[pl.BlockSpec((tm, d), lambda i, e: (i, 0)),
                  pl.BlockSpec((d, th), lambda i, e: (0, e)),
                  pl.BlockSpec((d, th), lambda i, e: (0, e)),
                  pl.BlockSpec((th, d), lambda i, e: (e, 0))],
        out_specs=pl.BlockSpec((tm, d), lambda i, e: (i, 0)),
        compiler_params=_params(("parallel", "arbitrary"), 56),
        name="moe_shared",
    )(hb, wg, wu, wd)


MOE_ROW_TILE = 512
MOE_COMBINE_TOKENS = 128


def _start_row_gather(ids_ref, src_hbm, dst, sem, n_rows, first=0):
    for r in range(first, n_rows):
        pltpu.make_async_copy(src_hbm.at[pl.ds(ids_ref[0, 0, r], 1)], dst.at[pl.ds(r, 1)], sem).start(priority=1)


def _wait_row_gather(src_hbm, dst, sem, n_rows):
    pltpu.make_async_copy(src_hbm.at[pl.ds(0, n_rows)], dst, sem).wait()


def _moe_sparse_body(te_ref, nv_ref, tok_ref, tokn_ref, h_hbm, wgf_ref, wuf_ref, wdf_ref, y_ref,
                     xbuf, sem, wg_ref, wu_ref, wd_ref, *, tr, n_col):
    i = pl.program_id(0)
    nv = nv_ref[0]
    slot = i % 2

    new_expert = jnp.logical_or(i == 0, te_ref[i] != te_ref[jnp.maximum(i - 1, 0)])

    @pl.when(jnp.logical_and(i < nv, new_expert))
    def _():
        wg_ref[...] = wgf_ref[0, 0].astype(BF16)
        wu_ref[...] = wuf_ref[0, 0].astype(BF16)
        wd_ref[...] = wdf_ref[0, 0].astype(BF16)

    @pl.when(i == 0)
    def _():
        _start_row_gather(tok_ref, h_hbm, xbuf.at[0], sem.at[0], tr)

    n_batch = 4 + 2 * n_col
    per = tr // n_batch

    def prefetch(b):
        _start_row_gather(tokn_ref, h_hbm, xbuf.at[1 - slot], sem.at[1 - slot], (b + 1) * per, first=b * per)

    @pl.when(i < nv)
    def _():
        _wait_row_gather(h_hbm, xbuf.at[slot], sem.at[slot], tr)
        x_lo, x_hi = _unpack_halves(xbuf[slot])
        x_lo, x_hi = x_lo.astype(BF16), x_hi.astype(BF16)
        half = x_lo.shape[1]
        prefetch(0)
        gte = _dot(x_lo, wg_ref[:half, :])
        prefetch(1)
        gte = gte + _dot(x_hi, wg_ref[half:, :])
        prefetch(2)
        up = _dot(x_lo, wu_ref[:half, :])
        prefetch(3)
        up = up + _dot(x_hi, wu_ref[half:, :])
        hid = ((gte * _sigmoid(gte)) * up).astype(BF16)
        cw = half // n_col
        for j in range(n_col):
            prefetch(4 + 2 * j)
            lo = _dot(hid, wd_ref[:, j * cw:(j + 1) * cw])
            prefetch(5 + 2 * j)
            hi = _dot(hid, wd_ref[:, half + j * cw:half + (j + 1) * cw])
            y_ref[:, j * cw:(j + 1) * cw] = pltpu.pack_elementwise([lo, hi], packed_dtype=BF16)

    @pl.when(i == nv)
    def _():
        _wait_row_gather(h_hbm, xbuf.at[slot], sem.at[slot], tr)

    @pl.when(i >= nv)
    def _():
        y_ref[...] = jnp.zeros_like(y_ref)


def _moe_sparse(h, wg, wu, wd, layer, tok3, tile_expert, n_valid):
    n, d = h.shape
    n_tiles, _, tr = tok3.shape
    dm = 2 * d
    last = lambda i, nv: jnp.minimum(i, nv[0] - 1)
    wspec = lambda w: pl.BlockSpec((1, 1) + w.shape[2:], lambda i, te, nv: (layer, te[last(i, nv)], 0, 0))
    grid_spec = pltpu.PrefetchScalarGridSpec(
        num_scalar_prefetch=2,
        grid=(n_tiles,),
        in_specs=[pl.BlockSpec((1, 1, tr), lambda i, te, nv: (i, 0, 0), memory_space=pltpu.SMEM),
                  pl.BlockSpec((1, 1, tr), lambda i, te, nv: (jnp.minimum(i + 1, n_tiles - 1), 0, 0),
                               memory_space=pltpu.SMEM),
                  pl.BlockSpec(memory_space=pl.ANY),
                  wspec(wg), wspec(wu), wspec(wd)],
        out_specs=pl.BlockSpec((tr, d), lambda i, te, nv: (i, 0)),
        scratch_shapes=[pltpu.VMEM((2, tr, d), jnp.uint32), pltpu.SemaphoreType.DMA((2,)),
                        pltpu.VMEM(wg.shape[2:], BF16), pltpu.VMEM(wu.shape[2:], BF16),
                        pltpu.VMEM(wd.shape[2:], BF16)])
    return pl.pallas_call(
        functools.partial(_moe_sparse_body, tr=tr, n_col=2),
        out_shape=jax.ShapeDtypeStruct((n_tiles * tr, d), jnp.uint32),
        grid_spec=grid_spec,
        compiler_params=_params(("arbitrary",), 60),
        name="moe_sparse",
    )(tile_expert, n_valid, tok3, tok3, h, wg, wu, wd)


def _moe_combine_body(sl_ref, sln_ref, y_hbm, gate_ref, sh_ref, h_ref, g_ref, b_ref, o_ref, ob_ref, buf, sem,
                      *, tm, alpha, n_steps):
    i = pl.program_id(0)
    slot = i % 2
    rows = TOP_K * tm

    @pl.when(i == 0)
    def _():
        _start_row_gather(sl_ref, y_hbm, buf.at[0], sem.at[0], rows)

    @pl.when(i + 1 < n_steps)
    def _():
        _start_row_gather(sln_ref, y_hbm, buf.at[1 - slot], sem.at[1 - slot], rows)

    _wait_row_gather(y_hbm, buf.at[slot], sem.at[slot], rows)
    d = h_ref.shape[1]
    half = d // 2
    acc_lo = sh_ref[:, :half]
    acc_hi = sh_ref[:, half:]
    for k in range(TOP_K):
        lo, hi = _unpack_halves(buf[slot, k * tm:(k + 1) * tm, :])
        gk = gate_ref[:, k:k + 1]
        acc_lo = acc_lo + gk * lo
        acc_hi = acc_hi + gk * hi
    v_lo = alpha * h_ref[:, :half] + acc_lo
    v_hi = alpha * h_ref[:, half:] + acc_hi
    mu = (jnp.sum(v_lo, axis=-1, keepdims=True) + jnp.sum(v_hi, axis=-1, keepdims=True)) / d
    c_lo = v_lo - mu
    c_hi = v_hi - mu
    var = (jnp.sum(c_lo * c_lo, axis=-1, keepdims=True) + jnp.sum(c_hi * c_hi, axis=-1, keepdims=True)) / d
    rs = lax.rsqrt(var + LN_EPS)
    out_lo = c_lo * rs * g_ref[:, :half] + b_ref[:, :half]
    out_hi = c_hi * rs * g_ref[:, half:] + b_ref[:, half:]
    o_ref[:, :half] = out_lo
    o_ref[:, half:] = out_hi
    ob_ref[:, :half] = out_lo.astype(BF16)
    ob_ref[:, half:] = out_hi.astype(BF16)


def _moe_combine(y_sorted, slot3, rout, shared, h, g, b, alpha):
    n, d = h.shape
    n_steps, _, rows = slot3.shape
    tm = rows // TOP_K
    row = lambda i: (i, 0)
    return pl.pallas_call(
        functools.partial(_moe_combine_body, tm=tm, alpha=alpha, n_steps=n_steps),
        out_shape=(jax.ShapeDtypeStruct((n, d), F32), jax.ShapeDtypeStruct((n, d), BF16)),
        grid=(n_steps,),
        in_specs=[pl.BlockSpec((1, 1, rows), lambda i: (i, 0, 0), memory_space=pltpu.SMEM),
                  pl.BlockSpec((1, 1, rows), lambda i: (jnp.minimum(i + 1, n_steps - 1), 0, 0),
                               memory_space=pltpu.SMEM),
                  pl.BlockSpec(memory_space=pl.ANY),
                  pl.BlockSpec((tm, LANES), row),
                  pl.BlockSpec((tm, d), row),
                  pl.BlockSpec((tm, d), row),
                  pl.BlockSpec((1, d), lambda i: (0, 0)),
                  pl.BlockSpec((1, d), lambda i: (0, 0))],
        out_specs=(pl.BlockSpec((tm, d), row), pl.BlockSpec((tm, d), row)),
        scratch_shapes=[pltpu.VMEM((2, rows, d // 2), jnp.uint32), pltpu.SemaphoreType.DMA((2,))],
        compiler_params=_params(("arbitrary",), 48),
        name="moe_combine",
    )(slot3, slot3, y_sorted, rout, shared, h, g.reshape(1, d), b.reshape(1, d))


def _slots_body(r_ref, o_ref, se_ref, run_ref, start_ref, *, tm, tr):
    phase = pl.program_id(0)
    i = pl.program_id(1)
    lane = lax.broadcasted_iota(jnp.int32, (tm, LANES), 1)
    lane_f = lane.astype(F32)
    hits = [r_ref[:, TOP_K + k:TOP_K + k + 1] == lane_f for k in range(TOP_K)]
    onehot = jnp.zeros((tm, LANES), F32)
    for hk in hits:
        onehot = onehot + jnp.where(hk, 1.0, 0.0)

    @pl.when(jnp.logical_and(phase == 0, i == 0))
    def _():
        run_ref[...] = jnp.zeros_like(run_ref)

    @pl.when(phase == 0)
    def _():
        run_ref[...] += jnp.sum(onehot, axis=0, keepdims=True)
        o_ref[...] = jnp.zeros_like(o_ref)
        se_ref[...] = jnp.zeros_like(se_ref)

    @pl.when(jnp.logical_and(phase == 1, i == 0))
    def _():
        tiles = jnp.floor((run_ref[...] + (tr - 1)) * (1.0 / tr))
        r = lax.broadcasted_iota(jnp.int32, (LANES, LANES), 0)
        c = lax.broadcasted_iota(jnp.int32, (LANES, LANES), 1)
        before = jnp.where(r < c, 1.0, 0.0).astype(BF16)
        t8 = jnp.broadcast_to(tiles, (SUBLANES, LANES)).astype(BF16)
        start_tiles = _dot(t8, before)[0:1, :]
        start_ref[...] = start_tiles * tr
        run_ref[...] = jnp.zeros_like(run_ref)

    @pl.when(phase == 1)
    def _():
        rr = lax.broadcasted_iota(jnp.int32, (tm, tm), 0)
        cc = lax.broadcasted_iota(jnp.int32, (tm, tm), 1)
        earlier = jnp.where(rr > cc, 1.0, 0.0).astype(BF16)
        rank = _dot(earlier, onehot.astype(BF16)) + run_ref[...] + start_ref[...]
        out = jnp.zeros((tm, LANES), F32)
        for k, hk in enumerate(hits):
            slot_k = jnp.sum(jnp.where(hk, rank, 0.0), axis=1, keepdims=True)
            out = jnp.where(lane == k, slot_k, out)
        o_ref[...] = out
        run_ref[...] += jnp.sum(onehot, axis=0, keepdims=True)
        tiles = jnp.floor((run_ref[...] + (tr - 1)) * (1.0 / tr))
        se_ref[...] = start_ref[...] + tiles * tr


def _route_slots(rout, tr, tm=512):
    n = rout.shape[0]
    return pl.pallas_call(
        functools.partial(_slots_body, tm=tm, tr=tr),
        out_shape=(jax.ShapeDtypeStruct((n, LANES), F32), jax.ShapeDtypeStruct((1, LANES), F32)),
        grid=(2, n // tm),
        in_specs=[pl.BlockSpec((tm, LANES), lambda p, i: (i, 0))],
        out_specs=(pl.BlockSpec((tm, LANES), lambda p, i: (p * i, 0)),
                   pl.BlockSpec((1, LANES), lambda p, i: (0, 0))),
        scratch_shapes=[pltpu.VMEM((1, LANES), F32), pltpu.VMEM((1, LANES), F32)],
        compiler_params=_params(("arbitrary", "arbitrary"), 32),
        name="moe_slots",
    )(rout)


def _dispatch_plan(rout, n_experts, tr, tm):
    n = rout.shape[0]
    slots_f, seg_end_f = _route_slots(rout, tr)
    slot = slots_f[:, :TOP_K].astype(jnp.int32)
    seg_end = seg_end_f[0, :n_experts].astype(jnp.int32)
    n_tiles = n * TOP_K // tr + n_experts + 1
    tok = jnp.zeros((n_tiles * tr,), jnp.int32).at[slot.reshape(-1)].set(
        jnp.repeat(jnp.arange(n, dtype=jnp.int32), TOP_K), unique_indices=True)
    tile_start = jnp.arange(n_tiles, dtype=jnp.int32) * tr
    tile_expert = jnp.minimum(jnp.sum((seg_end[None, :] <= tile_start[:, None]).astype(jnp.int32), axis=1),
                              n_experts - 1)
    n_valid = (seg_end[-1:] // tr).astype(jnp.int32)
    slot3 = slot.reshape(n // tm, tm, TOP_K).transpose(0, 2, 1).reshape(n // tm, 1, TOP_K * tm)
    return tok.reshape(n_tiles, 1, tr), tile_expert, n_valid, slot3.astype(jnp.int32)


def kernel(x, mem, ev_w_in, ev_pool_w, ev_pool_scale, ev_sinks, ev_w_out, rel_bias, od_w_in, hg_lb_logits, od_hg_norm, od_a_re, od_a_im, od_log_step, od_b_re, od_b_im, od_c_re, od_c_im, od_d_skip, od_w_glu, od_b_glu, od_w_out, xa_wq, xa_wk, xa_wv, xa_wo, moe_router, moe_bias, moe_w_gate, moe_w_up, moe_w_down, sh_w_gate, sh_w_up, sh_w_down, ln_g, ln_b):
    bsz, seq, d = x.shape
    n = bsz * seq
    depth = ln_g.shape[0]
    alpha = (2 * depth) ** 0.25
    memb = mem.astype(BF16).reshape(bsz * mem.shape[1], d)

    h = x.astype(F32).reshape(n, d)
    hb = h.astype(BF16)
    bias_tab = _bias_table(rel_bias)

    for l in range(depth):
        j = l // 2
        if l % 2 == 0:
            pool_w = ev_pool_w[j]
            pool_width = pool_w.shape[0] * pool_w.shape[1]
            n_q = ev_sinks.shape[1]
            n_kv = n_q // SWA_GQ
            z = _mm(hb, ev_w_in[j].astype(BF16), F32)
            z3 = z.reshape(bsz, seq, z.shape[1])
            y_a = _pool(z3, pool_w.astype(BF16), ev_pool_scale[j].astype(F32))
            y_b = _swa(z3, ev_sinks[j], bias_tab, pool_width, n_q, n_kv)
            mix = _mm2(y_a.reshape(n, -1), y_b.reshape(n, -1), ev_w_out[j].astype(BF16), F32)
        else:
            n_heads = hg_lb_logits.shape[1] // HG_DIM
            hgw = n_heads * HG_DIM
            ng, nst = od_a_re.shape[1], od_a_re.shape[2]
            s5w = ng * S5_GROUP
            z = _mm(hb, od_w_in[j].astype(BF16), F32)
            z3 = z.reshape(bsz, seq, z.shape[1])
            y_c = _hgrn(z3, hg_lb_logits, od_hg_norm[j], l, n_heads)
            m_mat, p_mat, q_mat, t_re, t_im = _s5_prep(od_a_re[j], od_a_im[j], od_log_step[j], od_b_re[j],
                                                        od_b_im[j], od_c_re[j], od_c_im[j])
            nck = seq // S5_CHUNK
            u = z3[:, :, 4 * hgw:].astype(BF16).reshape(bsz, nck, S5_CHUNK, ng, S5_GROUP)
            ug = u.transpose(3, 0, 1, 2, 4).reshape(ng, bsz, nck, S5_CHUNK * S5_GROUP)
            yg = _s5_scan(ug, m_mat, p_mat, q_mat, t_re, t_im)
            y_s = yg.reshape(ng, bsz, nck, S5_CHUNK, S5_GROUP).transpose(1, 2, 3, 0, 4).reshape(n, s5w)
            y_d = _glu(y_s, z, (4 * hgw) // s5w, od_d_skip[j], od_w_glu[j], od_b_glu[j])
            mix = _mm2(y_c.reshape(n, hgw), y_d, od_w_out[j].astype(BF16), F32)
        h, hb = _add_ln(h, mix, ln_g[l, 0], ln_b[l, 0], alpha)

        xw = xa_wq.shape[2]
        kmem = _mm(memb, xa_wk[l].astype(BF16), BF16).reshape(bsz, -1, xw)
        vmem = _mm(memb, xa_wv[l].astype(BF16), BF16).reshape(bsz, -1, xw)
        h, hb, hp = _xattn(h, hb, kmem, vmem, xa_wq[l].astype(BF16), xa_wo[l].astype(BF16),
                           ln_g[l, 1], ln_b[l, 1], alpha, seq)

        n_exp = moe_router.shape[1]
        rout = _router(h, moe_router[l], moe_bias[l])
        tok3, tile_expert, n_valid, slot3 = _dispatch_plan(rout, n_exp, MOE_ROW_TILE, MOE_COMBINE_TOKENS)
        shared = _shared_ffn(hb, sh_w_gate[l].astype(BF16), sh_w_up[l].astype(BF16), sh_w_down[l].astype(BF16))
        y_sorted = _moe_sparse(hp, moe_w_gate.astype(F32), moe_w_up.astype(F32), moe_w_down.astype(F32), l,
                               tok3, tile_expert, n_valid)
        h, hb = _moe_combine(y_sorted, slot3, rout, shared, h, ln_g[l, 2], ln_b[l, 2], alpha)

    return h.reshape(bsz, seq, d)
```

```python
import functools
import math

import jax
import jax.numpy as jnp
from jax import lax
from jax.experimental import pallas as pl
from jax.experimental.pallas import tpu as pltpu

F32 = jnp.float32
BF16 = jnp.bfloat16

V7X_VMEM_BYTES = 64 * 1024 * 1024
LANES = 128
SUBLANES = 8

POOL_WINDOWS = (2, 4, 8, 16)
SWA_HEAD_DIM = 64
SWA_GQ = 8
SWA_BLOCK = 128
REL_BUCKETS = 32
REL_MAX_DIST = 128
HG_DIM = 128
HG_CHUNK = 64
RMS_EPS = 1e-6
S5_GROUP = 16
S5_STATE = 64
S5_CHUNK = 16
XA_HEADS = 4
XA_HEAD_DIM = 128
N_EXPERTS = 64
TOP_K = 8
N_EXPERT_GROUPS = 8
TOPK_GROUPS = 4
EXPERT_HIDDEN = 256
ROUTED_SCALE = 2.5
LN_EPS = 1e-5


def _params(sem, vmem_mb):
    return pltpu.CompilerParams(dimension_semantics=sem, vmem_limit_bytes=vmem_mb * 1024 * 1024)


def _nt(a, b):
    return lax.dot_general(a, b, (((1,), (1,)), ((), ())), preferred_element_type=F32)


def _tn(a, b):
    return lax.dot_general(a, b, (((0,), (0,)), ((), ())), preferred_element_type=F32)


def _dot(a, b):
    return jnp.dot(a, b, preferred_element_type=F32)


def _sigmoid(x):
    return 0.5 * (jnp.tanh(0.5 * x) + 1.0)


def _mm_body(x_ref, w_ref, o_ref):
    o_ref[...] = _dot(x_ref[...], w_ref[...]).astype(o_ref.dtype)


def _mm(x, w, out_dtype, tm=1024, tn=512):
    m, k = x.shape
    n = w.shape[1]
    tm = min(tm, m)
    tn = min(tn, n)
    return pl.pallas_call(
        _mm_body,
        out_shape=jax.ShapeDtypeStruct((m, n), out_dtype),
        grid=(m // tm, n // tn),
        in_specs=[pl.BlockSpec((tm, k), lambda i, j: (i, 0)),
                  pl.BlockSpec((k, tn), lambda i, j: (0, j))],
        out_specs=pl.BlockSpec((tm, tn), lambda i, j: (i, j)),
        compiler_params=_params(("parallel", "parallel"), 48),
        name="mm",
    )(x, w)


def _mm2_body(a_ref, b_ref, w_ref, o_ref):
    ka = a_ref.shape[1]
    o_ref[...] = (_dot(a_ref[...], w_ref[:ka, :]) + _dot(b_ref[...], w_ref[ka:, :])).astype(o_ref.dtype)


def _mm2(a, b, w, out_dtype, tm=1024, tn=512):
    m, ka = a.shape
    kb = b.shape[1]
    n = w.shape[1]
    return pl.pallas_call(
        _mm2_body,
        out_shape=jax.ShapeDtypeStruct((m, n), out_dtype),
        grid=(m // tm, n // tn),
        in_specs=[pl.BlockSpec((tm, ka), lambda i, j: (i, 0)),
                  pl.BlockSpec((tm, kb), lambda i, j: (i, 0)),
                  pl.BlockSpec((ka + kb, tn), lambda i, j: (0, j))],
        out_specs=pl.BlockSpec((tm, tn), lambda i, j: (i, j)),
        compiler_params=_params(("parallel", "parallel"), 48),
        name="mm2",
    )(a, b, w)


def _add_ln_body(h_ref, y_ref, g_ref, b_ref, o_ref, ob_ref, *, alpha):
    v = alpha * h_ref[...] + y_ref[...].astype(F32)
    mu = jnp.mean(v, axis=-1, keepdims=True)
    c = v - mu
    var = jnp.mean(c * c, axis=-1, keepdims=True)
    out = c * lax.rsqrt(var + LN_EPS) * g_ref[...] + b_ref[...]
    o_ref[...] = out
    ob_ref[...] = out.astype(BF16)


def _add_ln(h, y, g, b, alpha, tm=256):
    n, d = h.shape
    return pl.pallas_call(
        functools.partial(_add_ln_body, alpha=alpha),
        out_shape=(jax.ShapeDtypeStruct((n, d), F32), jax.ShapeDtypeStruct((n, d), BF16)),
        grid=(n // tm,),
        in_specs=[pl.BlockSpec((tm, d), lambda i: (i, 0)),
                  pl.BlockSpec((tm, d), lambda i: (i, 0)),
                  pl.BlockSpec((1, d), lambda i: (0, 0)),
                  pl.BlockSpec((1, d), lambda i: (0, 0))],
        out_specs=(pl.BlockSpec((tm, d), lambda i: (i, 0)),
                   pl.BlockSpec((tm, d), lambda i: (i, 0))),
        compiler_params=_params(("parallel",), 48),
        name="add_ln",
    )(h, y, g.reshape(1, d), b.reshape(1, d))


def _pool_body(u_ref, prev_ref, w_ref, sc_ref, o_ref, *, ts, gw, halo):
    i = pl.program_id(1)
    row = lax.broadcasted_iota(jnp.int32, (ts, gw), 0)
    pos1 = (row + i * ts + 1).astype(F32)
    for g, win in enumerate(POOL_WINDOWS):
        u = u_ref[0, :, g * gw:(g + 1) * gw]
        prev = prev_ref[0, :, g * gw:(g + 1) * gw]
        prev = jnp.where(i == 0, 0.0, prev)
        s = jnp.concatenate([prev, u], axis=0)
        step = 1
        while step < win:
            s = s + pltpu.roll(s, step, axis=0)
            step *= 2
        wsum = s[halo:, :]
        pooled = wsum / jnp.minimum(pos1, float(win)) - u
        y = _dot(pooled.astype(BF16), w_ref[g])
        o_ref[0, :, g * gw:(g + 1) * gw] = (y * sc_ref[:, g * gw:(g + 1) * gw]).astype(o_ref.dtype)


def _pool(z3, pool_w, pool_scale, ts=512):
    bsz, s, _ = z3.shape
    ng, gw, _ = pool_w.shape
    width = ng * gw
    halo = max(POOL_WINDOWS)
    return pl.pallas_call(
        functools.partial(_pool_body, ts=ts, gw=gw, halo=halo),
        out_shape=jax.ShapeDtypeStruct((bsz, s, width), BF16),
        grid=(bsz, s // ts),
        in_specs=[pl.BlockSpec((1, ts, width), lambda b, i: (b, i, 0)),
                  pl.BlockSpec((1, halo, width), lambda b, i: (b, jnp.maximum(i * (ts // halo) - 1, 0), 0)),
                  pl.BlockSpec((ng, gw, gw), lambda b, i: (0, 0, 0)),
                  pl.BlockSpec((1, width), lambda b, i: (0, 0))],
        out_specs=pl.BlockSpec((1, ts, width), lambda b, i: (b, i, 0)),
        compiler_params=_params(("parallel", "parallel"), 48),
        name="pool",
    )(z3, z3, pool_w, pool_scale.reshape(1, width))


def _t5_bucket_table():
    qi = jnp.arange(SWA_BLOCK)[:, None]
    kj = jnp.arange(2 * SWA_BLOCK)[None, :]
    dist = qi + SWA_BLOCK - kj
    band = (dist >= 0) & (dist < SWA_BLOCK)
    d = jnp.maximum(dist, 0)
    exact = REL_BUCKETS // 2
    far = exact + (jnp.log(jnp.maximum(d, exact).astype(F32) / exact)
                   / math.log(REL_MAX_DIST / exact) * (REL_BUCKETS - exact)).astype(jnp.int32)
    bucket = jnp.where(d < exact, d, jnp.minimum(far, REL_BUCKETS - 1))
    return jnp.where(band, bucket, -1).astype(jnp.int32)


def _bias_body(rb_ref, bucket_ref, o_ref):
    h = pl.program_id(0)
    bucket = bucket_ref[...]
    acc = jnp.full(bucket.shape, -jnp.inf, F32)
    for b in range(REL_BUCKETS):
        acc = jnp.where(bucket == b, rb_ref[b, h], acc)
    o_ref[0] = acc


def _bias_table(rel_bias):
    nb, nh = rel_bias.shape
    bucket = _t5_bucket_table()
    return pl.pallas_call(
        _bias_body,
        out_shape=jax.ShapeDtypeStruct((nh, SWA_BLOCK, 2 * SWA_BLOCK), F32),
        grid=(nh,),
        in_specs=[pl.BlockSpec(memory_space=pltpu.SMEM),
                  pl.BlockSpec((SWA_BLOCK, 2 * SWA_BLOCK), lambda h: (0, 0))],
        out_specs=pl.BlockSpec((1, SWA_BLOCK, 2 * SWA_BLOCK), lambda h: (h, 0, 0)),
        compiler_params=_params(("parallel",), 32),
        name="swa_bias",
    )(rel_bias.astype(F32), bucket)


def _swa_body(sink_ref, q_ref, kp_ref, kc_ref, vp_ref, vc_ref, bias_ref, o_ref, *, n_kv):
    n = pl.program_id(1)
    dh = SWA_HEAD_DIM
    col = lax.broadcasted_iota(jnp.int32, (SWA_BLOCK, 2 * SWA_BLOCK), 1)
    no_prev = jnp.logical_and(n == 0, col < SWA_BLOCK)
    scale = dh ** -0.5
    for kv in range(n_kv):
        sl = slice(kv * dh, (kv + 1) * dh)
        k2 = jnp.concatenate([kp_ref[0, :, sl], kc_ref[0, :, sl]], axis=0).astype(BF16)
        v2 = jnp.concatenate([vp_ref[0, :, sl], vc_ref[0, :, sl]], axis=0).astype(BF16)
        for g in range(SWA_GQ):
            h = kv * SWA_GQ + g
            qh = q_ref[0, :, h * dh:(h + 1) * dh].astype(BF16)
            logits = _nt(qh, k2) * scale + bias_ref[h]
            logits = jnp.where(no_prev, -jnp.inf, logits)
            sink = sink_ref[h]
            m = jnp.maximum(jnp.max(logits, axis=-1, keepdims=True), sink)
            p = jnp.exp(logits - m)
            p = p / (jnp.sum(p, axis=-1, keepdims=True) + jnp.exp(sink - m))
            o_ref[0, :, h * dh:(h + 1) * dh] = _dot(p.astype(BF16), v2).astype(o_ref.dtype)


def _swa(z3, sinks, bias_tab, q_off, n_q, n_kv):
    bsz, s, _ = z3.shape
    dh = SWA_HEAD_DIM
    qw, kvw = n_q * dh, n_kv * dh
    qb = q_off // qw
    kb = (q_off + qw) // kvw
    vb = kb + 1
    blk = SWA_BLOCK
    prev = lambda n: jnp.maximum(n - 1, 0)
    return pl.pallas_call(
        functools.partial(_swa_body, n_kv=n_kv),
        out_shape=jax.ShapeDtypeStruct((bsz, s, qw), BF16),
        grid=(bsz, s // blk),
        in_specs=[pl.BlockSpec(memory_space=pltpu.SMEM),
                  pl.BlockSpec((1, blk, qw), lambda b, n: (b, n, qb)),
                  pl.BlockSpec((1, blk, kvw), lambda b, n: (b, prev(n), kb)),
                  pl.BlockSpec((1, blk, kvw), lambda b, n: (b, n, kb)),
                  pl.BlockSpec((1, blk, kvw), lambda b, n: (b, prev(n), vb)),
                  pl.BlockSpec((1, blk, kvw), lambda b, n: (b, n, vb)),
                  pl.BlockSpec((n_q, blk, 2 * blk), lambda b, n: (0, 0, 0))],
        out_specs=pl.BlockSpec((1, blk, qw), lambda b, n: (b, n, 0)),
        compiler_params=_params(("parallel", "parallel"), 48),
        name="swa",
    )(sinks.astype(F32), z3, z3, z3, z3, z3, bias_tab)


def _hgrn_body(q_ref, f_ref, i_ref, g_ref, lbl_ref, nw_ref, o_ref, st_ref, *, n_chunk, layer):
    @pl.when(pl.program_id(2) == 0)
    def _():
        st_ref[...] = jnp.zeros_like(st_ref)

    lg = lbl_ref[:, 0, 0, :]
    e = jnp.exp(lg - jnp.max(lg, axis=0, keepdims=True))
    p = e / jnp.sum(e, axis=0, keepdims=True)
    lb = jnp.sum(p[:layer + 1], axis=0, keepdims=True) - p[0:1]
    nw = nw_ref[...]
    c = HG_CHUNK
    r = lax.broadcasted_iota(jnp.int32, (c, c), 0)
    cc = lax.broadcasted_iota(jnp.int32, (c, c), 1)
    causal = r >= cc
    tri = jnp.where(causal, 1.0, 0.0).astype(BF16)

    d = HG_DIM
    chunks = lambda t: [t[ci * c:(ci + 1) * c] for ci in range(n_chunk)]
    q = q_ref[0]
    iv = chunks(i_ref[0].astype(BF16))
    g = g_ref[0]
    qf = q * _sigmoid(q)
    forget = lb + (1.0 - lb) * _sigmoid(f_ref[0])
    log_f = jnp.log(forget)
    k_in = 1.0 - forget
    hi = log_f.astype(BF16)
    r1 = log_f - hi.astype(F32)
    mid = r1.astype(BF16)
    lo = (r1 - mid.astype(F32)).astype(BF16)
    cum = jnp.concatenate([_dot(tri, a) + _dot(tri, b) + _dot(tri, e3)
                           for a, b, e3 in zip(chunks(hi), chunks(mid), chunks(lo))], axis=0)
    cum3 = cum.reshape(n_chunk, c, d)
    ref = cum3[:, c // 2:c // 2 + 1, :]
    total = cum3[:, c - 1:c, :]
    qa = qf.reshape(n_chunk, c, d) * jnp.exp(cum3 - ref)
    kb = k_in.reshape(n_chunk, c, d) * jnp.exp(ref - cum3)
    qe = chunks((qa * jnp.exp(ref)).astype(BF16).reshape(n_chunk * c, d))
    ku = chunks((kb * jnp.exp(total - ref)).astype(BF16).reshape(n_chunk * c, d))
    qa = chunks(qa.astype(BF16).reshape(n_chunk * c, d))
    kb = chunks(kb.astype(BF16).reshape(n_chunk * c, d))
    decay = jnp.exp(total)
    st = st_ref[...]
    outs = []
    for ci in range(n_chunk):
        scores = jnp.where(causal, _nt(qa[ci], kb[ci]), 0.0)
        o_intra = _dot(scores.astype(BF16), iv[ci])
        o_inter = _nt(qe[ci], st.astype(BF16))
        st = st * decay[ci] + _tn(iv[ci], ku[ci])
        outs.append(o_intra + o_inter)
    st_ref[...] = st
    o = jnp.concatenate(outs, axis=0)
    o = o * lax.rsqrt(jnp.mean(o * o, axis=-1, keepdims=True) + RMS_EPS) * nw
    o_ref[0] = (o * (g * _sigmoid(g))).astype(o_ref.dtype)


def _hgrn(z3, lb_logits, norm_w, layer, n_heads, ts=512):
    bsz, s, _ = z3.shape
    depth = lb_logits.shape[0]
    d = HG_DIM
    spec = lambda off: pl.BlockSpec((1, ts, d), lambda b, h, t: (b, t, off + h))
    return pl.pallas_call(
        functools.partial(_hgrn_body, n_chunk=ts // HG_CHUNK, layer=layer),
        out_shape=jax.ShapeDtypeStruct((bsz, s, n_heads * d), BF16),
        grid=(bsz, n_heads, s // ts),
        in_specs=[spec(0), spec(n_heads), spec(2 * n_heads), spec(3 * n_heads),
                  pl.BlockSpec((depth, 1, 1, d), lambda b, h, t: (0, h, 0, 0)),
                  pl.BlockSpec((1, d), lambda b, h, t: (0, 0))],
        out_specs=pl.BlockSpec((1, ts, d), lambda b, h, t: (b, t, h)),
        scratch_shapes=[pltpu.VMEM((d, d), F32)],
        compiler_params=_params(("parallel", "parallel", "arbitrary"), 32),
        name="hgrn2",
    )(z3, z3, z3, z3, lb_logits.astype(F32).reshape(depth, n_heads, 1, d), norm_w.astype(F32).reshape(1, d))


def _cpow(pw, lam_re, lam_im):
    mag = jnp.exp(pw * lam_re)
    ang = pw * lam_im
    return mag * jnp.cos(ang), mag * jnp.sin(ang)


def _s5_prep_body(ls_ref, ar_ref, ai_ref, arc_ref, aic_ref, br_ref, bi_ref, cr_ref, ci_ref,
                  m_ref, p_ref, q_ref, tr_ref, ti_ref):
    nst = S5_STATE
    lc = S5_CHUNK
    gc = S5_GROUP
    w = lc * gc
    dt = jnp.exp(ls_ref[0])
    a_re, a_im = ar_ref[0], ai_ref[0]
    lam_re, lam_im = dt * a_re, dt * a_im
    ab_re, ab_im = _cpow(1.0, lam_re, lam_im)
    den = a_re * a_re + a_im * a_im
    co_re = ((ab_re - 1.0) * a_re + ab_im * a_im) / den
    co_im = (ab_im * a_re - (ab_re - 1.0) * a_im) / den
    b_re, b_im = br_ref[0], bi_ref[0]
    bb_re = co_re * b_re - co_im * b_im
    bb_im = co_re * b_im + co_im * b_re
    lane = lax.broadcasted_iota(jnp.int32, (w, 2 * nst), 1)
    rowi = lax.broadcasted_iota(jnp.int32, (w, 2 * nst), 0)

    pw = (lc - 1 - rowi // gc).astype(F32)
    ap_re, ap_im = _cpow(pw, lam_re, lam_im)
    tb_re = jnp.concatenate([bb_re] * lc, axis=0)
    tb_im = jnp.concatenate([bb_im] * lc, axis=0)
    p_ref[0] = jnp.where(lane < nst, ap_re * tb_re - ap_im * tb_im,
                         ap_re * tb_im + ap_im * tb_re).astype(p_ref.dtype)

    lam_re_c, lam_im_c = dt * arc_ref[0], dt * aic_ref[0]
    c_re, c_im = cr_ref[0], ci_ref[0]
    lane_q = lax.broadcasted_iota(jnp.int32, (2 * nst, w), 1)
    row_q = lax.broadcasted_iota(jnp.int32, (2 * nst, w), 0)
    tq = (lane_q // gc).astype(F32)

    def c_times_apow(pwq):
        pr, pi = _cpow(pwq, lam_re_c, lam_im_c)
        return jnp.where(row_q < nst, c_re * pr - c_im * pi, -(c_re * pi + c_im * pr))

    q_ref[0] = c_times_apow(tq + 1.0).astype(q_ref.dtype)
    wmat = c_times_apow(tq)

    bbt = jnp.where(lax.broadcasted_iota(jnp.int32, (gc, 2 * nst), 1) < nst, bb_re, bb_im)
    r = jnp.dot(bbt, wmat, preferred_element_type=F32, precision=lax.Precision.HIGHEST)
    lane_r = lax.broadcasted_iota(jnp.int32, (gc, w), 1)
    for t in range(lc):
        blk = r if t == 0 else jnp.where(lane_r >= t * gc, pltpu.roll(r, t * gc, axis=1), 0.0)
        m_ref[0, t * gc:(t + 1) * gc, :] = blk.astype(m_ref.dtype)

    kk = lax.broadcasted_iota(jnp.int32, (16, 2 * nst), 0)
    pws = (lc * jnp.left_shift(1, jnp.minimum(kk, 12))).astype(F32)
    sr, si = _cpow(pws, lam_re, lam_im)
    tr_ref[0] = sr
    ti_ref[0] = jnp.where(lax.broadcasted_iota(jnp.int32, (16, 2 * nst), 1) < nst, -si, si)


def _s5_prep(a_re, a_im, log_step, b_re, b_im, c_re, c_im):
    ng, nst = a_re.shape
    gc, lc = S5_GROUP, S5_CHUNK
    w = lc * gc
    dup = lambda t: jnp.concatenate([t, t], axis=-1)
    ar = dup(a_re.astype(F32))
    ai = dup(a_im.astype(F32))
    brt = dup(jnp.swapaxes(b_re.astype(F32), 1, 2))
    bit = dup(jnp.swapaxes(b_im.astype(F32), 1, 2))
    tile_c = lambda t: jnp.tile(jnp.concatenate([jnp.swapaxes(t.astype(F32), 1, 2)] * 2, axis=1), (1, 1, lc))
    crt, cit = tile_c(c_re), tile_c(c_im)
    g3 = lambda blk: pl.BlockSpec((1,) + blk, lambda g: (g, 0, 0))
    return pl.pallas_call(
        _s5_prep_body,
        out_shape=(jax.ShapeDtypeStruct((ng, w, w), BF16),
                   jax.ShapeDtypeStruct((ng, w, 2 * nst), BF16),
                   jax.ShapeDtypeStruct((ng, 2 * nst, w), BF16),
                   jax.ShapeDtypeStruct((ng, 16, 2 * nst), F32),
                   jax.ShapeDtypeStruct((ng, 16, 2 * nst), F32)),
        grid=(ng,),
        in_specs=[g3((1, 1)), g3((1, 2 * nst)), g3((1, 2 * nst)), g3((2 * nst, 1)), g3((2 * nst, 1)),
                  g3((gc, 2 * nst)), g3((gc, 2 * nst)), g3((2 * nst, w)), g3((2 * nst, w))],
        out_specs=(g3((w, w)), g3((w, 2 * nst)), g3((2 * nst, w)), g3((16, 2 * nst)), g3((16, 2 * nst))),
        compiler_params=_params(("parallel",), 32),
        name="s5_prep",
    )(log_step.astype(F32).reshape(ng, 1, 1), ar.reshape(ng, 1, 2 * nst), ai.reshape(ng, 1, 2 * nst),
      ar.reshape(ng, 2 * nst, 1), ai.reshape(ng, 2 * nst, 1), brt, bit, crt, cit)


def _s5_scan_body(u_ref, m_ref, p_ref, q_ref, tr_ref, ti_ref, y_ref, *, n_steps):
    nst = S5_STATE
    u = u_ref[0, 0]
    x = _dot(u, p_ref[0])
    row = lax.broadcasted_iota(jnp.int32, x.shape, 0)
    tr = tr_ref[0]
    ti = ti_ref[0]
    for k in range(n_steps):
        s = 1 << k
        sh = jnp.where(row >= s, pltpu.roll(x, s, axis=0), 0.0)
        x = x + tr[k:k + 1, :] * sh + ti[k:k + 1, :] * pltpu.roll(sh, nst, axis=1)
    x_in = jnp.where(row >= 1, pltpu.roll(x, 1, axis=0), 0.0)
    y_ref[0, 0] = _dot(u, m_ref[0]) + _dot(x_in.astype(BF16), q_ref[0])


def _s5_scan(ug, m, p, q, tr, ti):
    ng, bsz, nck, w = ug.shape
    nst2 = p.shape[-1]
    n_steps = int(math.log2(nck))
    assert (1 << n_steps) == nck
    g3 = lambda blk: pl.BlockSpec((1,) + blk, lambda g, b: (g, 0, 0))
    return pl.pallas_call(
        functools.partial(_s5_scan_body, n_steps=n_steps),
        out_shape=jax.ShapeDtypeStruct((ng, bsz, nck, w), F32),
        grid=(ng, bsz),
        in_specs=[pl.BlockSpec((1, 1, nck, w), lambda g, b: (g, b, 0, 0)),
                  g3((w, w)), g3((w, nst2)), g3((nst2, w)), g3((16, nst2)), g3((16, nst2))],
        out_specs=pl.BlockSpec((1, 1, nck, w), lambda g, b: (g, b, 0, 0)),
        compiler_params=_params(("parallel", "parallel"), 32),
        name="s5_scan",
    )(ug, m, p, q, tr, ti)


def _glu_body(y_ref, u_ref, d_ref, w_ref, b_ref, o_ref):
    yy = y_ref[...] + d_ref[...] * u_ref[...]
    z = 0.5 * yy * (1.0 + jnp.tanh(math.sqrt(2.0 / math.pi) * (yy + 0.044715 * (yy * yy * yy))))
    gate = _sigmoid(_dot(z.astype(BF16), w_ref[...]) + b_ref[...])
    o_ref[...] = (z * gate).astype(o_ref.dtype)


def _glu(y, z2, u_blk, d_skip, w_glu, b_glu, tm=512):
    n, wd = y.shape
    return pl.pallas_call(
        _glu_body,
        out_shape=jax.ShapeDtypeStruct((n, wd), BF16),
        grid=(n // tm,),
        in_specs=[pl.BlockSpec((tm, wd), lambda i: (i, 0)),
                  pl.BlockSpec((tm, wd), lambda i: (i, u_blk)),
                  pl.BlockSpec((1, wd), lambda i: (0, 0)),
                  pl.BlockSpec((wd, wd), lambda i: (0, 0)),
                  pl.BlockSpec((1, wd), lambda i: (0, 0))],
        out_specs=pl.BlockSpec((tm, wd), lambda i: (i, 0)),
        compiler_params=_params(("parallel",), 32),
        name="s5_glu",
    )(y, z2, d_skip.astype(F32).reshape(1, wd), w_glu.astype(BF16), b_glu.astype(F32).reshape(1, wd))


def _pack_halves(v):
    c = v.shape[1] // 2
    return pltpu.pack_elementwise([v[:, :c], v[:, c:]], packed_dtype=BF16)


def _unpack_halves(w):
    return tuple(pltpu.unpack_elementwise(w, index=i, packed_dtype=BF16, unpacked_dtype=F32) for i in (0, 1))


def _xattn_body(hb_ref, h_ref, wq_ref, k_ref, v_ref, wo_ref, g_ref, b_ref, o_ref, ob_ref, op_ref, *, alpha):
    d = XA_HEAD_DIM
    q = _dot(hb_ref[...], wq_ref[...]).astype(BF16)
    outs = []
    for hd in range(XA_HEADS):
        sl = slice(hd * d, (hd + 1) * d)
        logits = _nt(q[:, sl], k_ref[0, :, sl]) * (d ** -0.5)
        m = jnp.max(logits, axis=-1, keepdims=True)
        p = jnp.exp(logits - m)
        p = p / jnp.sum(p, axis=-1, keepdims=True)
        outs.append(_dot(p.astype(BF16), v_ref[0, :, sl]))
    o = jnp.concatenate(outs, axis=-1).astype(BF16)
    v = alpha * h_ref[...] + _dot(o, wo_ref[...])
    mu = jnp.mean(v, axis=-1, keepdims=True)
    c = v - mu
    var = jnp.mean(c * c, axis=-1, keepdims=True)
    out = c * lax.rsqrt(var + LN_EPS) * g_ref[...] + b_ref[...]
    o_ref[...] = out
    ob_ref[...] = out.astype(BF16)
    op_ref[...] = _pack_halves(out)


def _xattn(h, hb, kmem, vmem, wq, wo, g, b, alpha, seq, tm=256):
    n, d = h.shape
    xw = wq.shape[1]
    ml = kmem.shape[1]
    per_b = seq // tm
    return pl.pallas_call(
        functools.partial(_xattn_body, alpha=alpha),
        out_shape=(jax.ShapeDtypeStruct((n, d), F32), jax.ShapeDtypeStruct((n, d), BF16),
                   jax.ShapeDtypeStruct((n, d // 2), jnp.uint32)),
        grid=(n // tm,),
        in_specs=[pl.BlockSpec((tm, d), lambda i: (i, 0)),
                  pl.BlockSpec((tm, d), lambda i: (i, 0)),
                  pl.BlockSpec((d, xw), lambda i: (0, 0)),
                  pl.BlockSpec((1, ml, xw), lambda i: (i // per_b, 0, 0)),
                  pl.BlockSpec((1, ml, xw), lambda i: (i // per_b, 0, 0)),
                  pl.BlockSpec((xw, d), lambda i: (0, 0)),
                  pl.BlockSpec((1, d), lambda i: (0, 0)),
                  pl.BlockSpec((1, d), lambda i: (0, 0))],
        out_specs=(pl.BlockSpec((tm, d), lambda i: (i, 0)),
                   pl.BlockSpec((tm, d), lambda i: (i, 0)),
                   pl.BlockSpec((tm, d // 2), lambda i: (i, 0))),
        compiler_params=_params(("parallel",), 56),
        name="xattn",
    )(hb, h, wq, kmem, vmem, wo, g.reshape(1, d), b.reshape(1, d))


def _first_max(v, idx, n):
    m = jnp.max(v, axis=0, keepdims=True)
    first = jnp.min(jnp.where(v == m, idx, n), axis=0, keepdims=True)
    return m, first


def _router_body(x_ref, r_ref, rb_ref, o_ref, *, tm):
    ne, ng = N_EXPERTS, N_EXPERT_GROUPS
    per = ne // ng
    logits = lax.dot_general(r_ref[...], x_ref[...], (((1,), (1,)), ((), ())),
                             preferred_element_type=F32, precision=lax.Precision.HIGHEST)
    scores = _sigmoid(logits)
    biased = scores + rb_ref[...]
    gsc = []
    eidx = lax.broadcasted_iota(jnp.int32, (per, tm), 0)
    for g in range(ng):
        v = biased[g * per:(g + 1) * per, :]
        m1, i1 = _first_max(v, eidx, per)
        m2 = jnp.max(jnp.where(eidx == i1, -jnp.inf, v), axis=0, keepdims=True)
        gsc.append(m1 + m2)
    gs = jnp.concatenate(gsc, axis=0)
    gidx = lax.broadcasted_iota(jnp.int32, (ng, tm), 0)
    gsel = jnp.zeros((ng, tm), jnp.bool_)
    for _ in range(TOPK_GROUPS):
        _, first = _first_max(gs, gidx, ng)
        hit = gidx == first
        gsel = jnp.logical_or(gsel, hit)
        gs = jnp.where(hit, -jnp.inf, gs)
    emask = jnp.concatenate([jnp.broadcast_to(gsel[g:g + 1, :], (per, tm)) for g in range(ng)], axis=0)
    cand = jnp.where(emask, biased, -jnp.inf)
    aidx = lax.broadcasted_iota(jnp.int32, (ne, tm), 0)
    picked, ids = [], []
    for _ in range(TOP_K):
        _, first = _first_max(cand, aidx, ne)
        hit = aidx == first
        picked.append(jnp.sum(jnp.where(hit, scores, 0.0), axis=0, keepdims=True))
        ids.append(first.astype(F32))
        cand = jnp.where(hit, -jnp.inf, cand)
    total = picked[0]
    for pk in picked[1:]:
        total = total + pk
    gates = [pk / total * ROUTED_SCALE for pk in picked]
    pad = jnp.zeros((LANES - 2 * TOP_K, tm), F32)
    o_ref[...] = jnp.concatenate(gates + ids + [pad], axis=0).T


def _router(h, router, router_bias, tm=512):
    n, d = h.shape
    ne = router.shape[0]
    return pl.pallas_call(
        functools.partial(_router_body, tm=tm),
        out_shape=jax.ShapeDtypeStruct((n, LANES), F32),
        grid=(n // tm,),
        in_specs=[pl.BlockSpec((tm, d), lambda i: (i, 0)),
                  pl.BlockSpec((ne, d), lambda i: (0, 0)),
                  pl.BlockSpec((ne, 1), lambda i: (0, 0))],
        out_specs=pl.BlockSpec((tm, LANES), lambda i: (i, 0)),
        compiler_params=_params(("parallel",), 48),
        name="moe_router",
    )(h, router.astype(F32), router_bias.astype(F32).reshape(ne, 1))


def _ffn_tile(x, wg, wu, wd, o_ref, accumulate, n_col=4):
    gte = _dot(x, wg)
    hid = ((gte * _sigmoid(gte)) * _dot(x, wu)).astype(BF16)
    d = o_ref.shape[1]
    cw = d // n_col
    for j in range(n_col):
        y = _dot(hid, wd[:, j * cw:(j + 1) * cw])
        if accumulate:
            o_ref[:, j * cw:(j + 1) * cw] += y
        else:
            o_ref[:, j * cw:(j + 1) * cw] = y


def _shared_body(x_ref, wg_ref, wu_ref, wd_ref, o_ref):
    @pl.when(pl.program_id(1) == 0)
    def _():
        o_ref[...] = jnp.zeros_like(o_ref)

    _ffn_tile(x_ref[...], wg_ref[...], wu_ref[...], wd_ref, o_ref, accumulate=True)


def _shared_ffn(hb, wg, wu, wd, tm=512, th=EXPERT_HIDDEN):
    n, d = hb.shape
    return pl.pallas_call(
        _shared_body,
        out_shape=jax.ShapeDtypeStruct((n, d), F32),
        grid=(n // tm, wg.shape[1] // th),
        in_specs=[pl.BlockSpec((tm, d), lambda i, e: (i, 0)),
                  pl.BlockSpec((d, th), lambda i, e: (0, e)),
                  pl.BlockSpec((d, th), lambda i, e: (0, e)),
                  pl.BlockSpec((th, d), lambda i, e: (e, 0))],
        out_specs=pl.BlockSpec((tm, d), lambda i, e: (i, 0)),
        compiler_params=_params(("parallel", "arbitrary"), 56),
        name="moe_shared",
    )(hb, wg, wu, wd)


MOE_ROW_TILE = 512
MOE_COMBINE_TOKENS = 128

def _start_row_gather(ids_ref, src_hbm, dst, sem, n_rows, first=0):
    for r in range(first, n_rows):
        pltpu.make_async_copy(src_hbm.at[pl.ds(ids_ref[0, 0, r], 1)], dst.at[pl.ds(r, 1)], sem).start(priority=r % 2)


def _wait_row_gather(src_hbm, dst, sem, n_rows):
    pltpu.make_async_copy(src_hbm.at[pl.ds(0, n_rows)], dst, sem).wait()


def _moe_sparse_body(te_ref, nv_ref, tok_ref, tokn_ref, h_hbm, wgf_ref, wuf_ref, wdf_ref, y_ref,
                     xbuf0, xbuf1, sem, wg_ref, wu_ref, wd_ref, *, tr, n_col):
    i = pl.program_id(0)
    nv = nv_ref[0]
    slot = i % 2

    new_expert = jnp.logical_or(i == 0, te_ref[i] != te_ref[jnp.maximum(i - 1, 0)])

    @pl.when(jnp.logical_and(i < nv, new_expert))
    def _():
        wg_ref[...] = wgf_ref[0, 0].astype(BF16)
        wu_ref[...] = wuf_ref[0, 0].astype(BF16)
        wd_ref[...] = wdf_ref[0, 0].astype(BF16)

    bufs = (xbuf0, xbuf1)

    @pl.when(i == 0)
    def _():
        _start_row_gather(tok_ref, h_hbm, xbuf0, sem.at[0], tr)

    n_batch = 4 + 2 * n_col
    per = tr // n_batch

    def tile(cur, nxt, sem_cur, sem_nxt):
        def prefetch(b):
            _start_row_gather(tokn_ref, h_hbm, nxt, sem_nxt, (b + 1) * per, first=b * per)

        _wait_row_gather(h_hbm, cur, sem_cur, tr)
        x_lo, x_hi = _unpack_halves(cur[...])
        x_lo, x_hi = x_lo.astype(BF16), x_hi.astype(BF16)
        half = x_lo.shape[1]
        prefetch(0)
        gte = _dot(x_lo, wg_ref[:half, :])
        prefetch(1)
        gte = gte + _dot(x_hi, wg_ref[half:, :])
        prefetch(2)
        up = _dot(x_lo, wu_ref[:half, :])
        prefetch(3)
        up = up + _dot(x_hi, wu_ref[half:, :])
        hid = ((gte * _sigmoid(gte)) * up).astype(BF16)
        cw = half // n_col
        for j in range(n_col):
            prefetch(4 + 2 * j)
            lo = _dot(hid, wd_ref[:, j * cw:(j + 1) * cw])
            prefetch(5 + 2 * j)
            hi = _dot(hid, wd_ref[:, half + j * cw:half + (j + 1) * cw])
            y_ref[:, j * cw:(j + 1) * cw] = pltpu.pack_elementwise([lo, hi], packed_dtype=BF16)

    for par in (0, 1):
        @pl.when(jnp.logical_and(i < nv, slot == par))
        def _(par=par):
            tile(bufs[par], bufs[1 - par], sem.at[par], sem.at[1 - par])

        @pl.when(jnp.logical_and(i == nv, slot == par))
        def _(par=par):
            _wait_row_gather(h_hbm, bufs[par], sem.at[par], tr)

    @pl.when(i >= nv)
    def _():
        y_ref[...] = jnp.zeros_like(y_ref)


def _moe_sparse(h, wg, wu, wd, layer, tok3, tile_expert, n_valid):
    n, d = h.shape
    n_tiles, _, tr = tok3.shape
    dm = 2 * d
    last = lambda i, nv: jnp.minimum(i, nv[0] - 1)
    wspec = lambda w: pl.BlockSpec((1, 1) + w.shape[2:], lambda i, te, nv: (layer, te[last(i, nv)], 0, 0))
    grid_spec = pltpu.PrefetchScalarGridSpec(
        num_scalar_prefetch=2,
        grid=(n_tiles,),
        in_specs=[pl.BlockSpec((1, 1, tr), lambda i, te, nv: (i, 0, 0), memory_space=pltpu.SMEM),
                  pl.BlockSpec((1, 1, tr), lambda i, te, nv: (jnp.minimum(i + 1, n_tiles - 1), 0, 0),
                               memory_space=pltpu.SMEM),
                  pl.BlockSpec(memory_space=pl.ANY),
                  wspec(wg), wspec(wu), wspec(wd)],
        out_specs=pl.BlockSpec((tr, d), lambda i, te, nv: (i, 0)),
        scratch_shapes=[pltpu.VMEM((tr, d), jnp.uint32), pltpu.VMEM((tr, d), jnp.uint32),
                        pltpu.SemaphoreType.DMA((2,)),
                        pltpu.VMEM(wg.shape[2:], BF16), pltpu.VMEM(wu.shape[2:], BF16),
                        pltpu.VMEM(wd.shape[2:], BF16)])
    return pl.pallas_call(
        functools.partial(_moe_sparse_body, tr=tr, n_col=2),
        out_shape=jax.ShapeDtypeStruct((n_tiles * tr, d), jnp.uint32),
        grid_spec=grid_spec,
        compiler_params=_params(("arbitrary",), 60),
        name="moe_sparse",
    )(tile_expert, n_valid, tok3, tok3, h, wg, wu, wd)


def _moe_combine_body(sl_ref, sln_ref, y_hbm, gate_ref, sh_ref, h_ref, g_ref, b_ref, o_ref, ob_ref, buf, sem,
                      *, tm, alpha, n_steps):
    i = pl.program_id(0)
    slot = i % 2
    rows = TOP_K * tm

    @pl.when(i == 0)
    def _():
        _start_row_gather(sl_ref, y_hbm, buf.at[0], sem.at[0], rows)

    @pl.when(i + 1 < n_steps)
    def _():
        _start_row_gather(sln_ref, y_hbm, buf.at[1 - slot], sem.at[1 - slot], rows)

    _wait_row_gather(y_hbm, buf.at[slot], sem.at[slot], rows)
    d = h_ref.shape[1]
    half = d // 2
    acc_lo = sh_ref[:, :half]
    acc_hi = sh_ref[:, half:]
    for k in range(TOP_K):
        lo, hi = _unpack_halves(buf[slot, k * tm:(k + 1) * tm, :])
        gk = gate_ref[:, k:k + 1]
        acc_lo = acc_lo + gk * lo
        acc_hi = acc_hi + gk * hi
    v_lo = alpha * h_ref[:, :half] + acc_lo
    v_hi = alpha * h_ref[:, half:] + acc_hi
    mu = (jnp.sum(v_lo, axis=-1, keepdims=True) + jnp.sum(v_hi, axis=-1, keepdims=True)) / d
    c_lo = v_lo - mu
    c_hi = v_hi - mu
    var = (jnp.sum(c_lo * c_lo, axis=-1, keepdims=True) + jnp.sum(c_hi * c_hi, axis=-1, keepdims=True)) / d
    rs = lax.rsqrt(var + LN_EPS)
    out_lo = c_lo * rs * g_ref[:, :half] + b_ref[:, :half]
    out_hi = c_hi * rs * g_ref[:, half:] + b_ref[:, half:]
    o_ref[:, :half] = out_lo
    o_ref[:, half:] = out_hi
    ob_ref[:, :half] = out_lo.astype(BF16)
    ob_ref[:, half:] = out_hi.astype(BF16)


def _moe_combine(y_sorted, slot3, rout, shared, h, g, b, alpha):
    n, d = h.shape
    n_steps, _, rows = slot3.shape
    tm = rows // TOP_K
    row = lambda i: (i, 0)
    return pl.pallas_call(
        functools.partial(_moe_combine_body, tm=tm, alpha=alpha, n_steps=n_steps),
        out_shape=(jax.ShapeDtypeStruct((n, d), F32), jax.ShapeDtypeStruct((n, d), BF16)),
        grid=(n_steps,),
        in_specs=[pl.BlockSpec((1, 1, rows), lambda i: (i, 0, 0), memory_space=pltpu.SMEM),
                  pl.BlockSpec((1, 1, rows), lambda i: (jnp.minimum(i + 1, n_steps - 1), 0, 0),
                               memory_space=pltpu.SMEM),
                  pl.BlockSpec(memory_space=pl.ANY),
                  pl.BlockSpec((tm, LANES), row),
                  pl.BlockSpec((tm, d), row),
                  pl.BlockSpec((tm, d), row),
                  pl.BlockSpec((1, d), lambda i: (0, 0)),
                  pl.BlockSpec((1, d), lambda i: (0, 0))],
        out_specs=(pl.BlockSpec((tm, d), row), pl.BlockSpec((tm, d), row)),
        scratch_shapes=[pltpu.VMEM((2, rows, d // 2), jnp.uint32), pltpu.SemaphoreType.DMA((2,))],
        compiler_params=_params(("arbitrary",), 48),
        name="moe_combine",
    )(slot3, slot3, y_sorted, rout, shared, h, g.reshape(1, d), b.reshape(1, d))


def _slots_body(r_ref, o_ref, se_ref, run_ref, start_ref, *, tm, tr):
    phase = pl.program_id(0)
    i = pl.program_id(1)
    lane = lax.broadcasted_iota(jnp.int32, (tm, LANES), 1)
    lane_f = lane.astype(F32)
    hits = [r_ref[:, TOP_K + k:TOP_K + k + 1] == lane_f for k in range(TOP_K)]
    onehot = jnp.zeros((tm, LANES), F32)
    for hk in hits:
        onehot = onehot + jnp.where(hk, 1.0, 0.0)

    @pl.when(jnp.logical_and(phase == 0, i == 0))
    def _():
        run_ref[...] = jnp.zeros_like(run_ref)

    @pl.when(phase == 0)
    def _():
        run_ref[...] += jnp.sum(onehot, axis=0, keepdims=True)
        o_ref[...] = jnp.zeros_like(o_ref)
        se_ref[...] = jnp.zeros_like(se_ref)

    @pl.when(jnp.logical_and(phase == 1, i == 0))
    def _():
        tiles = jnp.floor((run_ref[...] + (tr - 1)) * (1.0 / tr))
        r = lax.broadcasted_iota(jnp.int32, (LANES, LANES), 0)
        c = lax.broadcasted_iota(jnp.int32, (LANES, LANES), 1)
        before = jnp.where(r < c, 1.0, 0.0).astype(BF16)
        t8 = jnp.broadcast_to(tiles, (SUBLANES, LANES)).astype(BF16)
        start_tiles = _dot(t8, before)[0:1, :]
        start_ref[...] = start_tiles * tr
        run_ref[...] = jnp.zeros_like(run_ref)

    @pl.when(phase == 1)
    def _():
        rr = lax.broadcasted_iota(jnp.int32, (tm, tm), 0)
        cc = lax.broadcasted_iota(jnp.int32, (tm, tm), 1)
        earlier = jnp.where(rr > cc, 1.0, 0.0).astype(BF16)
        rank = _dot(earlier, onehot.astype(BF16)) + run_ref[...] + start_ref[...]
        out = jnp.zeros((tm, LANES), F32)
        for k, hk in enumerate(hits):
            slot_k = jnp.sum(jnp.where(hk, rank, 0.0), axis=1, keepdims=True)
            out = jnp.where(lane == k, slot_k, out)
        o_ref[...] = out
        run_ref[...] += jnp.sum(onehot, axis=0, keepdims=True)
        tiles = jnp.floor((run_ref[...] + (tr - 1)) * (1.0 / tr))
        se_ref[...] = start_ref[...] + tiles * tr


def _route_slots(rout, tr, tm=512):
    n = rout.shape[0]
    return pl.pallas_call(
        functools.partial(_slots_body, tm=tm, tr=tr),
        out_shape=(jax.ShapeDtypeStruct((n, LANES), F32), jax.ShapeDtypeStruct((1, LANES), F32)),
        grid=(2, n // tm),
        in_specs=[pl.BlockSpec((tm, LANES), lambda p, i: (i, 0))],
        out_specs=(pl.BlockSpec((tm, LANES), lambda p, i: (p * i, 0)),
                   pl.BlockSpec((1, LANES), lambda p, i: (0, 0))),
        scratch_shapes=[pltpu.VMEM((1, LANES), F32), pltpu.VMEM((1, LANES), F32)],
        compiler_params=_params(("arbitrary", "arbitrary"), 32),
        name="moe_slots",
    )(rout)


def _dispatch_plan(rout, n_experts, tr, tm):
    n = rout.shape[0]
    slots_f, seg_end_f = _route_slots(rout, tr)
    slot = slots_f[:, :TOP_K].astype(jnp.int32)
    seg_end = seg_end_f[0, :n_experts].astype(jnp.int32)
    n_tiles = n * TOP_K // tr + n_experts + 1
    tok = jnp.zeros((n_tiles * tr,), jnp.int32).at[slot.reshape(-1)].set(
        jnp.repeat(jnp.arange(n, dtype=jnp.int32), TOP_K), unique_indices=True)
    tile_start = jnp.arange(n_tiles, dtype=jnp.int32) * tr
    tile_expert = jnp.minimum(jnp.sum((seg_end[None, :] <= tile_start[:, None]).astype(jnp.int32), axis=1),
                              n_experts - 1)
    n_valid = (seg_end[-1:] // tr).astype(jnp.int32)
    slot3 = slot.reshape(n // tm, tm, TOP_K).transpose(0, 2, 1).reshape(n // tm, 1, TOP_K * tm)
    return tok.reshape(n_tiles, 1, tr), tile_expert, n_valid, slot3.astype(jnp.int32)


def kernel(x, mem, ev_w_in, ev_pool_w, ev_pool_scale, ev_sinks, ev_w_out, rel_bias, od_w_in, hg_lb_logits, od_hg_norm, od_a_re, od_a_im, od_log_step, od_b_re, od_b_im, od_c_re, od_c_im, od_d_skip, od_w_glu, od_b_glu, od_w_out, xa_wq, xa_wk, xa_wv, xa_wo, moe_router, moe_bias, moe_w_gate, moe_w_up, moe_w_down, sh_w_gate, sh_w_up, sh_w_down, ln_g, ln_b):
    bsz, seq, d = x.shape
    n = bsz * seq
    depth = ln_g.shape[0]
    alpha = (2 * depth) ** 0.25
    memb = mem.astype(BF16).reshape(bsz * mem.shape[1], d)

    h = x.astype(F32).reshape(n, d)
    hb = h.astype(BF16)
    bias_tab = _bias_table(rel_bias)

    for l in range(depth):
        j = l // 2
        if l % 2 == 0:
            pool_w = ev_pool_w[j]
            pool_width = pool_w.shape[0] * pool_w.shape[1]
            n_q = ev_sinks.shape[1]
            n_kv = n_q // SWA_GQ
            z = _mm(hb, ev_w_in[j].astype(BF16), F32)
            z3 = z.reshape(bsz, seq, z.shape[1])
            y_a = _pool(z3, pool_w.astype(BF16), ev_pool_scale[j].astype(F32))
            y_b = _swa(z3, ev_sinks[j], bias_tab, pool_width, n_q, n_kv)
            mix = _mm2(y_a.reshape(n, -1), y_b.reshape(n, -1), ev_w_out[j].astype(BF16), F32)
        else:
            n_heads = hg_lb_logits.shape[1] // HG_DIM
            hgw = n_heads * HG_DIM
            ng, nst = od_a_re.shape[1], od_a_re.shape[2]
            s5w = ng * S5_GROUP
            z = _mm(hb, od_w_in[j].astype(BF16), F32)
            z3 = z.reshape(bsz, seq, z.shape[1])
            y_c = _hgrn(z3, hg_lb_logits, od_hg_norm[j], l, n_heads)
            m_mat, p_mat, q_mat, t_re, t_im = _s5_prep(od_a_re[j], od_a_im[j], od_log_step[j], od_b_re[j],
                                                        od_b_im[j], od_c_re[j], od_c_im[j])
            nck = seq // S5_CHUNK
            u = z3[:, :, 4 * hgw:].astype(BF16).reshape(bsz, nck, S5_CHUNK, ng, S5_GROUP)
            ug = u.transpose(3, 0, 1, 2, 4).reshape(ng, bsz, nck, S5_CHUNK * S5_GROUP)
            yg = _s5_scan(ug, m_mat, p_mat, q_mat, t_re, t_im)
            y_s = yg.reshape(ng, bsz, nck, S5_CHUNK, S5_GROUP).transpose(1, 2, 3, 0, 4).reshape(n, s5w)
            y_d = _glu(y_s, z, (4 * hgw) // s5w, od_d_skip[j], od_w_glu[j], od_b_glu[j])
            mix = _mm2(y_c.reshape(n, hgw), y_d, od_w_out[j].astype(BF16), F32)
        h, hb = _add_ln(h, mix, ln_g[l, 0], ln_b[l, 0], alpha)

        xw = xa_wq.shape[2]
        kmem = _mm(memb, xa_wk[l].astype(BF16), BF16).reshape(bsz, -1, xw)
        vmem = _mm(memb, xa_wv[l].astype(BF16), BF16).reshape(bsz, -1, xw)
        h, hb, hp = _xattn(h, hb, kmem, vmem, xa_wq[l].astype(BF16), xa_wo[l].astype(BF16),
                           ln_g[l, 1], ln_b[l, 1], alpha, seq)

        n_exp = moe_router.shape[1]
        rout = _router(h, moe_router[l], moe_bias[l])
        tok3, tile_expert, n_valid, slot3 = _dispatch_plan(rout, n_exp, MOE_ROW_TILE, MOE_COMBINE_TOKENS)
        shared = _shared_ffn(hb, sh_w_gate[l].astype(BF16), sh_w_up[l].astype(BF16), sh_w_down[l].astype(BF16))
        y_sorted = _moe_sparse(hp, moe_w_gate.astype(F32), moe_w_up.astype(F32), moe_w_down.astype(F32), l,
                               tok3, tile_expert, n_valid)
        h, hb = _moe_combine(y_sorted, slot3, rout, shared, h, ln_g[l, 2], ln_b[l, 2], alpha)

    return h.reshape(bsz, seq, d)
```

```python
import functools
import math

import jax
import jax.numpy as jnp
from jax import lax
from jax.experimental import pallas as pl
from jax.experimental.pallas import tpu as pltpu

F32 = jnp.float32
BF16 = jnp.bfloat16

V7X_VMEM_BYTES = 64 * 1024 * 1024
LANES = 128
SUBLANES = 8

POOL_WINDOWS = (2, 4, 8, 16)
SWA_HEAD_DIM = 64
SWA_GQ = 8
SWA_BLOCK = 128
REL_BUCKETS = 32
REL_MAX_DIST = 128
HG_DIM = 128
HG_CHUNK = 64
RMS_EPS = 1e-6
S5_GROUP = 16
S5_STATE = 64
S5_CHUNK = 16
XA_HEADS = 4
XA_HEAD_DIM = 128
N_EXPERTS = 64
TOP_K = 8
N_EXPERT_GROUPS = 8
TOPK_GROUPS = 4
EXPERT_HIDDEN = 256
ROUTED_SCALE = 2.5
LN_EPS = 1e-5


def _params(sem, vmem_mb):
    return pltpu.CompilerParams(dimension_semantics=sem, vmem_limit_bytes=vmem_mb * 1024 * 1024)


def _nt(a, b):
    return lax.dot_general(a, b, (((1,), (1,)), ((), ())), preferred_element_type=F32)


def _tn(a, b):
    return lax.dot_general(a, b, (((0,), (0,)), ((), ())), preferred_element_type=F32)


def _dot(a, b):
    return jnp.dot(a, b, preferred_element_type=F32)


def _sigmoid(x):
    return 0.5 * (jnp.tanh(0.5 * x) + 1.0)


def _mm_body(x_ref, w_ref, o_ref):
    o_ref[...] = _dot(x_ref[...], w_ref[...]).astype(o_ref.dtype)


def _mm(x, w, out_dtype, tm=1024, tn=512):
    m, k = x.shape
    n = w.shape[1]
    tm = min(tm, m)
    tn = min(tn, n)
    return pl.pallas_call(
        _mm_body,
        out_shape=jax.ShapeDtypeStruct((m, n), out_dtype),
        grid=(m // tm, n // tn),
        in_specs=[pl.BlockSpec((tm, k), lambda i, j: (i, 0)),
                  pl.BlockSpec((k, tn), lambda i, j: (0, j))],
        out_specs=pl.BlockSpec((tm, tn), lambda i, j: (i, j)),
        compiler_params=_params(("parallel", "parallel"), 48),
        name="mm",
    )(x, w)


def _mm2_body(a_ref, b_ref, w_ref, o_ref):
    ka = a_ref.shape[1]
    o_ref[...] = (_dot(a_ref[...], w_ref[:ka, :]) + _dot(b_ref[...], w_ref[ka:, :])).astype(o_ref.dtype)


def _mm2(a, b, w, out_dtype, tm=1024, tn=512):
    m, ka = a.shape
    kb = b.shape[1]
    n = w.shape[1]
    return pl.pallas_call(
        _mm2_body,
        out_shape=jax.ShapeDtypeStruct((m, n), out_dtype),
        grid=(m // tm, n // tn),
        in_specs=[pl.BlockSpec((tm, ka), lambda i, j: (i, 0)),
                  pl.BlockSpec((tm, kb), lambda i, j: (i, 0)),
                  pl.BlockSpec((ka + kb, tn), lambda i, j: (0, j))],
        out_specs=pl.BlockSpec((tm, tn), lambda i, j: (i, j)),
        compiler_params=_params(("parallel", "parallel"), 48),
        name="mm2",
    )(a, b, w)


def _add_ln_body(h_ref, y_ref, g_ref, b_ref, o_ref, ob_ref, *, alpha):
    v = alpha * h_ref[...] + y_ref[...].astype(F32)
    mu = jnp.mean(v, axis=-1, keepdims=True)
    c = v - mu
    var = jnp.mean(c * c, axis=-1, keepdims=True)
    out = c * lax.rsqrt(var + LN_EPS) * g_ref[...] + b_ref[...]
    o_ref[...] = out
    ob_ref[...] = out.astype(BF16)


def _add_ln(h, y, g, b, alpha, tm=256):
    n, d = h.shape
    return pl.pallas_call(
        functools.partial(_add_ln_body, alpha=alpha),
        out_shape=(jax.ShapeDtypeStruct((n, d), F32), jax.ShapeDtypeStruct((n, d), BF16)),
        grid=(n // tm,),
        in_specs=[pl.BlockSpec((tm, d), lambda i: (i, 0)),
                  pl.BlockSpec((tm, d), lambda i: (i, 0)),
                  pl.BlockSpec((1, d), lambda i: (0, 0)),
                  pl.BlockSpec((1, d), lambda i: (0, 0))],
        out_specs=(pl.BlockSpec((tm, d), lambda i: (i, 0)),
                   pl.BlockSpec((tm, d), lambda i: (i, 0))),
        compiler_params=_params(("parallel",), 48),
        name="add_ln",
    )(h, y, g.reshape(1, d), b.reshape(1, d))


def _pool_body(u_ref, prev_ref, w_ref, sc_ref, o_ref, *, ts, gw, halo):
    i = pl.program_id(1)
    row = lax.broadcasted_iota(jnp.int32, (ts, gw), 0)
    pos1 = (row + i * ts + 1).astype(F32)
    for g, win in enumerate(POOL_WINDOWS):
        u = u_ref[0, :, g * gw:(g + 1) * gw]
        prev = prev_ref[0, :, g * gw:(g + 1) * gw]
        prev = jnp.where(i == 0, 0.0, prev)
        s = jnp.concatenate([prev, u], axis=0)
        step = 1
        while step < win:
            s = s + pltpu.roll(s, step, axis=0)
            step *= 2
        wsum = s[halo:, :]
        pooled = wsum / jnp.minimum(pos1, float(win)) - u
        y = _dot(pooled.astype(BF16), w_ref[g])
        o_ref[0, :, g * gw:(g + 1) * gw] = (y * sc_ref[:, g * gw:(g + 1) * gw]).astype(o_ref.dtype)


def _pool(z3, pool_w, pool_scale, ts=512):
    bsz, s, _ = z3.shape
    ng, gw, _ = pool_w.shape
    width = ng * gw
    halo = max(POOL_WINDOWS)
    return pl.pallas_call(
        functools.partial(_pool_body, ts=ts, gw=gw, halo=halo),
        out_shape=jax.ShapeDtypeStruct((bsz, s, width), BF16),
        grid=(bsz, s // ts),
        in_specs=[pl.BlockSpec((1, ts, width), lambda b, i: (b, i, 0)),
                  pl.BlockSpec((1, halo, width), lambda b, i: (b, jnp.maximum(i * (ts // halo) - 1, 0), 0)),
                  pl.BlockSpec((ng, gw, gw), lambda b, i: (0, 0, 0)),
                  pl.BlockSpec((1, width), lambda b, i: (0, 0))],
        out_specs=pl.BlockSpec((1, ts, width), lambda b, i: (b, i, 0)),
        compiler_params=_params(("parallel", "parallel"), 48),
        name="pool",
    )(z3, z3, pool_w, pool_scale.reshape(1, width))


def _t5_bucket_table():
    qi = jnp.arange(SWA_BLOCK)[:, None]
    kj = jnp.arange(2 * SWA_BLOCK)[None, :]
    dist = qi + SWA_BLOCK - kj
    band = (dist >= 0) & (dist < SWA_BLOCK)
    d = jnp.maximum(dist, 0)
    exact = REL_BUCKETS // 2
    far = exact + (jnp.log(jnp.maximum(d, exact).astype(F32) / exact)
                   / math.log(REL_MAX_DIST / exact) * (REL_BUCKETS - exact)).astype(jnp.int32)
    bucket = jnp.where(d < exact, d, jnp.minimum(far, REL_BUCKETS - 1))
    return jnp.where(band, bucket, -1).astype(jnp.int32)


def _bias_body(rb_ref, bucket_ref, o_ref):
    h = pl.program_id(0)
    bucket = bucket_ref[...]
    acc = jnp.full(bucket.shape, -jnp.inf, F32)
    for b in range(REL_BUCKETS):
        acc = jnp.where(bucket == b, rb_ref[b, h], acc)
    o_ref[0] = acc


def _bias_table(rel_bias):
    nb, nh = rel_bias.shape
    bucket = _t5_bucket_table()
    return pl.pallas_call(
        _bias_body,
        out_shape=jax.ShapeDtypeStruct((nh, SWA_BLOCK, 2 * SWA_BLOCK), F32),
        grid=(nh,),
        in_specs=[pl.BlockSpec(memory_space=pltpu.SMEM),
                  pl.BlockSpec((SWA_BLOCK, 2 * SWA_BLOCK), lambda h: (0, 0))],
        out_specs=pl.BlockSpec((1, SWA_BLOCK, 2 * SWA_BLOCK), lambda h: (h, 0, 0)),
        compiler_params=_params(("parallel",), 32),
        name="swa_bias",
    )(rel_bias.astype(F32), bucket)


def _swa_body(sink_ref, q_ref, kp_ref, kc_ref, vp_ref, vc_ref, bias_ref, o_ref, *, n_kv):
    n = pl.program_id(1)
    dh = SWA_HEAD_DIM
    col = lax.broadcasted_iota(jnp.int32, (SWA_BLOCK, 2 * SWA_BLOCK), 1)
    no_prev = jnp.logical_and(n == 0, col < SWA_BLOCK)
    scale = dh ** -0.5
    for kv in range(n_kv):
        sl = slice(kv * dh, (kv + 1) * dh)
        k2 = jnp.concatenate([kp_ref[0, :, sl], kc_ref[0, :, sl]], axis=0).astype(BF16)
        v2 = jnp.concatenate([vp_ref[0, :, sl], vc_ref[0, :, sl]], axis=0).astype(BF16)
        for g in range(SWA_GQ):
            h = kv * SWA_GQ + g
            qh = q_ref[0, :, h * dh:(h + 1) * dh].astype(BF16)
            logits = _nt(qh, k2) * scale + bias_ref[h]
            logits = jnp.where(no_prev, -jnp.inf, logits)
            sink = sink_ref[h]
            m = jnp.maximum(jnp.max(logits, axis=-1, keepdims=True), sink)
            p = jnp.exp(logits - m)
            p = p / (jnp.sum(p, axis=-1, keepdims=True) + jnp.exp(sink - m))
            o_ref[0, :, h * dh:(h + 1) * dh] = _dot(p.astype(BF16), v2).astype(o_ref.dtype)


def _swa(z3, sinks, bias_tab, q_off, n_q, n_kv):
    bsz, s, _ = z3.shape
    dh = SWA_HEAD_DIM
    qw, kvw = n_q * dh, n_kv * dh
    qb = q_off // qw
    kb = (q_off + qw) // kvw
    vb = kb + 1
    blk = SWA_BLOCK
    prev = lambda n: jnp.maximum(n - 1, 0)
    return pl.pallas_call(
        functools.partial(_swa_body, n_kv=n_kv),
        out_shape=jax.ShapeDtypeStruct((bsz, s, qw), BF16),
        grid=(bsz, s // blk),
        in_specs=[pl.BlockSpec(memory_space=pltpu.SMEM),
                  pl.BlockSpec((1, blk, qw), lambda b, n: (b, n, qb)),
                  pl.BlockSpec((1, blk, kvw), lambda b, n: (b, prev(n), kb)),
                  pl.BlockSpec((1, blk, kvw), lambda b, n: (b, n, kb)),
                  pl.BlockSpec((1, blk, kvw), lambda b, n: (b, prev(n), vb)),
                  pl.BlockSpec((1, blk, kvw), lambda b, n: (b, n, vb)),
                  pl.BlockSpec((n_q, blk, 2 * blk), lambda b, n: (0, 0, 0))],
        out_specs=pl.BlockSpec((1, blk, qw), lambda b, n: (b, n, 0)),
        compiler_params=_params(("parallel", "parallel"), 48),
        name="swa",
    )(sinks.astype(F32), z3, z3, z3, z3, z3, bias_tab)


def _hgrn_body(q_ref, f_ref, i_ref, g_ref, lbl_ref, nw_ref, o_ref, st_ref, *, n_chunk, layer):
    @pl.when(pl.program_id(2) == 0)
    def _():
        st_ref[...] = jnp.zeros_like(st_ref)

    lg = lbl_ref[:, 0, 0, :]
    e = jnp.exp(lg - jnp.max(lg, axis=0, keepdims=True))
    p = e / jnp.sum(e, axis=0, keepdims=True)
    lb = jnp.sum(p[:layer + 1], axis=0, keepdims=True) - p[0:1]
    nw = nw_ref[...]
    c = HG_CHUNK
    r = lax.broadcasted_iota(jnp.int32, (c, c), 0)
    cc = lax.broadcasted_iota(jnp.int32, (c, c), 1)
    causal = r >= cc
    tri = jnp.where(causal, 1.0, 0.0).astype(BF16)

    d = HG_DIM
    chunks = lambda t: [t[ci * c:(ci + 1) * c] for ci in range(n_chunk)]
    q = q_ref[0]
    iv = chunks(i_ref[0].astype(BF16))
    g = g_ref[0]
    qf = q * _sigmoid(q)
    forget = lb + (1.0 - lb) * _sigmoid(f_ref[0])
    log_f = jnp.log(forget)
    k_in = 1.0 - forget
    hi = log_f.astype(BF16)
    r1 = log_f - hi.astype(F32)
    mid = r1.astype(BF16)
    lo = (r1 - mid.astype(F32)).astype(BF16)
    cum = jnp.concatenate([_dot(tri, a) + _dot(tri, b) + _dot(tri, e3)
                           for a, b, e3 in zip(chunks(hi), chunks(mid), chunks(lo))], axis=0)
    cum3 = cum.reshape(n_chunk, c, d)
    ref = cum3[:, c // 2:c // 2 + 1, :]
    total = cum3[:, c - 1:c, :]
    qa = qf.reshape(n_chunk, c, d) * jnp.exp(cum3 - ref)
    kb = k_in.reshape(n_chunk, c, d) * jnp.exp(ref - cum3)
    qe = chunks((qa * jnp.exp(ref)).astype(BF16).reshape(n_chunk * c, d))
    ku = chunks((kb * jnp.exp(total - ref)).astype(BF16).reshape(n_chunk * c, d))
    qa = chunks(qa.astype(BF16).reshape(n_chunk * c, d))
    kb = chunks(kb.astype(BF16).reshape(n_chunk * c, d))
    decay = jnp.exp(total)
    st = st_ref[...]
    outs = []
    for ci in range(n_chunk):
        scores = jnp.where(causal, _nt(qa[ci], kb[ci]), 0.0)
        o_intra = _dot(scores.astype(BF16), iv[ci])
        o_inter = _nt(qe[ci], st.astype(BF16))
        st = st * decay[ci] + _tn(iv[ci], ku[ci])
        outs.append(o_intra + o_inter)
    st_ref[...] = st
    o = jnp.concatenate(outs, axis=0)
    o = o * lax.rsqrt(jnp.mean(o * o, axis=-1, keepdims=True) + RMS_EPS) * nw
    o_ref[0] = (o * (g * _sigmoid(g))).astype(o_ref.dtype)


def _hgrn(z3, lb_logits, norm_w, layer, n_heads, ts=512):
    bsz, s, _ = z3.shape
    depth = lb_logits.shape[0]
    d = HG_DIM
    spec = lambda off: pl.BlockSpec((1, ts, d), lambda b, h, t: (b, t, off + h))
    return pl.pallas_call(
        functools.partial(_hgrn_body, n_chunk=ts // HG_CHUNK, layer=layer),
        out_shape=jax.ShapeDtypeStruct((bsz, s, n_heads * d), BF16),
        grid=(bsz, n_heads, s // ts),
        in_specs=[spec(0), spec(n_heads), spec(2 * n_heads), spec(3 * n_heads),
                  pl.BlockSpec((depth, 1, 1, d), lambda b, h, t: (0, h, 0, 0)),
                  pl.BlockSpec((1, d), lambda b, h, t: (0, 0))],
        out_specs=pl.BlockSpec((1, ts, d), lambda b, h, t: (b, t, h)),
        scratch_shapes=[pltpu.VMEM((d, d), F32)],
        compiler_params=_params(("parallel", "parallel", "arbitrary"), 32),
        name="hgrn2",
    )(z3, z3, z3, z3, lb_logits.astype(F32).reshape(depth, n_heads, 1, d), norm_w.astype(F32).reshape(1, d))


def _cpow(pw, lam_re, lam_im):
    mag = jnp.exp(pw * lam_re)
    ang = pw * lam_im
    return mag * jnp.cos(ang), mag * jnp.sin(ang)


def _s5_prep_body(ls_ref, ar_ref, ai_ref, arc_ref, aic_ref, br_ref, bi_ref, cr_ref, ci_ref,
                  m_ref, p_ref, q_ref, tr_ref, ti_ref):
    nst = S5_STATE
    lc = S5_CHUNK
    gc = S5_GROUP
    w = lc * gc
    dt = jnp.exp(ls_ref[0])
    a_re, a_im = ar_ref[0], ai_ref[0]
    lam_re, lam_im = dt * a_re, dt * a_im
    ab_re, ab_im = _cpow(1.0, lam_re, lam_im)
    den = a_re * a_re + a_im * a_im
    co_re = ((ab_re - 1.0) * a_re + ab_im * a_im) / den
    co_im = (ab_im * a_re - (ab_re - 1.0) * a_im) / den
    b_re, b_im = br_ref[0], bi_ref[0]
    bb_re = co_re * b_re - co_im * b_im
    bb_im = co_re * b_im + co_im * b_re
    lane = lax.broadcasted_iota(jnp.int32, (w, 2 * nst), 1)
    rowi = lax.broadcasted_iota(jnp.int32, (w, 2 * nst), 0)

    pw = (lc - 1 - rowi // gc).astype(F32)
    ap_re, ap_im = _cpow(pw, lam_re, lam_im)
    tb_re = jnp.concatenate([bb_re] * lc, axis=0)
    tb_im = jnp.concatenate([bb_im] * lc, axis=0)
    p_ref[0] = jnp.where(lane < nst, ap_re * tb_re - ap_im * tb_im,
                         ap_re * tb_im + ap_im * tb_re).astype(p_ref.dtype)

    lam_re_c, lam_im_c = dt * arc_ref[0], dt * aic_ref[0]
    c_re, c_im = cr_ref[0], ci_ref[0]
    lane_q = lax.broadcasted_iota(jnp.int32, (2 * nst, w), 1)
    row_q = lax.broadcasted_iota(jnp.int32, (2 * nst, w), 0)
    tq = (lane_q // gc).astype(F32)

    def c_times_apow(pwq):
        pr, pi = _cpow(pwq, lam_re_c, lam_im_c)
        return jnp.where(row_q < nst, c_re * pr - c_im * pi, -(c_re * pi + c_im * pr))

    q_ref[0] = c_times_apow(tq + 1.0).astype(q_ref.dtype)
    wmat = c_times_apow(tq)

    bbt = jnp.where(lax.broadcasted_iota(jnp.int32, (gc, 2 * nst), 1) < nst, bb_re, bb_im)
    r = jnp.dot(bbt, wmat, preferred_element_type=F32, precision=lax.Precision.HIGHEST)
    lane_r = lax.broadcasted_iota(jnp.int32, (gc, w), 1)
    for t in range(lc):
        blk = r if t == 0 else jnp.where(lane_r >= t * gc, pltpu.roll(r, t * gc, axis=1), 0.0)
        m_ref[0, t * gc:(t + 1) * gc, :] = blk.astype(m_ref.dtype)

    kk = lax.broadcasted_iota(jnp.int32, (16, 2 * nst), 0)
    pws = (lc * jnp.left_shift(1, jnp.minimum(kk, 12))).astype(F32)
    sr, si = _cpow(pws, lam_re, lam_im)
    tr_ref[0] = sr
    ti_ref[0] = jnp.where(lax.broadcasted_iota(jnp.int32, (16, 2 * nst), 1) < nst, -si, si)


def _s5_prep(a_re, a_im, log_step, b_re, b_im, c_re, c_im):
    ng, nst = a_re.shape
    gc, lc = S5_GROUP, S5_CHUNK
    w = lc * gc
    dup = lambda t: jnp.concatenate([t, t], axis=-1)
    ar = dup(a_re.astype(F32))
    ai = dup(a_im.astype(F32))
    brt = dup(jnp.swapaxes(b_re.astype(F32), 1, 2))
    bit = dup(jnp.swapaxes(b_im.astype(F32), 1, 2))
    tile_c = lambda t: jnp.tile(jnp.concatenate([jnp.swapaxes(t.astype(F32), 1, 2)] * 2, axis=1), (1, 1, lc))
    crt, cit = tile_c(c_re), tile_c(c_im)
    g3 = lambda blk: pl.BlockSpec((1,) + blk, lambda g: (g, 0, 0))
    return pl.pallas_call(
        _s5_prep_body,
        out_shape=(jax.ShapeDtypeStruct((ng, w, w), BF16),
                   jax.ShapeDtypeStruct((ng, w, 2 * nst), BF16),
                   jax.ShapeDtypeStruct((ng, 2 * nst, w), BF16),
                   jax.ShapeDtypeStruct((ng, 16, 2 * nst), F32),
                   jax.ShapeDtypeStruct((ng, 16, 2 * nst), F32)),
        grid=(ng,),
        in_specs=[g3((1, 1)), g3((1, 2 * nst)), g3((1, 2 * nst)), g3((2 * nst, 1)), g3((2 * nst, 1)),
                  g3((gc, 2 * nst)), g3((gc, 2 * nst)), g3((2 * nst, w)), g3((2 * nst, w))],
        out_specs=(g3((w, w)), g3((w, 2 * nst)), g3((2 * nst, w)), g3((16, 2 * nst)), g3((16, 2 * nst))),
        compiler_params=_params(("parallel",), 32),
        name="s5_prep",
    )(log_step.astype(F32).reshape(ng, 1, 1), ar.reshape(ng, 1, 2 * nst), ai.reshape(ng, 1, 2 * nst),
      ar.reshape(ng, 2 * nst, 1), ai.reshape(ng, 2 * nst, 1), brt, bit, crt, cit)


def _s5_scan_body(u_ref, m_ref, p_ref, q_ref, tr_ref, ti_ref, y_ref, *, n_steps):
    nst = S5_STATE
    u = u_ref[0, 0]
    x = _dot(u, p_ref[0])
    row = lax.broadcasted_iota(jnp.int32, x.shape, 0)
    tr = tr_ref[0]
    ti = ti_ref[0]
    for k in range(n_steps):
        s = 1 << k
        sh = jnp.where(row >= s, pltpu.roll(x, s, axis=0), 0.0)
        x = x + tr[k:k + 1, :] * sh + ti[k:k + 1, :] * pltpu.roll(sh, nst, axis=1)
    x_in = jnp.where(row >= 1, pltpu.roll(x, 1, axis=0), 0.0)
    y_ref[0, 0] = _dot(u, m_ref[0]) + _dot(x_in.astype(BF16), q_ref[0])


def _s5_scan(ug, m, p, q, tr, ti):
    ng, bsz, nck, w = ug.shape
    nst2 = p.shape[-1]
    n_steps = int(math.log2(nck))
    assert (1 << n_steps) == nck
    g3 = lambda blk: pl.BlockSpec((1,) + blk, lambda g, b: (g, 0, 0))
    return pl.pallas_call(
        functools.partial(_s5_scan_body, n_steps=n_steps),
        out_shape=jax.ShapeDtypeStruct((ng, bsz, nck, w), F32),
        grid=(ng, bsz),
        in_specs=[pl.BlockSpec((1, 1, nck, w), lambda g, b: (g, b, 0, 0)),
                  g3((w, w)), g3((w, nst2)), g3((nst2, w)), g3((16, nst2)), g3((16, nst2))],
        out_specs=pl.BlockSpec((1, 1, nck, w), lambda g, b: (g, b, 0, 0)),
        compiler_params=_params(("parallel", "parallel"), 32),
        name="s5_scan",
    )(ug, m, p, q, tr, ti)


def _glu_body(y_ref, u_ref, d_ref, w_ref, b_ref, o_ref):
    yy = y_ref[...] + d_ref[...] * u_ref[...]
    z = 0.5 * yy * (1.0 + jnp.tanh(math.sqrt(2.0 / math.pi) * (yy + 0.044715 * (yy * yy * yy))))
    gate = _sigmoid(_dot(z.astype(BF16), w_ref[...]) + b_ref[...])
    o_ref[...] = (z * gate).astype(o_ref.dtype)


def _glu(y, z2, u_blk, d_skip, w_glu, b_glu, tm=512):
    n, wd = y.shape
    return pl.pallas_call(
        _glu_body,
        out_shape=jax.ShapeDtypeStruct((n, wd), BF16),
        grid=(n // tm,),
        in_specs=[pl.BlockSpec((tm, wd), lambda i: (i, 0)),
                  pl.BlockSpec((tm, wd), lambda i: (i, u_blk)),
                  pl.BlockSpec((1, wd), lambda i: (0, 0)),
                  pl.BlockSpec((wd, wd), lambda i: (0, 0)),
                  pl.BlockSpec((1, wd), lambda i: (0, 0))],
        out_specs=pl.BlockSpec((tm, wd), lambda i: (i, 0)),
        compiler_params=_params(("parallel",), 32),
        name="s5_glu",
    )(y, z2, d_skip.astype(F32).reshape(1, wd), w_glu.astype(BF16), b_glu.astype(F32).reshape(1, wd))


def _pack_halves(v):
    c = v.shape[1] // 2
    return pltpu.pack_elementwise([v[:, :c], v[:, c:]], packed_dtype=BF16)


def _unpack_halves(w):
    return tuple(pltpu.unpack_elementwise(w, index=i, packed_dtype=BF16, unpacked_dtype=F32) for i in (0, 1))


def _xattn_body(hb_ref, h_ref, wq_ref, k_ref, v_ref, wo_ref, g_ref, b_ref, o_ref, ob_ref, op_ref, *, alpha):
    d = XA_HEAD_DIM
    q = _dot(hb_ref[...], wq_ref[...]).astype(BF16)
    outs = []
    for hd in range(XA_HEADS):
        sl = slice(hd * d, (hd + 1) * d)
        logits = _nt(q[:, sl], k_ref[0, :, sl]) * (d ** -0.5)
        m = jnp.max(logits, axis=-1, keepdims=True)
        p = jnp.exp(logits - m)
        p = p / jnp.sum(p, axis=-1, keepdims=True)
        outs.append(_dot(p.astype(BF16), v_ref[0, :, sl]))
    o = jnp.concatenate(outs, axis=-1).astype(BF16)
    v = alpha * h_ref[...] + _dot(o, wo_ref[...])
    mu = jnp.mean(v, axis=-1, keepdims=True)
    c = v - mu
    var = jnp.mean(c * c, axis=-1, keepdims=True)
    out = c * lax.rsqrt(var + LN_EPS) * g_ref[...] + b_ref[...]
    o_ref[...] = out
    ob_ref[...] = out.astype(BF16)
    op_ref[...] = _pack_halves(out)


def _xattn(h, hb, kmem, vmem, wq, wo, g, b, alpha, seq, tm=256):
    n, d = h.shape
    xw = wq.shape[1]
    ml = kmem.shape[1]
    per_b = seq // tm
    return pl.pallas_call(
        functools.partial(_xattn_body, alpha=alpha),
        out_shape=(jax.ShapeDtypeStruct((n, d), F32), jax.ShapeDtypeStruct((n, d), BF16),
                   jax.ShapeDtypeStruct((n, d // 2), jnp.uint32)),
        grid=(n // tm,),
        in_specs=[pl.BlockSpec((tm, d), lambda i: (i, 0)),
                  pl.BlockSpec((tm, d), lambda i: (i, 0)),
                  pl.BlockSpec((d, xw), lambda i: (0, 0)),
                  pl.BlockSpec((1, ml, xw), lambda i: (i // per_b, 0, 0)),
                  pl.BlockSpec((1, ml, xw), lambda i: (i // per_b, 0, 0)),
                  pl.BlockSpec((xw, d), lambda i: (0, 0)),
                  pl.BlockSpec((1, d), lambda i: (0, 0)),
                  pl.BlockSpec((1, d), lambda i: (0, 0))],
        out_specs=(pl.BlockSpec((tm, d), lambda i: (i, 0)),
                   pl.BlockSpec((tm, d), lambda i: (i, 0)),
                   pl.BlockSpec((tm, d // 2), lambda i: (i, 0))),
        compiler_params=_params(("parallel",), 56),
        name="xattn",
    )(hb, h, wq, kmem, vmem, wo, g.reshape(1, d), b.reshape(1, d))


def _first_max(v, idx, n):
    m = jnp.max(v, axis=0, keepdims=True)
    first = jnp.min(jnp.where(v == m, idx, n), axis=0, keepdims=True)
    return m, first


def _router_body(x_ref, r_ref, rb_ref, o_ref, *, tm):
    ne, ng = N_EXPERTS, N_EXPERT_GROUPS
    per = ne // ng
    logits = _nt(r_ref[...], x_ref[...])
    scores = _sigmoid(logits)
    biased = scores + rb_ref[...]
    gsc = []
    eidx = lax.broadcasted_iota(jnp.int32, (per, tm), 0)
    for g in range(ng):
        v = biased[g * per:(g + 1) * per, :]
        m1, i1 = _first_max(v, eidx, per)
        m2 = jnp.max(jnp.where(eidx == i1, -jnp.inf, v), axis=0, keepdims=True)
        gsc.append(m1 + m2)
    gs = jnp.concatenate(gsc, axis=0)
    gidx = lax.broadcasted_iota(jnp.int32, (ng, tm), 0)
    gsel = jnp.zeros((ng, tm), jnp.bool_)
    for _ in range(TOPK_GROUPS):
        _, first = _first_max(gs, gidx, ng)
        hit = gidx == first
        gsel = jnp.logical_or(gsel, hit)
        gs = jnp.where(hit, -jnp.inf, gs)
    emask = jnp.concatenate([jnp.broadcast_to(gsel[g:g + 1, :], (per, tm)) for g in range(ng)], axis=0)
    cand = jnp.where(emask, biased, -jnp.inf)
    aidx = lax.broadcasted_iota(jnp.int32, (ne, tm), 0)
    picked, ids = [], []
    for _ in range(TOP_K):
        _, first = _first_max(cand, aidx, ne)
        hit = aidx == first
        picked.append(jnp.sum(jnp.where(hit, scores, 0.0), axis=0, keepdims=True))
        ids.append(first.astype(F32))
        cand = jnp.where(hit, -jnp.inf, cand)
    total = picked[0]
    for pk in picked[1:]:
        total = total + pk
    gates = [pk / total * ROUTED_SCALE for pk in picked]
    pad = jnp.zeros((LANES - 2 * TOP_K, tm), F32)
    o_ref[...] = jnp.concatenate(gates + ids + [pad], axis=0).T


def _router(h, router, router_bias, tm=512):
    n, d = h.shape
    ne = router.shape[0]
    return pl.pallas_call(
        functools.partial(_router_body, tm=tm),
        out_shape=jax.ShapeDtypeStruct((n, LANES), F32),
        grid=(n // tm,),
        in_specs=[pl.BlockSpec((tm, d), lambda i: (i, 0)),
                  pl.BlockSpec((ne, d), lambda i: (0, 0)),
                  pl.BlockSpec((ne, 1), lambda i: (0, 0))],
        out_specs=pl.BlockSpec((tm, LANES), lambda i: (i, 0)),
        compiler_params=_params(("parallel",), 48),
        name="moe_router",
    )(h, router.astype(BF16), router_bias.astype(F32).reshape(ne, 1))


def _ffn_tile(x, wg, wu, wd, o_ref, accumulate, n_col=4):
    gte = _dot(x, wg)
    hid = ((gte * _sigmoid(gte)) * _dot(x, wu)).astype(BF16)
    d = o_ref.shape[1]
    cw = d // n_col
    for j in range(n_col):
        y = _dot(hid, wd[:, j * cw:(j + 1) * cw])
        if accumulate:
            o_ref[:, j * cw:(j + 1) * cw] += y
        else:
            o_ref[:, j * cw:(j + 1) * cw] = y


def _shared_body(x_ref, wg_ref, wu_ref, wd_ref, o_ref):
    @pl.when(pl.program_id(1) == 0)
    def _():
        o_ref[...] = jnp.zeros_like(o_ref)

    _ffn_tile(x_ref[...], wg_ref[...], wu_ref[...], wd_ref, o_ref, accumulate=True)


def _shared_ffn(hb, wg, wu, wd, tm=512, th=EXPERT_HIDDEN):
    n, d = hb.shape
    return pl.pallas_call(
        _shared_body,
        out_shape=jax.ShapeDtypeStruct((n, d), F32),
        grid=(n // tm, wg.shape[1] // th),
        in_specs=[pl.BlockSpec((tm, d), lambda i, e: (i, 0)),
                  pl.BlockSpec((d, th), lambda i, e: (0, e)),
                  pl.BlockSpec((d, th), lambda i, e: (0, e)),
                  pl.BlockSpec((th, d), lambda i, e: (e, 0))],
        out_specs=pl.BlockSpec((tm, d), lambda i, e: (i, 0)),
        compiler_params=_params(("parallel", "arbitrary"), 56),
        name="moe_shared",
    )(hb, wg, wu, wd)


MOE_ROW_TILE = 512
MOE_COMBINE_TOKENS = 128

def _start_row_gather(ids_ref, src_hbm, dst, sem, n_rows, first=0):
    for r in range(first, n_rows):
        pltpu.make_async_copy(src_hbm.at[pl.ds(ids_ref[0, 0, r], 1)], dst.at[pl.ds(r, 1)], sem).start(priority=r % 2)


def _wait_row_gather(src_hbm, dst, sem, n_rows):
    pltpu.make_async_copy(src_hbm.at[pl.ds(0, n_rows)], dst, sem).wait()


def _moe_sparse_body(te_ref, nv_ref, tok_ref, tokn_ref, h_hbm, wgf_ref, wuf_ref, wdf_ref, y_ref,
                     xbuf0, xbuf1, sem, wg_ref, wu_ref, wd_ref, *, tr, n_col):
    i = pl.program_id(0)
    nv = nv_ref[0]
    slot = i % 2

    new_expert = jnp.logical_or(i == 0, te_ref[i] != te_ref[jnp.maximum(i - 1, 0)])

    @pl.when(jnp.logical_and(i < nv, new_expert))
    def _():
        wg_ref[...] = wgf_ref[0, 0].astype(BF16)
        wu_ref[...] = wuf_ref[0, 0].astype(BF16)
        wd_ref[...] = wdf_ref[0, 0].astype(BF16)

    bufs = (xbuf0, xbuf1)

    @pl.when(i == 0)
    def _():
        _start_row_gather(tok_ref, h_hbm, xbuf0, sem.at[0], tr)

    n_batch = 4 + 2 * n_col
    per = tr // n_batch

    def tile(cur, nxt, sem_cur, sem_nxt):
        def prefetch(b):
            _start_row_gather(tokn_ref, h_hbm, nxt, sem_nxt, (b + 1) * per, first=b * per)

        _wait_row_gather(h_hbm, cur, sem_cur, tr)
        x_lo, x_hi = _unpack_halves(cur[...])
        x_lo, x_hi = x_lo.astype(BF16), x_hi.astype(BF16)
        half = x_lo.shape[1]
        prefetch(0)
        gte = _dot(x_lo, wg_ref[:half, :])
        prefetch(1)
        gte = gte + _dot(x_hi, wg_ref[half:, :])
        prefetch(2)
        up = _dot(x_lo, wu_ref[:half, :])
        prefetch(3)
        up = up + _dot(x_hi, wu_ref[half:, :])
        hid = ((gte * _sigmoid(gte)) * up).astype(BF16)
        cw = half // n_col
        for j in range(n_col):
            prefetch(4 + 2 * j)
            lo = _dot(hid, wd_ref[:, j * cw:(j + 1) * cw])
            prefetch(5 + 2 * j)
            hi = _dot(hid, wd_ref[:, half + j * cw:half + (j + 1) * cw])
            y_ref[:, j * cw:(j + 1) * cw] = pltpu.pack_elementwise([lo, hi], packed_dtype=BF16)

    for par in (0, 1):
        @pl.when(jnp.logical_and(i < nv, slot == par))
        def _(par=par):
            tile(bufs[par], bufs[1 - par], sem.at[par], sem.at[1 - par])

        @pl.when(jnp.logical_and(i == nv, slot == par))
        def _(par=par):
            _wait_row_gather(h_hbm, bufs[par], sem.at[par], tr)

    @pl.when(i >= nv)
    def _():
        y_ref[...] = jnp.zeros_like(y_ref)


def _moe_sparse(h, wg, wu, wd, layer, tok3, tile_expert, n_valid):
    n, d = h.shape
    n_tiles, _, tr = tok3.shape
    dm = 2 * d
    last = lambda i, nv: jnp.minimum(i, nv[0] - 1)
    wspec = lambda w: pl.BlockSpec((1, 1) + w.shape[2:], lambda i, te, nv: (layer, te[last(i, nv)], 0, 0))
    grid_spec = pltpu.PrefetchScalarGridSpec(
        num_scalar_prefetch=2,
        grid=(n_tiles,),
        in_specs=[pl.BlockSpec((1, 1, tr), lambda i, te, nv: (i, 0, 0), memory_space=pltpu.SMEM),
                  pl.BlockSpec((1, 1, tr), lambda i, te, nv: (jnp.minimum(i + 1, n_tiles - 1), 0, 0),
                               memory_space=pltpu.SMEM),
                  pl.BlockSpec(memory_space=pl.ANY),
                  wspec(wg), wspec(wu), wspec(wd)],
        out_specs=pl.BlockSpec((tr, d), lambda i, te, nv: (i, 0)),
        scratch_shapes=[pltpu.VMEM((tr, d), jnp.uint32), pltpu.VMEM((tr, d), jnp.uint32),
                        pltpu.SemaphoreType.DMA((2,)),
                        pltpu.VMEM(wg.shape[2:], BF16), pltpu.VMEM(wu.shape[2:], BF16),
                        pltpu.VMEM(wd.shape[2:], BF16)])
    return pl.pallas_call(
        functools.partial(_moe_sparse_body, tr=tr, n_col=2),
        out_shape=jax.ShapeDtypeStruct((n_tiles * tr, d), jnp.uint32),
        grid_spec=grid_spec,
        compiler_params=_params(("arbitrary",), 60),
        name="moe_sparse",
    )(tile_expert, n_valid, tok3, tok3, h, wg, wu, wd)


def _moe_combine_body(sl_ref, sln_ref, y_hbm, gate_ref, sh_ref, h_ref, g_ref, b_ref, o_ref, ob_ref, buf, sem,
                      *, tm, alpha, n_steps):
    i = pl.program_id(0)
    slot = i % 2
    rows = TOP_K * tm

    @pl.when(i == 0)
    def _():
        _start_row_gather(sl_ref, y_hbm, buf.at[0], sem.at[0], rows)

    @pl.when(i + 1 < n_steps)
    def _():
        _start_row_gather(sln_ref, y_hbm, buf.at[1 - slot], sem.at[1 - slot], rows)

    _wait_row_gather(y_hbm, buf.at[slot], sem.at[slot], rows)
    d = h_ref.shape[1]
    half = d // 2
    acc_lo = sh_ref[:, :half]
    acc_hi = sh_ref[:, half:]
    for k in range(TOP_K):
        lo, hi = _unpack_halves(buf[slot, k * tm:(k + 1) * tm, :])
        gk = gate_ref[:, k:k + 1]
        acc_lo = acc_lo + gk * lo
        acc_hi = acc_hi + gk * hi
    v_lo = alpha * h_ref[:, :half] + acc_lo
    v_hi = alpha * h_ref[:, half:] + acc_hi
    mu = (jnp.sum(v_lo, axis=-1, keepdims=True) + jnp.sum(v_hi, axis=-1, keepdims=True)) / d
    c_lo = v_lo - mu
    c_hi = v_hi - mu
    var = (jnp.sum(c_lo * c_lo, axis=-1, keepdims=True) + jnp.sum(c_hi * c_hi, axis=-1, keepdims=True)) / d
    rs = lax.rsqrt(var + LN_EPS)
    out_lo = c_lo * rs * g_ref[:, :half] + b_ref[:, :half]
    out_hi = c_hi * rs * g_ref[:, half:] + b_ref[:, half:]
    o_ref[:, :half] = out_lo
    o_ref[:, half:] = out_hi
    ob_ref[:, :half] = out_lo.astype(BF16)
    ob_ref[:, half:] = out_hi.astype(BF16)


def _moe_combine(y_sorted, slot3, rout, shared, h, g, b, alpha):
    n, d = h.shape
    n_steps, _, rows = slot3.shape
    tm = rows // TOP_K
    row = lambda i: (i, 0)
    return pl.pallas_call(
        functools.partial(_moe_combine_body, tm=tm, alpha=alpha, n_steps=n_steps),
        out_shape=(jax.ShapeDtypeStruct((n, d), F32), jax.ShapeDtypeStruct((n, d), BF16)),
        grid=(n_steps,),
        in_specs=[pl.BlockSpec((1, 1, rows), lambda i: (i, 0, 0), memory_space=pltpu.SMEM),
                  pl.BlockSpec((1, 1, rows), lambda i: (jnp.minimum(i + 1, n_steps - 1), 0, 0),
                               memory_space=pltpu.SMEM),
                  pl.BlockSpec(memory_space=pl.ANY),
                  pl.BlockSpec((tm, LANES), row),
                  pl.BlockSpec((tm, d), row),
                  pl.BlockSpec((tm, d), row),
                  pl.BlockSpec((1, d), lambda i: (0, 0)),
                  pl.BlockSpec((1, d), lambda i: (0, 0))],
        out_specs=(pl.BlockSpec((tm, d), row), pl.BlockSpec((tm, d), row)),
        scratch_shapes=[pltpu.VMEM((2, rows, d // 2), jnp.uint32), pltpu.SemaphoreType.DMA((2,))],
        compiler_params=_params(("arbitrary",), 48),
        name="moe_combine",
    )(slot3, slot3, y_sorted, rout, shared, h, g.reshape(1, d), b.reshape(1, d))


def _slots_body(r_ref, o_ref, se_ref, run_ref, start_ref, *, tm, tr):
    phase = pl.program_id(0)
    i = pl.program_id(1)
    lane = lax.broadcasted_iota(jnp.int32, (tm, LANES), 1)
    lane_f = lane.astype(F32)
    hits = [r_ref[:, TOP_K + k:TOP_K + k + 1] == lane_f for k in range(TOP_K)]
    onehot = jnp.zeros((tm, LANES), F32)
    for hk in hits:
        onehot = onehot + jnp.where(hk, 1.0, 0.0)

    @pl.when(jnp.logical_and(phase == 0, i == 0))
    def _():
        run_ref[...] = jnp.zeros_like(run_ref)

    @pl.when(phase == 0)
    def _():
        run_ref[...] += jnp.sum(onehot, axis=0, keepdims=True)
        o_ref[...] = jnp.zeros_like(o_ref)
        se_ref[...] = jnp.zeros_like(se_ref)

    @pl.when(jnp.logical_and(phase == 1, i == 0))
    def _():
        tiles = jnp.floor((run_ref[...] + (tr - 1)) * (1.0 / tr))
        r = lax.broadcasted_iota(jnp.int32, (LANES, LANES), 0)
        c = lax.broadcasted_iota(jnp.int32, (LANES, LANES), 1)
        before = jnp.where(r < c, 1.0, 0.0).astype(BF16)
        t8 = jnp.broadcast_to(tiles, (SUBLANES, LANES)).astype(BF16)
        start_tiles = _dot(t8, before)[0:1, :]
        start_ref[...] = start_tiles * tr
        run_ref[...] = jnp.zeros_like(run_ref)

    @pl.when(phase == 1)
    def _():
        rr = lax.broadcasted_iota(jnp.int32, (tm, tm), 0)
        cc = lax.broadcasted_iota(jnp.int32, (tm, tm), 1)
        earlier = jnp.where(rr > cc, 1.0, 0.0).astype(BF16)
        rank = _dot(earlier, onehot.astype(BF16)) + run_ref[...] + start_ref[...]
        out = jnp.zeros((tm, LANES), F32)
        for k, hk in enumerate(hits):
            slot_k = jnp.sum(jnp.where(hk, rank, 0.0), axis=1, keepdims=True)
            out = jnp.where(lane == k, slot_k, out)
        o_ref[...] = out
        run_ref[...] += jnp.sum(onehot, axis=0, keepdims=True)
        tiles = jnp.floor((run_ref[...] + (tr - 1)) * (1.0 / tr))
        se_ref[...] = start_ref[...] + tiles * tr


def _route_slots(rout, tr, tm=512):
    n = rout.shape[0]
    return pl.pallas_call(
        functools.partial(_slots_body, tm=tm, tr=tr),
        out_shape=(jax.ShapeDtypeStruct((n, LANES), F32), jax.ShapeDtypeStruct((1, LANES), F32)),
        grid=(2, n // tm),
        in_specs=[pl.BlockSpec((tm, LANES), lambda p, i: (i, 0))],
        out_specs=(pl.BlockSpec((tm, LANES), lambda p, i: (p * i, 0)),
                   pl.BlockSpec((1, LANES), lambda p, i: (0, 0))),
        scratch_shapes=[pltpu.VMEM((1, LANES), F32), pltpu.VMEM((1, LANES), F32)],
        compiler_params=_params(("arbitrary", "arbitrary"), 32),
        name="moe_slots",
    )(rout)


def _dispatch_plan(rout, n_experts, tr, tm):
    n = rout.shape[0]
    slots_f, seg_end_f = _route_slots(rout, tr)
    slot = slots_f[:, :TOP_K].astype(jnp.int32)
    seg_end = seg_end_f[0, :n_experts].astype(jnp.int32)
    n_tiles = n * TOP_K // tr + n_experts + 1
    tok = jnp.zeros((n_tiles * tr,), jnp.int32).at[slot.reshape(-1)].set(
        jnp.repeat(jnp.arange(n, dtype=jnp.int32), TOP_K), unique_indices=True)
    tile_start = jnp.arange(n_tiles, dtype=jnp.int32) * tr
    tile_expert = jnp.minimum(jnp.sum((seg_end[None, :] <= tile_start[:, None]).astype(jnp.int32), axis=1),
                              n_experts - 1)
    n_valid = (seg_end[-1:] // tr).astype(jnp.int32)
    slot3 = slot.reshape(n // tm, tm, TOP_K).transpose(0, 2, 1).reshape(n // tm, 1, TOP_K * tm)
    return tok.reshape(n_tiles, 1, tr), tile_expert, n_valid, slot3.astype(jnp.int32)


def kernel(x, mem, ev_w_in, ev_pool_w, ev_pool_scale, ev_sinks, ev_w_out, rel_bias, od_w_in, hg_lb_logits, od_hg_norm, od_a_re, od_a_im, od_log_step, od_b_re, od_b_im, od_c_re, od_c_im, od_d_skip, od_w_glu, od_b_glu, od_w_out, xa_wq, xa_wk, xa_wv, xa_wo, moe_router, moe_bias, moe_w_gate, moe_w_up, moe_w_down, sh_w_gate, sh_w_up, sh_w_down, ln_g, ln_b):
    bsz, seq, d = x.shape
    n = bsz * seq
    depth = ln_g.shape[0]
    alpha = (2 * depth) ** 0.25
    memb = mem.astype(BF16).reshape(bsz * mem.shape[1], d)

    h = x.astype(F32).reshape(n, d)
    hb = h.astype(BF16)
    bias_tab = _bias_table(rel_bias)

    for l in range(depth):
        j = l // 2
        if l % 2 == 0:
            pool_w = ev_pool_w[j]
            pool_width = pool_w.shape[0] * pool_w.shape[1]
            n_q = ev_sinks.shape[1]
            n_kv = n_q // SWA_GQ
            z = _mm(hb, ev_w_in[j].astype(BF16), F32)
            z3 = z.reshape(bsz, seq, z.shape[1])
            y_a = _pool(z3, pool_w.astype(BF16), ev_pool_scale[j].astype(F32))
            y_b = _swa(z3, ev_sinks[j], bias_tab, pool_width, n_q, n_kv)
            mix = _mm2(y_a.reshape(n, -1), y_b.reshape(n, -1), ev_w_out[j].astype(BF16), F32)
        else:
            n_heads = hg_lb_logits.shape[1] // HG_DIM
            hgw = n_heads * HG_DIM
            ng, nst = od_a_re.shape[1], od_a_re.shape[2]
            s5w = ng * S5_GROUP
            z = _mm(hb, od_w_in[j].astype(BF16), F32)
            z3 = z.reshape(bsz, seq, z.shape[1])
            y_c = _hgrn(z3, hg_lb_logits, od_hg_norm[j], l, n_heads)
            m_mat, p_mat, q_mat, t_re, t_im = _s5_prep(od_a_re[j], od_a_im[j], od_log_step[j], od_b_re[j],
                                                        od_b_im[j], od_c_re[j], od_c_im[j])
            nck = seq // S5_CHUNK
            u = z3[:, :, 4 * hgw:].astype(BF16).reshape(bsz, nck, S5_CHUNK, ng, S5_GROUP)
            ug = u.transpose(3, 0, 1, 2, 4).reshape(ng, bsz, nck, S5_CHUNK * S5_GROUP)
            yg = _s5_scan(ug, m_mat, p_mat, q_mat, t_re, t_im)
            y_s = yg.reshape(ng, bsz, nck, S5_CHUNK, S5_GROUP).transpose(1, 2, 3, 0, 4).reshape(n, s5w)
            y_d = _glu(y_s, z, (4 * hgw) // s5w, od_d_skip[j], od_w_glu[j], od_b_glu[j])
            mix = _mm2(y_c.reshape(n, hgw), y_d, od_w_out[j].astype(BF16), F32)
        h, hb = _add_ln(h, mix, ln_g[l, 0], ln_b[l, 0], alpha)

        xw = xa_wq.shape[2]
        kmem = _mm(memb, xa_wk[l].astype(BF16), BF16).reshape(bsz, -1, xw)
        vmem = _mm(memb, xa_wv[l].astype(BF16), BF16).reshape(bsz, -1, xw)
        h, hb, hp = _xattn(h, hb, kmem, vmem, xa_wq[l].astype(BF16), xa_wo[l].astype(BF16),
                           ln_g[l, 1], ln_b[l, 1], alpha, seq)

        n_exp = moe_router.shape[1]
        rout = _router(hb, moe_router[l], moe_bias[l])
        tok3, tile_expert, n_valid, slot3 = _dispatch_plan(rout, n_exp, MOE_ROW_TILE, MOE_COMBINE_TOKENS)
        shared = _shared_ffn(hb, sh_w_gate[l].astype(BF16), sh_w_up[l].astype(BF16), sh_w_down[l].astype(BF16))
        y_sorted = _moe_sparse(hp, moe_w_gate.astype(F32), moe_w_up.astype(F32), moe_w_down.astype(F32), l,
                               tok3, tile_expert, n_valid)
        h, hb = _moe_combine(y_sorted, slot3, rout, shared, h, ln_g[l, 2], ln_b[l, 2], alpha)

    return h.reshape(bsz, seq, d)
```
